```python
import jax, jax.numpy as jnp
from jax import lax
import numpy as np

D_MODEL = 1024
BATCH = 16
SEQ = 2048
DEPTH = 2

N_A_LAYERS = DEPTH // 2
N_B_LAYERS = DEPTH - N_A_LAYERS
PLE_DIM = 256
NORM_EPS = 1e-6

GDN_HEADS = 8
GDN_HEAD_DIM = 128
GDN_KEY_DIM = GDN_HEADS * GDN_HEAD_DIM
GDN_VALUE_DIM = GDN_HEADS * GDN_HEAD_DIM
GDN_CONV = 4
GDN_CHUNK = 64
GDN_QKV_DIM = 2 * GDN_KEY_DIM + GDN_VALUE_DIM
GDN_IN_DIM = GDN_QKV_DIM + GDN_VALUE_DIM + 2 * GDN_HEADS

SB_Q_HEADS = 16
SB_KV_HEADS = 4
SB_GROUP = SB_Q_HEADS // SB_KV_HEADS
SB_HEAD_DIM = 64
SB_BLOCK = 128

MOE_GROUPS = 4
MOE_EXPERTS_PER_GROUP = 8
MOE_N_EXPERTS = MOE_GROUPS * MOE_EXPERTS_PER_GROUP
MOE_TOP_K = 2
MOE_D_EXPERT = 256

kernel_name = "yoco_gdn_stickbreak_hmoe"


def rms_norm(x, gain):
    xf = x.astype(jnp.float32)
    y = xf * lax.rsqrt(jnp.mean(xf * xf, axis=-1, keepdims=True) + NORM_EPS) * gain.astype(jnp.float32)
    return y.astype(x.dtype)


def l2_normalize(t):
    return t * lax.rsqrt(jnp.sum(t * t, axis=-1, keepdims=True) + NORM_EPS)


def causal_depthwise_conv(x, w):
    c = x.shape[-1]
    return lax.conv_general_dilated(
        x, w[:, None, :].astype(x.dtype), window_strides=(1,), padding=[(GDN_CONV - 1, 0)],
        dimension_numbers=("NWC", "WIO", "NWC"), feature_group_count=c)


def chunk_gated_delta_rule(q, k, v, g, beta):
    b_, s_, h_, dk = q.shape
    dv = v.shape[-1]
    c = GDN_CHUNK
    n = s_ // c

    def to_chunks(t):
        return t.reshape(b_, n, c, h_, t.shape[-1]).transpose(0, 3, 1, 2, 4)

    q, k, v = to_chunks(q), to_chunks(k), to_chunks(v)
    g = jnp.cumsum(g.reshape(b_, n, c, h_).transpose(0, 3, 1, 2), axis=-1)
    beta = beta.reshape(b_, n, c, h_).transpose(0, 3, 1, 2)
    k_beta = k * beta[..., None]
    v_beta = v * beta[..., None]

    incl = jnp.tril(jnp.ones((c, c), dtype=bool))
    strict = jnp.tril(jnp.ones((c, c), dtype=bool), -1)
    diff = g[..., :, None] - g[..., None, :]
    decay = jnp.where(incl, jnp.exp(jnp.where(incl, diff, 0.0)), 0.0)

    lmat = jnp.where(strict, jnp.einsum("bhncd,bhnsd->bhncs", k_beta, k) * decay, 0.0)
    eye = jnp.broadcast_to(jnp.eye(c, dtype=jnp.float32), lmat.shape)
    tmat = lax.linalg.triangular_solve(eye + lmat, eye, left_side=True, lower=True, unit_diagonal=True)
    u = jnp.einsum("bhncs,bhnse->bhnce", tmat, v_beta)
    w = jnp.einsum("bhncs,bhnsd->bhncd", tmat, k_beta * jnp.exp(g)[..., None])
    attn_qk = jnp.where(incl, jnp.einsum("bhncd,bhnsd->bhncs", q, k) * decay, 0.0)

    def step(state, xs):
        q_i, k_i, u_i, w_i, g_i, a_i = xs
        v_new = u_i - jnp.einsum("bhcd,bhde->bhce", w_i, state)
        o_i = (jnp.einsum("bhcd,bhde->bhce", q_i * jnp.exp(g_i)[..., None], state)
               + jnp.einsum("bhcs,bhse->bhce", a_i, v_new))
        g_last = g_i[..., -1]
        k_dec = k_i * jnp.exp(g_last[..., None] - g_i)[..., None]
        state = state * jnp.exp(g_last)[..., None, None] + jnp.einsum("bhcd,bhce->bhde", k_dec, v_new)
        return state, o_i

    xs = tuple(jnp.moveaxis(t, 2, 0) for t in (q, k, u, w, g, attn_qk))
    state0 = jnp.zeros((b_, h_, dk, dv), jnp.float32)
    _, o = lax.scan(step, state0, xs)
    return o.transpose(1, 0, 3, 2, 4).reshape(b_, s_, h_, dv)


def gated_deltanet(xn, w_in, conv_w, a_log, dt_bias, o_norm, w_out):
    b_, s_, _ = xn.shape
    proj = xn @ w_in
    qkv = jax.nn.silu(causal_depthwise_conv(proj[..., :GDN_QKV_DIM], conv_w)).astype(jnp.float32)
    z = proj[..., GDN_QKV_DIM:GDN_QKV_DIM + GDN_VALUE_DIM]
    a = proj[..., GDN_QKV_DIM + GDN_VALUE_DIM:GDN_QKV_DIM + GDN_VALUE_DIM + GDN_HEADS].astype(jnp.float32)
    bb = proj[..., GDN_QKV_DIM + GDN_VALUE_DIM + GDN_HEADS:].astype(jnp.float32)
    q = qkv[..., :GDN_KEY_DIM].reshape(b_, s_, GDN_HEADS, GDN_HEAD_DIM)
    k = qkv[..., GDN_KEY_DIM:2 * GDN_KEY_DIM].reshape(b_, s_, GDN_HEADS, GDN_HEAD_DIM)
    v = qkv[..., 2 * GDN_KEY_DIM:].reshape(b_, s_, GDN_HEADS, GDN_HEAD_DIM)
    q = l2_normalize(q) * (GDN_HEAD_DIM ** -0.5)
    k = l2_normalize(k)
    beta = jax.nn.sigmoid(bb)
    g = -jnp.exp(a_log.astype(jnp.float32)) * jax.nn.softplus(a + dt_bias.astype(jnp.float32))
    o = chunk_gated_delta_rule(q, k, v, g, beta)
    o = o * lax.rsqrt(jnp.mean(o * o, axis=-1, keepdims=True) + NORM_EPS) * o_norm.astype(jnp.float32)
    o = o * jax.nn.silu(z.astype(jnp.float32)).reshape(b_, s_, GDN_HEADS, GDN_HEAD_DIM)
    return o.reshape(b_, s_, GDN_VALUE_DIM).astype(xn.dtype) @ w_out


def stick_breaking_attention(q, k, v):
    s_ = q.shape[1]
    scale = SB_HEAD_DIM ** -0.5
    outs = []
    for i in range(s_ // SB_BLOCK):
        end = (i + 1) * SB_BLOCK
        q_blk = q[:, i * SB_BLOCK:end].astype(jnp.float32)
        k_pre = k[:, :end].astype(jnp.float32)
        v_pre = v[:, :end].astype(jnp.float32)
        z = jnp.einsum("bqhgd,bkhd->bhgqk", q_blk, k_pre) * scale
        t_pos = i * SB_BLOCK + jnp.arange(SB_BLOCK)
        s_pos = jnp.arange(end)
        mask = s_pos[None, :] < t_pos[:, None]
        log1m = jnp.where(mask, jax.nn.log_sigmoid(-z), 0.0)
        rev = lax.cumsum(log1m, axis=log1m.ndim - 1, reverse=True)
        rev_excl = jnp.concatenate([rev[..., 1:], jnp.zeros_like(rev[..., :1])], axis=-1)
        attn = jnp.where(mask, jnp.exp(jax.nn.log_sigmoid(z) + rev_excl), 0.0)
        outs.append(jnp.einsum("bhgqk,bkhd->bqhgd", attn, v_pre))
    return jnp.concatenate(outs, axis=1)


def stick_breaking_mixer(xn, k, v, w_q, w_out):
    b_, s_, _ = xn.shape
    q = (xn @ w_q).reshape(b_, s_, SB_KV_HEADS, SB_GROUP, SB_HEAD_DIM)
    o = stick_breaking_attention(q, k, v)
    return o.reshape(b_, s_, SB_Q_HEADS * SB_HEAD_DIM).astype(xn.dtype) @ w_out


def hierarchical_moe(xn, w_group, b_group, w_expert, b_expert, w_gate, w_up, w_down):
    b_, s_, d_ = xn.shape
    xf = xn.reshape(b_ * s_, d_)
    t_ = xf.shape[0]
    gprob = jax.nn.softmax((xf @ w_group + b_group).astype(jnp.float32), axis=-1)
    gp, gidx = lax.top_k(gprob, 1)
    elog = (xf @ w_expert + b_expert).astype(jnp.float32).reshape(t_, MOE_GROUPS, MOE_EXPERTS_PER_GROUP)
    elog_sel = jnp.take_along_axis(elog, gidx[:, :, None], axis=1)[:, 0]
    ev, eidx = lax.top_k(elog_sel, MOE_TOP_K)
    ew = jax.nn.softmax(ev, axis=-1) * gp
    flat = gidx * MOE_EXPERTS_PER_GROUP + eidx
    combine = jnp.sum(jax.nn.one_hot(flat, MOE_N_EXPERTS, dtype=jnp.float32) * ew[..., None], axis=1)
    combine = combine.astype(xf.dtype)
    y = jnp.zeros((t_, d_), xf.dtype)
    for gi in range(MOE_GROUPS):
        cw = combine[:, gi * MOE_EXPERTS_PER_GROUP:(gi + 1) * MOE_EXPERTS_PER_GROUP]
        hg = jnp.einsum("td,edf->tef", xf, w_gate[gi])
        hu = jnp.einsum("td,edf->tef", xf, w_up[gi])
        hh = jax.nn.silu(hg) * hu * cw[:, :, None]
        y = y + jnp.einsum("tef,efd->td", hh, w_down[gi])
    return y.reshape(b_, s_, d_)


def setup_inputs(seed: int = 0) -> dict:
    key = jax.random.key(seed)
    ks = iter(jax.random.split(key, 40))

    def nrm(shape, scale):
        return jax.random.normal(next(ks), shape, jnp.float32) * scale

    def gain(shape):
        return 1.0 + nrm(shape, 0.05)

    dt = jnp.exp(jax.random.uniform(next(ks), (N_A_LAYERS, GDN_HEADS), jnp.float32,
                                    float(np.log(1e-3)), float(np.log(1e-1))))
    return {
        "x": nrm((BATCH, SEQ, D_MODEL), 1.0),
        "p": nrm((DEPTH, BATCH, SEQ, PLE_DIM), 1.0),
        "attn_norm": gain((DEPTH, D_MODEL)),
        "moe_norm": gain((DEPTH, D_MODEL)),
        "ple_norm": gain((DEPTH, D_MODEL)),
        "gdn_w_in": nrm((N_A_LAYERS, D_MODEL, GDN_IN_DIM), D_MODEL ** -0.5),
        "gdn_conv": nrm((N_A_LAYERS, GDN_CONV, GDN_QKV_DIM), GDN_CONV ** -0.5),
        "gdn_a_log": jnp.log(jax.random.uniform(next(ks), (N_A_LAYERS, GDN_HEADS), jnp.float32, 1.0, 16.0)),
        "gdn_dt_bias": dt + jnp.log(-jnp.expm1(-dt)),
        "gdn_o_norm": gain((N_A_LAYERS, GDN_HEAD_DIM)),
        "gdn_w_out": nrm((N_A_LAYERS, GDN_VALUE_DIM, D_MODEL), GDN_VALUE_DIM ** -0.5),
        "kv_norm": gain((D_MODEL,)),
        "w_kv": nrm((D_MODEL, 2 * SB_KV_HEADS * SB_HEAD_DIM), D_MODEL ** -0.5),
        "sb_w_q": nrm((N_B_LAYERS, D_MODEL, SB_Q_HEADS * SB_HEAD_DIM), D_MODEL ** -0.5),
        "sb_w_out": nrm((N_B_LAYERS, SB_Q_HEADS * SB_HEAD_DIM, D_MODEL), (SB_Q_HEADS * SB_HEAD_DIM) ** -0.5),
        "moe_w_group": nrm((DEPTH, D_MODEL, MOE_GROUPS), D_MODEL ** -0.5),
        "moe_b_group": nrm((DEPTH, MOE_GROUPS), 0.01),
        "moe_w_expert": nrm((DEPTH, D_MODEL, MOE_N_EXPERTS), D_MODEL ** -0.5),
        "moe_b_expert": nrm((DEPTH, MOE_N_EXPERTS), 0.01),
        "moe_w_gate": nrm((DEPTH, MOE_GROUPS, MOE_EXPERTS_PER_GROUP, D_MODEL, MOE_D_EXPERT), D_MODEL ** -0.5),
        "moe_w_up": nrm((DEPTH, MOE_GROUPS, MOE_EXPERTS_PER_GROUP, D_MODEL, MOE_D_EXPERT), D_MODEL ** -0.5),
        "moe_w_down": nrm((DEPTH, MOE_GROUPS, MOE_EXPERTS_PER_GROUP, MOE_D_EXPERT, D_MODEL), MOE_D_EXPERT ** -0.5),
        "ple_w_gate": nrm((DEPTH, D_MODEL, D_MODEL), D_MODEL ** -0.5),
        "ple_w_proj": nrm((DEPTH, PLE_DIM, D_MODEL), PLE_DIM ** -0.5),
        "final_norm": gain((D_MODEL,)),
    }


def reference(x, p, attn_norm, moe_norm, ple_norm, gdn_w_in, gdn_conv, gdn_a_log, gdn_dt_bias,
              gdn_o_norm, gdn_w_out, kv_norm, w_kv, sb_w_q, sb_w_out, moe_w_group, moe_b_group,
              moe_w_expert, moe_b_expert, moe_w_gate, moe_w_up, moe_w_down, ple_w_gate, ple_w_proj,
              final_norm):
    h = x
    b_, s_, _ = x.shape
    kv_dim = SB_KV_HEADS * SB_HEAD_DIM
    k_shared = None
    v_shared = None
    for i in range(DEPTH):
        if i < N_A_LAYERS:
            a = i
            mix = gated_deltanet(rms_norm(h, attn_norm[i]), gdn_w_in[a], gdn_conv[a], gdn_a_log[a],
                                 gdn_dt_bias[a], gdn_o_norm[a], gdn_w_out[a])
        else:
            if i == N_A_LAYERS:
                kv = rms_norm(h, kv_norm) @ w_kv
                k_shared = kv[..., :kv_dim].reshape(b_, s_, SB_KV_HEADS, SB_HEAD_DIM)
                v_shared = kv[..., kv_dim:].reshape(b_, s_, SB_KV_HEADS, SB_HEAD_DIM)
            bi = i - N_A_LAYERS
            mix = stick_breaking_mixer(rms_norm(h, attn_norm[i]), k_shared, v_shared, sb_w_q[bi], sb_w_out[bi])
        h = h + mix.astype(h.dtype)
        h = h + hierarchical_moe(rms_norm(h, moe_norm[i]), moe_w_group[i], moe_b_group[i], moe_w_expert[i],
                                 moe_b_expert[i], moe_w_gate[i], moe_w_up[i], moe_w_down[i]).astype(h.dtype)
        gate = jax.nn.sigmoid((rms_norm(h, ple_norm[i]) @ ple_w_gate[i]).astype(jnp.float32))
        emb = (p[i] @ ple_w_proj[i]).astype(jnp.float32)
        h = h + (gate * emb).astype(h.dtype)
    return rms_norm(h, final_norm)
```

```python
import functools

import jax
import jax.numpy as jnp
from jax import lax
from jax.experimental import pallas as pl
from jax.experimental.pallas import tpu as pltpu

NORM_EPS = 1e-6
LANES = 128
GDN_HEADS = 8
GDN_HEAD_DIM = 128
GDN_CONV = 4
GDN_CHUNK = 64
SB_Q_HEADS = 16
SB_KV_HEADS = 4
SB_GROUP = SB_Q_HEADS // SB_KV_HEADS
SB_HEAD_DIM = 64
SB_BLOCK = 128
MOE_GROUPS = 4
MOE_EXPERTS_PER_GROUP = 8
MOE_N_EXPERTS = MOE_GROUPS * MOE_EXPERTS_PER_GROUP
MOE_D_EXPERT = 256

VMEM_LIMIT = 56 * 1024 * 1024
TOKEN_TILE = 256
FFN_TILE = 256
GDN_TIME_BLOCK = 512

F32 = jnp.float32
BF16 = jnp.bfloat16


def _params(*sem):
    return pltpu.CompilerParams(dimension_semantics=sem, vmem_limit_bytes=VMEM_LIMIT)


def _dot(a, b):
    return jnp.dot(a.astype(BF16), b.astype(BF16), preferred_element_type=F32)


def _dot_nt(a, b):
    return lax.dot_general(a.astype(BF16), b.astype(BF16), (((1,), (1,)), ((), ())),
                           preferred_element_type=F32)


def _dot_tn(a, b):
    return lax.dot_general(a.astype(BF16), b.astype(BF16), (((0,), (0,)), ((), ())),
                           preferred_element_type=F32)


def _dot_f32(a, b):
    return jnp.dot(a, b, precision=lax.Precision.HIGHEST, preferred_element_type=F32)


def _rms(x, gain):
    return x * lax.rsqrt(jnp.mean(x * x, axis=-1, keepdims=True) + NORM_EPS) * gain


def _silu(x):
    return x * (1.0 / (1.0 + jnp.exp(-x)))


def _sigmoid(x):
    return 1.0 / (1.0 + jnp.exp(-x))


def _softplus(x):
    return jnp.maximum(x, 0.0) + jnp.log(1.0 + jnp.exp(-jnp.abs(x)))


def _norm_matmul_kernel(plan, n_chunk, x_ref, gains_ref, *refs):
    n_w = len(plan)
    w_refs, o_refs = refs[:n_w], refs[n_w:]
    x = x_ref[...]
    inv = lax.rsqrt(jnp.mean(x * x, axis=-1, keepdims=True) + NORM_EPS)
    xn = {}
    for (g, _, _) in plan:
        if g not in xn:
            xn[g] = (x * inv * gains_ref[g:g + 1, :]).astype(BF16)
    for (g, scale, _), w_ref, o_ref in zip(plan, w_refs, o_refs):
        n = w_ref.shape[1]
        for n0 in range(0, n, n_chunk):
            n1 = min(n, n0 + n_chunk)
            acc = jnp.dot(xn[g], w_ref[:, n0:n1], preferred_element_type=F32)
            if scale != 1.0:
                acc = acc * scale
            o_ref[:, n0:n1] = acc.astype(o_ref.dtype)


def _norm_matmul(x, gains, ws, plan):
    t, d = x.shape
    tm = TOKEN_TILE
    in_specs = [pl.BlockSpec((tm, d), lambda i: (i, 0)),
                pl.BlockSpec(gains.shape, lambda i: (0, 0))]
    in_specs += [pl.BlockSpec(w.shape, lambda i: (0, 0)) for w in ws]
    out_specs = [pl.BlockSpec((tm, w.shape[1]), lambda i: (i, 0)) for w in ws]
    out_shape = [jax.ShapeDtypeStruct((t, w.shape[1]), p[2]) for w, p in zip(ws, plan)]
    return pl.pallas_call(
        functools.partial(_norm_matmul_kernel, tuple(plan), 512),
        grid=(t // tm,), in_specs=in_specs, out_specs=out_specs, out_shape=out_shape,
        compiler_params=_params("parallel"), name="norm_matmul",
    )(x, gains, *ws)


def _tri_inverse(lm, row, col):
    eye = (row == col).astype(F32)
    same_blk = (row >> 4) == (col >> 4)
    ld = jnp.where(same_blk, lm, 0.0)
    lo = lm - ld
    p = eye - ld
    sq = _dot(ld, ld)
    p = p + _dot(p, sq)
    sq = _dot(sq, sq)
    p = p + _dot(p, sq)
    sq = _dot(sq, sq)
    dinv = p + _dot(p, sq)
    m = _dot(dinv, lo)
    m2 = _dot(m, m)
    r = (eye - m) + _dot(eye - m, m2)
    return _dot(r, dinv)


def _gdn_kernel(n_heads, ts, qkv_ref, z_ref, ab_ref, conv_ref, alog_ref, dtb_ref, onorm_ref,
                o_ref, xpad_ref, state_ref, g_ref, beta_ref):
    c = GDN_CHUNK
    dk = GDN_HEAD_DIM
    tb = pl.program_id(1)
    n_ch = ts // c
    qkv_dim = qkv_ref.shape[-1]

    @pl.when(tb == 0)
    def _():
        state_ref[...] = jnp.zeros_like(state_ref)
        xpad_ref[0:8, :] = jnp.zeros((8, qkv_dim), F32)

    xpad_ref[8:8 + ts, :] = qkv_ref[0]

    ab = ab_ref[0]
    g_all = -jnp.exp(alog_ref[...]) * _softplus(ab + dtb_ref[...])
    beta_all = _sigmoid(ab)
    rin = lax.broadcasted_iota(jnp.int32, (ts, LANES), 0) & (c - 1)
    for h in range(n_heads):
        gh = jnp.broadcast_to(g_all[:, h:h + 1], (ts, LANES))
        shift = 1
        while shift < c:
            gh = gh + jnp.where(rin >= shift, pltpu.roll(gh, shift, 0), 0.0)
            shift *= 2
        g_ref[h] = gh
        beta_ref[h] = jnp.broadcast_to(beta_all[:, n_heads + h:n_heads + h + 1], (ts, LANES))

    row = lax.broadcasted_iota(jnp.int32, (c, c), 0)
    col = lax.broadcasted_iota(jnp.int32, (c, c), 1)
    incl = row >= col
    strict = row > col

    def conv_silu(r0, c0):
        win = xpad_ref[pl.ds(r0, c + 8), c0:c0 + LANES]
        acc = jnp.zeros((c, LANES), F32)
        for j in range(GDN_CONV):
            off = 8 - (GDN_CONV - 1) + j
            acc = acc + win[off:off + c, :] * conv_ref[j:j + 1, c0:c0 + LANES]
        return _silu(acc)

    def l2n(t):
        return t * lax.rsqrt(jnp.sum(t * t, axis=-1, keepdims=True) + NORM_EPS)

    def chunk_body(n, carry):
        r0 = pl.multiple_of(n * c, c)
        for h in range(n_heads):
            q = l2n(conv_silu(r0, h * dk)) * (dk ** -0.5)
            k = l2n(conv_silu(r0, (n_heads + h) * dk))
            v = conv_silu(r0, (2 * n_heads + h) * dk)
            gc = g_ref[h, pl.ds(r0, c), :]
            beta = beta_ref[h, pl.ds(r0, c), :]
            gc_row = gc.T[0:c, :]
            diff = gc[:, 0:c] - gc_row
            decay = jnp.where(incl, jnp.exp(jnp.where(incl, diff, 0.0)), 0.0)
            kb = k * beta
            vb = v * beta
            lm = jnp.where(strict, _dot_nt(kb, k) * decay, 0.0)
            tmat = _tri_inverse(lm, row, col)
            eg = jnp.exp(gc)
            u = _dot(tmat, vb)
            w = _dot(tmat, kb * eg)
            attn = jnp.where(incl, _dot_nt(q, k) * decay, 0.0)
            g_last = gc[c - 1:c, :]
            s = state_ref[h]
            v_new = u - _dot(w, s)
            o = _dot(q * eg, s) + _dot(attn, v_new)
            k_dec = k * jnp.exp(g_last - gc)
            state_ref[h] = s * jnp.exp(g_last) + _dot_tn(k_dec, v_new)
            o = o * lax.rsqrt(jnp.mean(o * o, axis=-1, keepdims=True) + NORM_EPS) * onorm_ref[...]
            zz = z_ref[0, pl.ds(r0, c), h * dk:(h + 1) * dk]
            o_ref[0, pl.ds(r0, c), h * dk:(h + 1) * dk] = (o * _silu(zz)).astype(o_ref.dtype)
        return carry

    lax.fori_loop(0, n_ch, chunk_body, 0)
    xpad_ref[0:8, :] = xpad_ref[ts:ts + 8, :]


def _gated_deltanet_core(qkv, z, ab, conv_w, alog, dtb, onorm):
    b, s, qkv_dim = qkv.shape
    n_heads = GDN_HEADS
    ts = min(GDN_TIME_BLOCK, s)
    vd = z.shape[-1]
    kern = functools.partial(_gdn_kernel, n_heads, ts)
    return pl.pallas_call(
        kern,
        grid=(b, s // ts),
        in_specs=[
            pl.BlockSpec((1, ts, qkv_dim), lambda i, j: (i, j, 0)),
            pl.BlockSpec((1, ts, vd), lambda i, j: (i, j, 0)),
            pl.BlockSpec((1, ts, LANES), lambda i, j: (i, j, 0)),
            pl.BlockSpec(conv_w.shape, lambda i, j: (0, 0)),
            pl.BlockSpec((1, LANES), lambda i, j: (0, 0)),
            pl.BlockSpec((1, LANES), lambda i, j: (0, 0)),
            pl.BlockSpec((1, GDN_HEAD_DIM), lambda i, j: (0, 0)),
        ],
        out_specs=pl.BlockSpec((1, ts, vd), lambda i, j: (i, j, 0)),
        out_shape=jax.ShapeDtypeStruct((b, s, vd), BF16),
        scratch_shapes=[
            pltpu.VMEM((ts + 8, qkv_dim), F32),
            pltpu.VMEM((n_heads, GDN_HEAD_DIM, GDN_HEAD_DIM), F32),
            pltpu.VMEM((n_heads, ts, LANES), F32),
            pltpu.VMEM((n_heads, ts, LANES), F32),
        ],
        compiler_params=_params("parallel", "arbitrary"), name="gdn_delta_rule",
    )(qkv, z, ab, conv_w, alog, dtb, onorm)


def _sb_attn_kernel(rg, q_ref, kt_ref, v_ref, o_ref, acc_ref, carry_ref):
    blk = SB_BLOCK
    dh = SB_HEAD_DIM
    grp = SB_GROUP
    i = pl.program_id(2)
    rows = grp * blk
    qb = q_ref[0]
    q4 = jnp.concatenate([qb[:, g * dh:(g + 1) * dh] for g in range(grp)], axis=0)
    acc_ref[...] = jnp.zeros_like(acc_ref)
    carry_ref[...] = jnp.zeros_like(carry_ref)
    krow = lax.broadcasted_iota(jnp.int32, (blk, blk), 0)
    kcol = lax.broadcasted_iota(jnp.int32, (blk, blk), 1)
    suffix = (krow > kcol).astype(BF16)
    t_in = lax.broadcasted_iota(jnp.int32, (rg, blk), 0) & (blk - 1)
    s_in = lax.broadcasted_iota(jnp.int32, (rg, blk), 1)

    def body(it, c):
        j = i - it
        kt = kt_ref[0, 0, j]
        vj = v_ref[0, 0, j]
        mask = (j * blk + s_in) < (i * blk + t_in)
        for r0 in range(0, rows, rg):
            z = jnp.dot(q4[r0:r0 + rg], kt, preferred_element_type=F32)
            l1m = jnp.where(mask, -_softplus(z), 0.0)
            hi = l1m.astype(BF16)
            lo = (l1m - hi.astype(F32)).astype(BF16)
            rev = (jnp.dot(hi, suffix, preferred_element_type=F32)
                   + jnp.dot(lo, suffix, preferred_element_type=F32))
            carry = carry_ref[r0:r0 + rg, :]
            p = jnp.where(mask, jnp.exp(z + l1m + rev + carry), 0.0)
            acc_ref[r0:r0 + rg, :] += jnp.dot(p.astype(BF16), vj, preferred_element_type=F32)
            carry_ref[r0:r0 + rg, :] = carry + jnp.sum(l1m, axis=-1, keepdims=True)
        return c

    lax.fori_loop(0, i + 1, body, 0)
    acc = acc_ref[...]
    for g in range(grp):
        o_ref[0, :, g * dh:(g + 1) * dh] = acc[g * blk:(g + 1) * blk, :].astype(o_ref.dtype)


def _stick_breaking_attention(q, kt, v):
    b, s, qd = q.shape
    nb = s // SB_BLOCK
    gw = SB_GROUP * SB_HEAD_DIM
    rows = SB_GROUP * SB_BLOCK
    return pl.pallas_call(
        functools.partial(_sb_attn_kernel, 128),
        grid=(b, SB_KV_HEADS, nb),
        in_specs=[
            pl.BlockSpec((1, SB_BLOCK, gw), lambda bi, g, i: (bi, i, g)),
            pl.BlockSpec((1, 1, nb, SB_HEAD_DIM, SB_BLOCK), lambda bi, g, i: (bi, g, 0, 0, 0)),
            pl.BlockSpec((1, 1, nb, SB_BLOCK, SB_HEAD_DIM), lambda bi, g, i: (bi, g, 0, 0, 0)),
        ],
        out_specs=pl.BlockSpec((1, SB_BLOCK, gw), lambda bi, g, i: (bi, i, g)),
        out_shape=jax.ShapeDtypeStruct((b, s, qd), BF16),
        scratch_shapes=[pltpu.VMEM((rows, SB_HEAD_DIM), F32), pltpu.VMEM((rows, LANES), F32)],
        compiler_params=_params("parallel", "parallel", "arbitrary"), name="stick_breaking_attention",
    )(q, kt, v)


def _out_router_kernel(mix_ref, h_ref, wout_ref, gain_ref, wr_ref, br_ref,
                       h1_ref, xn_ref, rw_ref, ri_ref):
    h1 = h_ref[...] + jnp.dot(mix_ref[...], wout_ref[...], preferred_element_type=F32)
    h1_ref[...] = h1
    xn = _rms(h1, gain_ref[...])
    xn_ref[...] = xn.astype(xn_ref.dtype)
    logits = _dot_f32(xn, wr_ref[...]) + br_ref[...]
    lane = lax.broadcasted_iota(jnp.int32, logits.shape, 1).astype(F32)
    neg = jnp.float32(-jnp.inf)
    big = jnp.float32(1e9)
    gl = jnp.where(lane < MOE_GROUPS, logits, neg)
    gmax = jnp.max(gl, axis=-1, keepdims=True)
    gidx = jnp.min(jnp.where(gl == gmax, lane, big), axis=-1, keepdims=True)
    gp = 1.0 / jnp.sum(jnp.exp(gl - gmax), axis=-1, keepdims=True)
    lo = MOE_GROUPS + gidx * MOE_EXPERTS_PER_GROUP
    el = jnp.where((lane >= lo) & (lane < lo + MOE_EXPERTS_PER_GROUP), logits, neg)
    m1 = jnp.max(el, axis=-1, keepdims=True)
    i1 = jnp.min(jnp.where(el == m1, lane, big), axis=-1, keepdims=True)
    el2 = jnp.where(lane == i1, neg, el)
    m2 = jnp.max(el2, axis=-1, keepdims=True)
    i2 = jnp.min(jnp.where(el2 == m2, lane, big), axis=-1, keepdims=True)
    e2 = jnp.exp(m2 - m1)
    w1 = gp / (1.0 + e2)
    w2 = gp * e2 / (1.0 + e2)
    rw_ref[...] = jnp.where(lane == 0, w1, jnp.where(lane == 1, w2, 0.0))
    ri_ref[...] = jnp.where(lane == 0, i1 - MOE_GROUPS, jnp.where(lane == 1, i2 - MOE_GROUPS, 0.0)).astype(jnp.int32)


def _out_router(mix, h, w_out, gain, w_router, b_router):
    t, d = h.shape
    tm = TOKEN_TILE
    kd = mix.shape[1]
    row = lambda i: (i, 0)
    fix = lambda i: (0, 0)
    return pl.pallas_call(
        _out_router_kernel,
        grid=(t // tm,),
        in_specs=[pl.BlockSpec((tm, kd), row), pl.BlockSpec((tm, d), row),
                  pl.BlockSpec(w_out.shape, fix), pl.BlockSpec((1, d), fix),
                  pl.BlockSpec(w_router.shape, fix), pl.BlockSpec((1, LANES), fix)],
        out_specs=[pl.BlockSpec((tm, d), row), pl.BlockSpec((tm, d), row),
                   pl.BlockSpec((tm, LANES), row), pl.BlockSpec((tm, LANES), row)],
        out_shape=[jax.ShapeDtypeStruct((t, d), F32), jax.ShapeDtypeStruct((t, d), BF16),
                   jax.ShapeDtypeStruct((t, LANES), F32), jax.ShapeDtypeStruct((t, LANES), jnp.int32)],
        compiler_params=_params("parallel"), name="out_proj_router",
    )(mix, h, w_out, gain, w_router, b_router)


def _expert_ffn_kernel(te_ref, nu_ref, xs_ref, cw_ref, wg_ref, wu_ref, wd_ref, ys_ref):
    i = pl.program_id(0)

    @pl.when(i < nu_ref[0])
    def _():
        x = xs_ref[...]
        hg = jnp.dot(x, wg_ref[0], preferred_element_type=F32)
        hu = jnp.dot(x, wu_ref[0], preferred_element_type=F32)
        hh = _silu(hg) * hu * cw_ref[...]
        ys_ref[...] = jnp.dot(hh.astype(BF16), wd_ref[0], preferred_element_type=F32).astype(ys_ref.dtype)

    @pl.when(i >= nu_ref[0])
    def _():
        ys_ref[...] = jnp.zeros_like(ys_ref)


def _expert_ffn(tile_expert, n_used, xs, cw, w_gate, w_up, w_down):
    r, d = xs.shape
    tm = FFN_TILE
    f = w_gate.shape[-1]
    grid_spec = pltpu.PrefetchScalarGridSpec(
        num_scalar_prefetch=2,
        grid=(r // tm,),
        in_specs=[
            pl.BlockSpec((tm, d), lambda i, te, nu: (i, 0)),
            pl.BlockSpec((tm, 1), lambda i, te, nu: (i, 0)),
            pl.BlockSpec((1, d, f), lambda i, te, nu: (te[i], 0, 0)),
            pl.BlockSpec((1, d, f), lambda i, te, nu: (te[i], 0, 0)),
            pl.BlockSpec((1, f, d), lambda i, te, nu: (te[i], 0, 0)),
        ],
        out_specs=pl.BlockSpec((tm, d), lambda i, te, nu: (i, 0)),
    )
    return pl.pallas_call(
        _expert_ffn_kernel, grid_spec=grid_spec,
        out_shape=jax.ShapeDtypeStruct((r, d), BF16),
        compiler_params=_params("arbitrary"), name="expert_ffn",
    )(tile_expert, n_used, xs, cw, w_gate, w_up, w_down)


def _ple_kernel(final, h_ref, m0_ref, m1_ref, p_ref, gain_ref, wg_ref, wp_ref, fgain_ref, o_ref):
    h2 = h_ref[...] + (m0_ref[...].astype(F32) + m1_ref[...].astype(F32))
    xn = _rms(h2, gain_ref[...]).astype(BF16)
    gate = _sigmoid(jnp.dot(xn, wg_ref[...], preferred_element_type=F32))
    emb = jnp.dot(p_ref[...].astype(BF16), wp_ref[...], preferred_element_type=F32)
    h3 = h2 + gate * emb
    if final:
        h3 = _rms(h3, fgain_ref[...])
    o_ref[...] = h3


def _ple(h, m0, m1, p, gain, w_gate, w_proj, final_gain, final):
    t, d = h.shape
    tm = TOKEN_TILE
    pd = p.shape[1]
    row = lambda i: (i, 0)
    fix = lambda i: (0, 0)
    return pl.pallas_call(
        functools.partial(_ple_kernel, final),
        grid=(t // tm,),
        in_specs=[pl.BlockSpec((tm, d), row), pl.BlockSpec((tm, d), row), pl.BlockSpec((tm, d), row),
                  pl.BlockSpec((tm, pd), row), pl.BlockSpec((1, d), fix),
                  pl.BlockSpec(w_gate.shape, fix), pl.BlockSpec(w_proj.shape, fix),
                  pl.BlockSpec((1, d), fix)],
        out_specs=pl.BlockSpec((tm, d), row),
        out_shape=jax.ShapeDtypeStruct((t, d), F32),
        compiler_params=_params("parallel"), name="moe_residual_ple",
    )(h, m0, m1, p, gain, w_gate, w_proj, final_gain)


def _routing_tables(ids, weights):
    t = ids.shape[0]
    tm = FFN_TILE
    n_pairs = 2 * t
    n_rows = n_pairs + MOE_N_EXPERTS * tm
    e_flat = ids.reshape(n_pairs)
    onehot = (e_flat[:, None] == jnp.arange(MOE_N_EXPERTS, dtype=jnp.int32)[None, :]).astype(jnp.int32)
    csum = jnp.cumsum(onehot, axis=0)
    pos = jnp.sum(jnp.where(onehot > 0, csum - 1, 0), axis=1)
    counts = csum[-1]
    padded = ((counts + tm - 1) // tm) * tm
    ends = jnp.cumsum(padded)
    starts = ends - padded
    dest = starts[e_flat] + pos
    src_tok = jnp.zeros((n_rows,), jnp.int32).at[dest].set(jnp.arange(n_pairs, dtype=jnp.int32) // 2)
    cw = jnp.zeros((n_rows,), F32).at[dest].set(weights.reshape(n_pairs))
    tile_start = jnp.arange(n_rows // tm, dtype=jnp.int32) * tm
    tile_expert = jnp.minimum(jnp.searchsorted(ends, tile_start, side="right"),
                              MOE_N_EXPERTS - 1).astype(jnp.int32)
    n_used = (ends[-1] // tm).astype(jnp.int32).reshape(1)
    return src_tok, cw.reshape(n_rows, 1), tile_expert, n_used, dest.reshape(t, 2)


def _moe_ple(mix, h, w_out, moe_gain, w_router, b_router, w_gate, w_up, w_down,
             p, ple_gain, ple_w_gate, ple_w_proj, final_gain, final):
    h1, xn, rw, ri = _out_router(mix, h, w_out, moe_gain, w_router, b_router)
    src_tok, cw, tile_expert, n_used, dest = _routing_tables(ri[:, :2], rw[:, :2])
    xs = jnp.take(xn, src_tok, axis=0)
    ys = _expert_ffn(tile_expert, n_used, xs, cw, w_gate, w_up, w_down)
    m0 = jnp.take(ys, dest[:, 0], axis=0)
    m1 = jnp.take(ys, dest[:, 1], axis=0)
    return _ple(h1, m0, m1, p, ple_gain, ple_w_gate, ple_w_proj, final_gain, final)


def _pad_lanes(v):
    return jnp.zeros((1, LANES), F32).at[0, :v.shape[0]].set(v.astype(F32))


def kernel(x, p, attn_norm, moe_norm, ple_norm, gdn_w_in, gdn_conv, gdn_a_log, gdn_dt_bias, gdn_o_norm, gdn_w_out, kv_norm, w_kv, sb_w_q, sb_w_out, moe_w_group, moe_b_group, moe_w_expert, moe_b_expert, moe_w_gate, moe_w_up, moe_w_down, ple_w_gate, ple_w_proj, final_norm):
    b, s, d = x.shape
    t = b * s
    h = x.reshape(t, d)
    pf = p.reshape(p.shape[0], t, p.shape[-1])
    kd = GDN_HEADS * GDN_HEAD_DIM
    qkv_dim = 3 * kd

    def router_params(i):
        wr = jnp.zeros((d, LANES), F32)
        wr = wr.at[:, :MOE_GROUPS].set(moe_w_group[i])
        wr = wr.at[:, MOE_GROUPS:MOE_GROUPS + MOE_N_EXPERTS].set(moe_w_expert[i])
        br = jnp.zeros((1, LANES), F32)
        br = br.at[0, :MOE_GROUPS].set(moe_b_group[i])
        br = br.at[0, MOE_GROUPS:MOE_GROUPS + MOE_N_EXPERTS].set(moe_b_expert[i])
        return wr, br

    def expert_params(i):
        return (moe_w_gate[i].reshape(MOE_N_EXPERTS, d, MOE_D_EXPERT).astype(BF16),
                moe_w_up[i].reshape(MOE_N_EXPERTS, d, MOE_D_EXPERT).astype(BF16),
                moe_w_down[i].reshape(MOE_N_EXPERTS, MOE_D_EXPERT, d).astype(BF16))

    w_in = gdn_w_in[0]
    w_ab = jnp.zeros((d, LANES), F32).at[:, :2 * GDN_HEADS].set(w_in[:, qkv_dim + kd:])
    qkv, z, ab = _norm_matmul(
        h, attn_norm[0:1],
        [w_in[:, :qkv_dim].astype(BF16), w_in[:, qkv_dim:qkv_dim + kd].astype(BF16), w_ab.astype(BF16)],
        [(0, 1.0, F32), (0, 1.0, F32), (0, 1.0, F32)])
    dtb = jnp.zeros((1, LANES), F32).at[0, :GDN_HEADS].set(gdn_dt_bias[0])
    og = _gated_deltanet_core(
        qkv.reshape(b, s, qkv_dim), z.reshape(b, s, kd), ab.reshape(b, s, LANES),
        gdn_conv[0], _pad_lanes(gdn_a_log[0]), dtb, gdn_o_norm[0].reshape(1, GDN_HEAD_DIM))
    wr, br = router_params(0)
    h = _moe_ple(og.reshape(t, kd), h, gdn_w_out[0].astype(BF16), moe_norm[0:1], wr, br,
                 *expert_params(0), pf[0], ple_norm[0:1], ple_w_gate[0].astype(BF16),
                 ple_w_proj[0].astype(BF16), final_norm.reshape(1, d), False)

    gains = jnp.stack([attn_norm[1], kv_norm], axis=0)
    q, kv = _norm_matmul(h, gains, [sb_w_q[0].astype(BF16), w_kv.astype(BF16)],
                         [(0, SB_HEAD_DIM ** -0.5, BF16), (1, 1.0, BF16)])
    nb = s // SB_BLOCK
    kvw = SB_KV_HEADS * SB_HEAD_DIM
    k5 = kv[:, :kvw].reshape(b, nb, SB_BLOCK, SB_KV_HEADS, SB_HEAD_DIM)
    v5 = kv[:, kvw:].reshape(b, nb, SB_BLOCK, SB_KV_HEADS, SB_HEAD_DIM)
    kt = k5.transpose(0, 3, 1, 4, 2)
    vv = v5.transpose(0, 3, 1, 2, 4)
    oa = _stick_breaking_attention(q.reshape(b, s, -1), kt, vv)
    wr, br = router_params(1)
    out = _moe_ple(oa.reshape(t, -1), h, sb_w_out[0].astype(BF16), moe_norm[1:2], wr, br,
                   *expert_params(1), pf[1], ple_norm[1:2], ple_w_gate[1].astype(BF16),
                   ple_w_proj[1].astype(BF16), final_norm.reshape(1, d), True)
    return out.reshape(b, s, d)
```

```python
import functools

import jax
import jax.numpy as jnp
from jax import lax
from jax.experimental import pallas as pl
from jax.experimental.pallas import tpu as pltpu

NORM_EPS = 1e-6
LOG2E = 1.4426950408889634
LANES = 128
GDN_HEADS = 8
GDN_HEAD_DIM = 128
GDN_CONV = 4
GDN_CHUNK = 64
SB_Q_HEADS = 16
SB_KV_HEADS = 4
SB_GROUP = SB_Q_HEADS // SB_KV_HEADS
SB_HEAD_DIM = 64
SB_BLOCK = 128
MOE_GROUPS = 4
MOE_EXPERTS_PER_GROUP = 8
MOE_N_EXPERTS = MOE_GROUPS * MOE_EXPERTS_PER_GROUP
MOE_D_EXPERT = 256

VMEM_LIMIT = 56 * 1024 * 1024
TOKEN_TILE = 256
FFN_TILE = 256
GDN_TIME_BLOCK = 512

F32 = jnp.float32
BF16 = jnp.bfloat16


def _params(*sem):
    return pltpu.CompilerParams(dimension_semantics=sem, vmem_limit_bytes=VMEM_LIMIT)


def _dot(a, b):
    return jnp.dot(a.astype(BF16), b.astype(BF16), preferred_element_type=F32)


def _dot_nt(a, b):
    return lax.dot_general(a.astype(BF16), b.astype(BF16), (((1,), (1,)), ((), ())),
                           preferred_element_type=F32)


def _dot_tn(a, b):
    return lax.dot_general(a.astype(BF16), b.astype(BF16), (((0,), (0,)), ((), ())),
                           preferred_element_type=F32)


def _dot_f32(a, b):
    return jnp.dot(a, b, precision=lax.Precision.HIGHEST, preferred_element_type=F32)


def _rms(x, gain):
    return x * lax.rsqrt(jnp.mean(x * x, axis=-1, keepdims=True) + NORM_EPS) * gain


def _silu(x):
    return x * (1.0 / (1.0 + jnp.exp(-x)))


def _sigmoid(x):
    return 1.0 / (1.0 + jnp.exp(-x))


def _softplus(x):
    return jnp.maximum(x, 0.0) + jnp.log(1.0 + jnp.exp(-jnp.abs(x)))


def _norm_matmul_kernel(plan, n_chunk, x_ref, gains_ref, *refs):
    n_w = len(plan)
    w_refs, o_refs = refs[:n_w], refs[n_w:]
    x = x_ref[...]
    inv = lax.rsqrt(jnp.mean(x * x, axis=-1, keepdims=True) + NORM_EPS)
    xn = {}
    for (g, _, _) in plan:
        if g not in xn:
            xn[g] = (x * inv * gains_ref[g:g + 1, :]).astype(BF16)
    for (g, scale, _), w_ref, o_ref in zip(plan, w_refs, o_refs):
        n = w_ref.shape[1]
        for n0 in range(0, n, n_chunk):
            n1 = min(n, n0 + n_chunk)
            acc = jnp.dot(xn[g], w_ref[:, n0:n1], preferred_element_type=F32)
            if scale != 1.0:
                acc = acc * scale
            o_ref[:, n0:n1] = acc.astype(o_ref.dtype)


def _norm_matmul(x, gains, ws, plan):
    t, d = x.shape
    tm = TOKEN_TILE
    in_specs = [pl.BlockSpec((tm, d), lambda i: (i, 0)),
                pl.BlockSpec(gains.shape, lambda i: (0, 0))]
    in_specs += [pl.BlockSpec(w.shape, lambda i: (0, 0)) for w in ws]
    out_specs = [pl.BlockSpec((tm, w.shape[1]), lambda i: (i, 0)) for w in ws]
    out_shape = [jax.ShapeDtypeStruct((t, w.shape[1]), p[2]) for w, p in zip(ws, plan)]
    return pl.pallas_call(
        functools.partial(_norm_matmul_kernel, tuple(plan), 512),
        grid=(t // tm,), in_specs=in_specs, out_specs=out_specs, out_shape=out_shape,
        compiler_params=_params("parallel"), name="norm_matmul",
    )(x, gains, *ws)


def _gdn_kernel(n_heads, ts, qkv_ref, z_ref, ab_ref, conv_ref, alog_ref, dtb_ref, onorm_ref,
                o_ref, xpad_ref, state_ref, g_ref, beta_ref):
    c = GDN_CHUNK
    dk = GDN_HEAD_DIM
    tb = pl.program_id(1)
    n_ch = ts // c
    qkv_dim = qkv_ref.shape[-1]

    @pl.when(tb == 0)
    def _():
        state_ref[...] = jnp.zeros_like(state_ref)
        xpad_ref[0:8, :] = jnp.zeros((8, qkv_dim), F32)

    xpad_ref[8:8 + ts, :] = qkv_ref[0]

    ab = ab_ref[0]
    g_all = -jnp.exp(alog_ref[...]) * _softplus(ab + dtb_ref[...])
    beta_all = _sigmoid(ab)
    rin = lax.broadcasted_iota(jnp.int32, (ts, LANES), 0) & (c - 1)
    for h in range(n_heads):
        gh = jnp.broadcast_to(g_all[:, h:h + 1], (ts, LANES))
        shift = 1
        while shift < c:
            gh = gh + jnp.where(rin >= shift, pltpu.roll(gh, shift, 0), 0.0)
            shift *= 2
        g_ref[h] = gh
        beta_ref[h] = jnp.broadcast_to(beta_all[:, n_heads + h:n_heads + h + 1], (ts, LANES))

    row = lax.broadcasted_iota(jnp.int32, (c, c), 0)
    col = lax.broadcasted_iota(jnp.int32, (c, c), 1)
    incl = row >= col
    strict = row > col

    def conv_silu(r0, c0):
        win = xpad_ref[pl.ds(r0, c + 8), c0:c0 + LANES]
        acc = jnp.zeros((c, LANES), F32)
        for j in range(GDN_CONV):
            off = 8 - (GDN_CONV - 1) + j
            acc = acc + win[off:off + c, :] * conv_ref[j:j + 1, c0:c0 + LANES]
        return _silu(acc)

    def l2n(t):
        return t * lax.rsqrt(jnp.sum(t * t, axis=-1, keepdims=True) + NORM_EPS)

    heads = range(n_heads)
    eye = (row == col).astype(F32)
    same_blk = (row >> 4) == (col >> 4)

    def chunk_body(n, carry):
        r0 = pl.multiple_of(n * c, c)
        q = [l2n(conv_silu(r0, h * dk)) * (dk ** -0.5) for h in heads]
        k = [l2n(conv_silu(r0, (n_heads + h) * dk)) for h in heads]
        v = [conv_silu(r0, (2 * n_heads + h) * dk) for h in heads]
        gc = [g_ref[h, pl.ds(r0, c), :] for h in heads]
        beta = [beta_ref[h, pl.ds(r0, c), :] for h in heads]
        kb = [k[h] * beta[h] for h in heads]
        kk = [_dot_nt(kb[h], k[h]) for h in heads]
        qk = [_dot_nt(q[h], k[h]) for h in heads]
        decay = []
        for h in heads:
            diff = gc[h][:, 0:c] - gc[h].T[0:c, :]
            decay.append(jnp.where(incl, jnp.exp(jnp.where(incl, diff, 0.0)), 0.0))
        lm = [jnp.where(strict, kk[h] * decay[h], 0.0) for h in heads]
        attn = [jnp.where(incl, qk[h] * decay[h], 0.0) for h in heads]
        ld = [jnp.where(same_blk, lm[h], 0.0) for h in heads]
        lo = [lm[h] - ld[h] for h in heads]
        p = [eye - ld[h] for h in heads]
        sq = [_dot(ld[h], ld[h]) for h in heads]
        for _ in range(2):
            pn = [_dot(p[h], sq[h]) for h in heads]
            sq2 = [_dot(sq[h], sq[h]) for h in heads]
            p = [p[h] + pn[h] for h in heads]
            sq = sq2
        dinv = [p[h] + _dot(p[h], sq[h]) for h in heads]
        m = [_dot(dinv[h], lo[h]) for h in heads]
        m2 = [_dot(m[h], m[h]) for h in heads]
        r = [(eye - m[h]) + _dot(eye - m[h], m2[h]) for h in heads]
        tmat = [_dot(r[h], dinv[h]) for h in heads]
        eg = [jnp.exp(gc[h]) for h in heads]
        rhs = [jnp.concatenate([v[h] * beta[h], kb[h] * eg[h]], axis=1) for h in heads]
        uw = [_dot(tmat[h], rhs[h]) for h in heads]
        s = [state_ref[h] for h in heads]
        lhs = [jnp.concatenate([uw[h][:, dk:], q[h] * eg[h]], axis=0) for h in heads]
        ws_qs = [_dot(lhs[h], s[h]) for h in heads]
        v_new = [uw[h][:, :dk] - ws_qs[h][:c] for h in heads]
        g_last = [gc[h][c - 1:c, :] for h in heads]
        k_dec = [k[h] * jnp.exp(g_last[h] - gc[h]) for h in heads]
        av = [_dot(attn[h], v_new[h]) for h in heads]
        kv = [_dot_tn(k_dec[h], v_new[h]) for h in heads]
        for h in heads:
            state_ref[h] = s[h] * jnp.exp(g_last[h]) + kv[h]
            o = ws_qs[h][c:] + av[h]
            o = o * lax.rsqrt(jnp.mean(o * o, axis=-1, keepdims=True) + NORM_EPS) * onorm_ref[...]
            zz = z_ref[0, pl.ds(r0, c), h * dk:(h + 1) * dk]
            o_ref[0, pl.ds(r0, c), h * dk:(h + 1) * dk] = (o * _silu(zz)).astype(o_ref.dtype)
        return carry

    lax.fori_loop(0, n_ch, chunk_body, 0)
    xpad_ref[0:8, :] = xpad_ref[ts:ts + 8, :]


def _gated_deltanet_core(qkv, z, ab, conv_w, alog, dtb, onorm):
    b, s, qkv_dim = qkv.shape
    n_heads = GDN_HEADS
    ts = min(GDN_TIME_BLOCK, s)
    vd = z.shape[-1]
    kern = functools.partial(_gdn_kernel, n_heads, ts)
    return pl.pallas_call(
        kern,
        grid=(b, s // ts),
        in_specs=[
            pl.BlockSpec((1, ts, qkv_dim), lambda i, j: (i, j, 0)),
            pl.BlockSpec((1, ts, vd), lambda i, j: (i, j, 0)),
            pl.BlockSpec((1, ts, LANES), lambda i, j: (i, j, 0)),
            pl.BlockSpec(conv_w.shape, lambda i, j: (0, 0)),
            pl.BlockSpec((1, LANES), lambda i, j: (0, 0)),
            pl.BlockSpec((1, LANES), lambda i, j: (0, 0)),
            pl.BlockSpec((1, GDN_HEAD_DIM), lambda i, j: (0, 0)),
        ],
        out_specs=pl.BlockSpec((1, ts, vd), lambda i, j: (i, j, 0)),
        out_shape=jax.ShapeDtypeStruct((b, s, vd), BF16),
        scratch_shapes=[
            pltpu.VMEM((ts + 8, qkv_dim), F32),
            pltpu.VMEM((n_heads, GDN_HEAD_DIM, GDN_HEAD_DIM), F32),
            pltpu.VMEM((n_heads, ts, LANES), F32),
            pltpu.VMEM((n_heads, ts, LANES), F32),
        ],
        compiler_params=_params("parallel", "arbitrary"), name="gdn_delta_rule",
    )(qkv, z, ab, conv_w, alog, dtb, onorm)


def _sb_attn_kernel(q_ref, kt_ref, v_ref, o_ref, acc_ref, cs_ref):
    blk = SB_BLOCK
    dh = SB_HEAD_DIM
    grp = SB_GROUP
    i = pl.program_id(2)
    rows = grp * blk
    qb = q_ref[0]
    q4 = jnp.concatenate([qb[:, g * dh:(g + 1) * dh] for g in range(grp)], axis=0)
    krow = lax.broadcasted_iota(jnp.int32, (blk, blk), 0)
    kcol = lax.broadcasted_iota(jnp.int32, (blk, blk), 1)
    neg_suffix = jnp.where(krow > kcol, -1.0, 0.0).astype(BF16)

    def process(ja, mask_a):
        jb = jnp.maximum(ja - 1, 0)
        valid_b = (ja >= 1).astype(F32)
        kts = [kt_ref[0, 0, ja], kt_ref[0, 0, jb]]
        vs = [v_ref[0, 0, ja], (v_ref[0, 0, jb].astype(F32) * valid_b).astype(BF16)]
        masks = [mask_a, None]
        zs = [jnp.dot(q4, kt, preferred_element_type=F32) for kt in kts]
        sps = []
        for z, mask in zip(zs, masks):
            sp = jnp.maximum(z, 0.0) + jnp.log(1.0 + jnp.exp2(jnp.abs(z) * -LOG2E))
            if mask is not None:
                sp = jnp.where(mask, sp, 0.0)
            sps.append(sp)
        revs = [jnp.dot(sp.astype(BF16), neg_suffix, preferred_element_type=F32) for sp in sps]
        rss = [jnp.sum(sp, axis=-1, keepdims=True) for sp in sps]
        cs = cs_ref[...]
        ps = []
        for z, sp, rev, rs, mask in zip(zs, sps, revs, rss, masks):
            p = jnp.exp2(((z - sp) + rev - cs) * LOG2E)
            if mask is not None:
                p = jnp.where(mask, p, 0.0)
            ps.append(p.astype(BF16))
            cs = cs + rs
        cs_ref[...] = cs
        acc = acc_ref[...]
        for p, v in zip(ps, vs):
            acc = acc + jnp.dot(p, v, preferred_element_type=F32)
        acc_ref[...] = acc

    acc_ref[...] = jnp.zeros_like(acc_ref)
    cs_ref[...] = jnp.zeros_like(cs_ref)
    t_in = lax.broadcasted_iota(jnp.int32, (rows, blk), 0) & (blk - 1)
    s_in = lax.broadcasted_iota(jnp.int32, (rows, blk), 1)
    process(i, s_in < t_in)

    def body(it, c):
        process(i - 2 - 2 * it, None)
        return c

    lax.fori_loop(0, i // 2, body, 0)
    acc = acc_ref[...]
    for g in range(grp):
        o_ref[0, :, g * dh:(g + 1) * dh] = acc[g * blk:(g + 1) * blk, :].astype(o_ref.dtype)


def _stick_breaking_attention(q, kt, v):
    b, s, qd = q.shape
    nb = s // SB_BLOCK
    gw = SB_GROUP * SB_HEAD_DIM
    rows = SB_GROUP * SB_BLOCK
    return pl.pallas_call(
        _sb_attn_kernel,
        grid=(b, SB_KV_HEADS, nb),
        in_specs=[
            pl.BlockSpec((1, SB_BLOCK, gw), lambda bi, g, i: (bi, i, g)),
            pl.BlockSpec((1, 1, nb, SB_HEAD_DIM, SB_BLOCK), lambda bi, g, i: (bi, g, 0, 0, 0)),
            pl.BlockSpec((1, 1, nb, SB_BLOCK, SB_HEAD_DIM), lambda bi, g, i: (bi, g, 0, 0, 0)),
        ],
        out_specs=pl.BlockSpec((1, SB_BLOCK, gw), lambda bi, g, i: (bi, i, g)),
        out_shape=jax.ShapeDtypeStruct((b, s, qd), BF16),
        scratch_shapes=[pltpu.VMEM((rows, SB_HEAD_DIM), F32), pltpu.VMEM((rows, LANES), F32)],
        compiler_params=_params("parallel", "parallel", "arbitrary"), name="stick_breaking_attention",
    )(q, kt, v)


def _out_router_kernel(mix_ref, h_ref, wout_ref, gain_ref, wr_ref, br_ref,
                       h1_ref, xn_ref, rw_ref, ri_ref, cnt_ref, run_ref):
    h1 = h_ref[...] + jnp.dot(mix_ref[...], wout_ref[...], preferred_element_type=F32)
    h1_ref[...] = h1
    xn = _rms(h1, gain_ref[...])
    xn_ref[...] = xn.astype(xn_ref.dtype)
    logits = _dot_f32(xn, wr_ref[...]) + br_ref[...]
    lane = lax.broadcasted_iota(jnp.int32, logits.shape, 1).astype(F32)
    neg = jnp.float32(-jnp.inf)
    big = jnp.float32(1e9)
    gl = jnp.where(lane < MOE_GROUPS, logits, neg)
    gmax = jnp.max(gl, axis=-1, keepdims=True)
    gidx = jnp.min(jnp.where(gl == gmax, lane, big), axis=-1, keepdims=True)
    gp = 1.0 / jnp.sum(jnp.exp(gl - gmax), axis=-1, keepdims=True)
    lo = MOE_GROUPS + gidx * MOE_EXPERTS_PER_GROUP
    el = jnp.where((lane >= lo) & (lane < lo + MOE_EXPERTS_PER_GROUP), logits, neg)
    m1 = jnp.max(el, axis=-1, keepdims=True)
    i1 = jnp.min(jnp.where(el == m1, lane, big), axis=-1, keepdims=True)
    el2 = jnp.where(lane == i1, neg, el)
    m2 = jnp.max(el2, axis=-1, keepdims=True)
    i2 = jnp.min(jnp.where(el2 == m2, lane, big), axis=-1, keepdims=True)
    e2 = jnp.exp(m2 - m1)
    w1 = gp / (1.0 + e2)
    w2 = gp * e2 / (1.0 + e2)
    rw_ref[...] = jnp.where(lane == 0, w1, jnp.where(lane == 1, w2, 0.0))

    @pl.when(pl.program_id(0) == 0)
    def _():
        run_ref[...] = jnp.zeros_like(run_ref)

    tm = logits.shape[0]
    trow = lax.broadcasted_iota(jnp.int32, (tm, tm), 0)
    tcol = lax.broadcasted_iota(jnp.int32, (tm, tm), 1)
    before = jnp.where(tcol < trow, 1.0, 0.0).astype(BF16)
    hot1 = lane == i1
    hot2 = lane == i2
    oh1 = jnp.where(hot1, 1.0, 0.0)
    oh2 = jnp.where(hot2, 1.0, 0.0)
    prior1 = jnp.dot(before, oh1.astype(BF16), preferred_element_type=F32)
    prior2 = jnp.dot(before, oh2.astype(BF16), preferred_element_type=F32)
    cnt1 = jnp.sum(oh1, axis=0, keepdims=True)
    cnt2 = jnp.sum(oh2, axis=0, keepdims=True)
    run = run_ref[...]
    rank1 = jnp.sum(jnp.where(hot1, prior1 + run, 0.0), axis=-1, keepdims=True)
    rank2 = jnp.sum(jnp.where(hot2, prior2 + (run + cnt1), 0.0), axis=-1, keepdims=True)
    run = run + cnt1 + cnt2
    run_ref[...] = run
    cnt_ref[...] = run
    ri = jnp.where(lane == 0, i1 - MOE_GROUPS, jnp.where(lane == 1, i2 - MOE_GROUPS,
                   jnp.where(lane == 2, rank1, jnp.where(lane == 3, rank2, 0.0))))
    ri_ref[...] = ri.astype(jnp.int32)


def _out_router(mix, h, w_out, gain, w_router, b_router):
    t, d = h.shape
    tm = TOKEN_TILE
    kd = mix.shape[1]
    row = lambda i: (i, 0)
    fix = lambda i: (0, 0)
    return pl.pallas_call(
        _out_router_kernel,
        grid=(t // tm,),
        in_specs=[pl.BlockSpec((tm, kd), row), pl.BlockSpec((tm, d), row),
                  pl.BlockSpec(w_out.shape, fix), pl.BlockSpec((1, d), fix),
                  pl.BlockSpec(w_router.shape, fix), pl.BlockSpec((1, LANES), fix)],
        out_specs=[pl.BlockSpec((tm, d), row), pl.BlockSpec((tm, d), row),
                   pl.BlockSpec((tm, LANES), row), pl.BlockSpec((tm, LANES), row),
                   pl.BlockSpec((1, LANES), fix)],
        out_shape=[jax.ShapeDtypeStruct((t, d), F32), jax.ShapeDtypeStruct((t, d), BF16),
                   jax.ShapeDtypeStruct((t, LANES), F32), jax.ShapeDtypeStruct((t, LANES), jnp.int32),
                   jax.ShapeDtypeStruct((1, LANES), F32)],
        scratch_shapes=[pltpu.VMEM((1, LANES), F32)],
        compiler_params=_params("arbitrary"), name="out_proj_router",
    )(mix, h, w_out, gain, w_router, b_router)


def _expert_ffn_kernel(te_ref, nu_ref, xs_ref, wg_ref, wu_ref, wd_ref, ys_ref):
    i = pl.program_id(0)

    @pl.when(i < nu_ref[0])
    def _():
        x = xs_ref[...]
        hg = jnp.dot(x, wg_ref[0].astype(BF16), preferred_element_type=F32)
        hu = jnp.dot(x, wu_ref[0].astype(BF16), preferred_element_type=F32)
        hh = _silu(hg) * hu
        ys_ref[...] = jnp.dot(hh.astype(BF16), wd_ref[0].astype(BF16),
                              preferred_element_type=F32).astype(ys_ref.dtype)

    @pl.when(i >= nu_ref[0])
    def _():
        ys_ref[...] = jnp.zeros_like(ys_ref)


def _expert_ffn(tile_expert, n_used, xs, w_gate, w_up, w_down):
    r, d = xs.shape
    tm = FFN_TILE
    f = w_gate.shape[-1]
    grid_spec = pltpu.PrefetchScalarGridSpec(
        num_scalar_prefetch=2,
        grid=(r // tm,),
        in_specs=[
            pl.BlockSpec((tm, d), lambda i, te, nu: (i, 0)),
            pl.BlockSpec((1, d, f), lambda i, te, nu: (te[i], 0, 0)),
            pl.BlockSpec((1, d, f), lambda i, te, nu: (te[i], 0, 0)),
            pl.BlockSpec((1, f, d), lambda i, te, nu: (te[i], 0, 0)),
        ],
        out_specs=pl.BlockSpec((tm, d), lambda i, te, nu: (i, 0)),
    )
    return pl.pallas_call(
        _expert_ffn_kernel, grid_spec=grid_spec,
        out_shape=jax.ShapeDtypeStruct((r, d), BF16),
        compiler_params=_params("arbitrary"), name="expert_ffn",
    )(tile_expert, n_used, xs, w_gate, w_up, w_down)


def _ple_kernel(final, h_ref, m0_ref, m1_ref, rw_ref, p_ref, gain_ref, wg_ref, wp_ref, fgain_ref, o_ref):
    rw = rw_ref[...]
    h2 = h_ref[...] + (rw[:, 0:1] * m0_ref[...].astype(F32) + rw[:, 1:2] * m1_ref[...].astype(F32))
    xn = _rms(h2, gain_ref[...]).astype(BF16)
    gate = _sigmoid(jnp.dot(xn, wg_ref[...], preferred_element_type=F32))
    emb = jnp.dot(p_ref[...].astype(BF16), wp_ref[...], preferred_element_type=F32)
    h3 = h2 + gate * emb
    if final:
        h3 = _rms(h3, fgain_ref[...])
    o_ref[...] = h3


def _ple(h, m0, m1, rw, p, gain, w_gate, w_proj, final_gain, final):
    t, d = h.shape
    tm = TOKEN_TILE
    pd = p.shape[1]
    row = lambda i: (i, 0)
    fix = lambda i: (0, 0)
    return pl.pallas_call(
        functools.partial(_ple_kernel, final),
        grid=(t // tm,),
        in_specs=[pl.BlockSpec((tm, d), row), pl.BlockSpec((tm, d), row), pl.BlockSpec((tm, d), row),
                  pl.BlockSpec((tm, LANES), row), pl.BlockSpec((tm, pd), row), pl.BlockSpec((1, d), fix),
                  pl.BlockSpec(w_gate.shape, fix), pl.BlockSpec(w_proj.shape, fix),
                  pl.BlockSpec((1, d), fix)],
        out_specs=pl.BlockSpec((tm, d), row),
        out_shape=jax.ShapeDtypeStruct((t, d), F32),
        compiler_params=_params("parallel"), name="moe_residual_ple",
    )(h, m0, m1, rw, p, gain, w_gate, w_proj, final_gain)


def _routing_tables(ri, counts):
    t = ri.shape[0]
    tm = FFN_TILE
    n_rows = 2 * t + MOE_N_EXPERTS * tm
    padded = ((counts + tm - 1) // tm) * tm
    ends = jnp.cumsum(padded)
    starts = ends - padded
    ids = ri[:, 0:2]
    onehot = ids[:, :, None] == jnp.arange(MOE_N_EXPERTS, dtype=jnp.int32)[None, None, :]
    dest = ri[:, 2:4] + jnp.sum(jnp.where(onehot, starts[None, None, :], 0), axis=-1)
    tok = jnp.broadcast_to(jnp.arange(t, dtype=jnp.int32)[:, None], (t, 2))
    src_tok = jnp.zeros((n_rows,), jnp.int32).at[dest.reshape(-1)].set(
        tok.reshape(-1), unique_indices=True, indices_are_sorted=False)
    tile_start = jnp.arange(n_rows // tm, dtype=jnp.int32) * tm
    tile_expert = jnp.minimum(jnp.searchsorted(ends, tile_start, side="right"),
                              MOE_N_EXPERTS - 1).astype(jnp.int32)
    n_used = (ends[-1] // tm).astype(jnp.int32).reshape(1)
    return src_tok, tile_expert, n_used, dest


def _moe_ple(mix, h, w_out, moe_gain, w_router, b_router, w_gate, w_up, w_down,
             p, ple_gain, ple_w_gate, ple_w_proj, final_gain, final):
    h1, xn, rw, ri, cnt = _out_router(mix, h, w_out, moe_gain, w_router, b_router)
    counts = cnt[0, MOE_GROUPS:MOE_GROUPS + MOE_N_EXPERTS].astype(jnp.int32)
    src_tok, tile_expert, n_used, dest = _routing_tables(ri, counts)
    xs = jnp.take(xn, src_tok, axis=0)
    ys = _expert_ffn(tile_expert, n_used, xs, w_gate, w_up, w_down)
    m0 = jnp.take(ys, dest[:, 0], axis=0)
    m1 = jnp.take(ys, dest[:, 1], axis=0)
    return _ple(h1, m0, m1, rw, p, ple_gain, ple_w_gate, ple_w_proj, final_gain, final)


def _pad_lanes(v):
    return jnp.zeros((1, LANES), F32).at[0, :v.shape[0]].set(v.astype(F32))


def kernel(x, p, attn_norm, moe_norm, ple_norm, gdn_w_in, gdn_conv, gdn_a_log, gdn_dt_bias, gdn_o_norm, gdn_w_out, kv_norm, w_kv, sb_w_q, sb_w_out, moe_w_group, moe_b_group, moe_w_expert, moe_b_expert, moe_w_gate, moe_w_up, moe_w_down, ple_w_gate, ple_w_proj, final_norm):
    b, s, d = x.shape
    t = b * s
    h = x.reshape(t, d)
    pf = p.reshape(p.shape[0], t, p.shape[-1])
    kd = GDN_HEADS * GDN_HEAD_DIM
    qkv_dim = 3 * kd

    def router_params(i):
        wr = jnp.zeros((d, LANES), F32)
        wr = wr.at[:, :MOE_GROUPS].set(moe_w_group[i])
        wr = wr.at[:, MOE_GROUPS:MOE_GROUPS + MOE_N_EXPERTS].set(moe_w_expert[i])
        br = jnp.zeros((1, LANES), F32)
        br = br.at[0, :MOE_GROUPS].set(moe_b_group[i])
        br = br.at[0, MOE_GROUPS:MOE_GROUPS + MOE_N_EXPERTS].set(moe_b_expert[i])
        return wr, br

    def expert_params(i):
        return (moe_w_gate[i].reshape(MOE_N_EXPERTS, d, MOE_D_EXPERT),
                moe_w_up[i].reshape(MOE_N_EXPERTS, d, MOE_D_EXPERT),
                moe_w_down[i].reshape(MOE_N_EXPERTS, MOE_D_EXPERT, d))

    w_in = gdn_w_in[0]
    w_ab = jnp.zeros((d, LANES), F32).at[:, :2 * GDN_HEADS].set(w_in[:, qkv_dim + kd:])
    qkv, z, ab = _norm_matmul(
        h, attn_norm[0:1],
        [w_in[:, :qkv_dim].astype(BF16), w_in[:, qkv_dim:qkv_dim + kd].astype(BF16), w_ab.astype(BF16)],
        [(0, 1.0, F32), (0, 1.0, F32), (0, 1.0, F32)])
    dtb = jnp.zeros((1, LANES), F32).at[0, :GDN_HEADS].set(gdn_dt_bias[0])
    og = _gated_deltanet_core(
        qkv.reshape(b, s, qkv_dim), z.reshape(b, s, kd), ab.reshape(b, s, LANES),
        gdn_conv[0], _pad_lanes(gdn_a_log[0]), dtb, gdn_o_norm[0].reshape(1, GDN_HEAD_DIM))
    wr, br = router_params(0)
    h = _moe_ple(og.reshape(t, kd), h, gdn_w_out[0].astype(BF16), moe_norm[0:1], wr, br,
                 *expert_params(0), pf[0], ple_norm[0:1], ple_w_gate[0].astype(BF16),
                 ple_w_proj[0].astype(BF16), final_norm.reshape(1, d), False)

    gains = jnp.stack([attn_norm[1], kv_norm], axis=0)
    q, kv = _norm_matmul(h, gains, [sb_w_q[0].astype(BF16), w_kv.astype(BF16)],
                         [(0, SB_HEAD_DIM ** -0.5, BF16), (1, 1.0, BF16)])
    nb = s // SB_BLOCK
    kvw = SB_KV_HEADS * SB_HEAD_DIM
    k5 = kv[:, :kvw].reshape(b, nb, SB_BLOCK, SB_KV_HEADS, SB_HEAD_DIM)
    v5 = kv[:, kvw:].reshape(b, nb, SB_BLOCK, SB_KV_HEADS, SB_HEAD_DIM)
    kt = k5.transpose(0, 3, 1, 4, 2)
    vv = v5.transpose(0, 3, 1, 2, 4)
    oa = _stick_breaking_attention(q.reshape(b, s, -1), kt, vv)
    wr, br = router_params(1)
    out = _moe_ple(oa.reshape(t, -1), h, sb_w_out[0].astype(BF16), moe_norm[1:2], wr, br,
                   *expert_params(1), pf[1], ple_norm[1:2], ple_w_gate[1].astype(BF16),
                   ple_w_proj[1].astype(BF16), final_norm.reshape(1, d), True)
    return out.reshape(b, s, d)
```

```python
import functools

import jax
import jax.numpy as jnp
from jax import lax
from jax.experimental import pallas as pl
from jax.experimental.pallas import tpu as pltpu

NORM_EPS = 1e-6
LOG2E = 1.4426950408889634
LANES = 128
GDN_HEADS = 8
GDN_HEAD_DIM = 128
GDN_CONV = 4
GDN_CHUNK = 64
SB_Q_HEADS = 16
SB_KV_HEADS = 4
SB_GROUP = SB_Q_HEADS // SB_KV_HEADS
SB_HEAD_DIM = 64
SB_BLOCK = 128
MOE_GROUPS = 4
MOE_EXPERTS_PER_GROUP = 8
MOE_N_EXPERTS = MOE_GROUPS * MOE_EXPERTS_PER_GROUP
MOE_D_EXPERT = 256

VMEM_LIMIT = 56 * 1024 * 1024
TOKEN_TILE = 256
FFN_TILE = 256
GDN_TIME_BLOCK = 512
ROUTE_COLS = 8
BATCH_GROUPS = 2

F32 = jnp.float32
BF16 = jnp.bfloat16


def _params(*sem):
    return pltpu.CompilerParams(dimension_semantics=sem, vmem_limit_bytes=VMEM_LIMIT)


def _dot(a, b):
    return jnp.dot(a.astype(BF16), b.astype(BF16), preferred_element_type=F32)


def _dot_nt(a, b):
    return lax.dot_general(a.astype(BF16), b.astype(BF16), (((1,), (1,)), ((), ())),
                           preferred_element_type=F32)


def _dot_tn(a, b):
    return lax.dot_general(a.astype(BF16), b.astype(BF16), (((0,), (0,)), ((), ())),
                           preferred_element_type=F32)


def _dot_f32(a, b):
    return jnp.dot(a, b, precision=lax.Precision.HIGHEST, preferred_element_type=F32)


def _rms(x, gain):
    return x * lax.rsqrt(jnp.mean(x * x, axis=-1, keepdims=True) + NORM_EPS) * gain


def _silu(x):
    return x * (1.0 / (1.0 + jnp.exp(-x)))


def _sigmoid(x):
    return 1.0 / (1.0 + jnp.exp(-x))


def _softplus(x):
    return jnp.maximum(x, 0.0) + jnp.log(1.0 + jnp.exp(-jnp.abs(x)))


def _norm_matmul_kernel(plan, n_chunk, x_ref, gains_ref, *refs):
    n_w = len(plan)
    w_refs, o_refs = refs[:n_w], refs[n_w:]
    x = x_ref[...]
    inv = lax.rsqrt(jnp.mean(x * x, axis=-1, keepdims=True) + NORM_EPS)
    xn = {}
    for (g, _, _) in plan:
        if g not in xn:
            xn[g] = (x * inv * gains_ref[g:g + 1, :]).astype(BF16)
    for (g, scale, _), w_ref, o_ref in zip(plan, w_refs, o_refs):
        n = w_ref.shape[1]
        for n0 in range(0, n, n_chunk):
            n1 = min(n, n0 + n_chunk)
            acc = jnp.dot(xn[g], w_ref[:, n0:n1], preferred_element_type=F32)
            if scale != 1.0:
                acc = acc * scale
            o_ref[:, n0:n1] = acc.astype(o_ref.dtype)


def _norm_matmul(x, gains, ws, plan):
    t, d = x.shape
    tm = TOKEN_TILE
    in_specs = [pl.BlockSpec((tm, d), lambda i: (i, 0)),
                pl.BlockSpec(gains.shape, lambda i: (0, 0))]
    in_specs += [pl.BlockSpec(w.shape, lambda i: (0, 0)) for w in ws]
    out_specs = [pl.BlockSpec((tm, w.shape[1]), lambda i: (i, 0)) for w in ws]
    out_shape = [jax.ShapeDtypeStruct((t, w.shape[1]), p[2]) for w, p in zip(ws, plan)]
    return pl.pallas_call(
        functools.partial(_norm_matmul_kernel, tuple(plan), 512),
        grid=(t // tm,), in_specs=in_specs, out_specs=out_specs, out_shape=out_shape,
        compiler_params=_params("parallel"), name="norm_matmul",
    )(x, gains, *ws)


def _gdn_kernel(n_heads, ts, qkv_ref, z_ref, ab_ref, conv_ref, alog_ref, dtb_ref, onorm_ref,
                o_ref, xpad_ref, state_ref, g_ref, beta_ref):
    c = GDN_CHUNK
    dk = GDN_HEAD_DIM
    tb = pl.program_id(1)
    n_ch = ts // c
    qkv_dim = qkv_ref.shape[-1]

    @pl.when(tb == 0)
    def _():
        state_ref[...] = jnp.zeros_like(state_ref)
        xpad_ref[0:8, :] = jnp.zeros((8, qkv_dim), F32)

    xpad_ref[8:8 + ts, :] = qkv_ref[0]

    ab = ab_ref[0]
    g_all = -jnp.exp(alog_ref[...]) * _softplus(ab + dtb_ref[...])
    beta_all = _sigmoid(ab)
    rin = lax.broadcasted_iota(jnp.int32, (ts, LANES), 0) & (c - 1)
    for h in range(n_heads):
        gh = jnp.broadcast_to(g_all[:, h:h + 1], (ts, LANES))
        shift = 1
        while shift < c:
            gh = gh + jnp.where(rin >= shift, pltpu.roll(gh, shift, 0), 0.0)
            shift *= 2
        g_ref[h] = gh
        beta_ref[h] = jnp.broadcast_to(beta_all[:, n_heads + h:n_heads + h + 1], (ts, LANES))

    row = lax.broadcasted_iota(jnp.int32, (c, c), 0)
    col = lax.broadcasted_iota(jnp.int32, (c, c), 1)
    incl = row >= col
    strict = row > col

    def conv_silu(r0, c0):
        win = xpad_ref[pl.ds(r0, c + 8), c0:c0 + LANES]
        acc = jnp.zeros((c, LANES), F32)
        for j in range(GDN_CONV):
            off = 8 - (GDN_CONV - 1) + j
            acc = acc + win[off:off + c, :] * conv_ref[j:j + 1, c0:c0 + LANES]
        return _silu(acc)

    def l2n(t):
        return t * lax.rsqrt(jnp.sum(t * t, axis=-1, keepdims=True) + NORM_EPS)

    heads = range(n_heads)
    eye = (row == col).astype(F32)
    same_blk = (row >> 4) == (col >> 4)

    def chunk_body(n, carry):
        r0 = pl.multiple_of(n * c, c)
        q = [l2n(conv_silu(r0, h * dk)) * (dk ** -0.5) for h in heads]
        k = [l2n(conv_silu(r0, (n_heads + h) * dk)) for h in heads]
        v = [conv_silu(r0, (2 * n_heads + h) * dk) for h in heads]
        gc = [g_ref[h, pl.ds(r0, c), :] for h in heads]
        beta = [beta_ref[h, pl.ds(r0, c), :] for h in heads]
        kb = [k[h] * beta[h] for h in heads]
        kk = [_dot_nt(kb[h], k[h]) for h in heads]
        qk = [_dot_nt(q[h], k[h]) for h in heads]
        decay = []
        for h in heads:
            diff = gc[h][:, 0:c] - gc[h].T[0:c, :]
            decay.append(jnp.where(incl, jnp.exp(jnp.where(incl, diff, 0.0)), 0.0))
        lm = [jnp.where(strict, kk[h] * decay[h], 0.0) for h in heads]
        attn = [jnp.where(incl, qk[h] * decay[h], 0.0) for h in heads]
        ld = [jnp.where(same_blk, lm[h], 0.0) for h in heads]
        lo = [lm[h] - ld[h] for h in heads]
        p = [eye - ld[h] for h in heads]
        sq = [_dot(ld[h], ld[h]) for h in heads]
        for _ in range(2):
            pn = [_dot(p[h], sq[h]) for h in heads]
            sq2 = [_dot(sq[h], sq[h]) for h in heads]
            p = [p[h] + pn[h] for h in heads]
            sq = sq2
        dinv = [p[h] + _dot(p[h], sq[h]) for h in heads]
        m = [_dot(dinv[h], lo[h]) for h in heads]
        m2 = [_dot(m[h], m[h]) for h in heads]
        r = [(eye - m[h]) + _dot(eye - m[h], m2[h]) for h in heads]
        tmat = [_dot(r[h], dinv[h]) for h in heads]
        eg = [jnp.exp(gc[h]) for h in heads]
        rhs = [jnp.concatenate([v[h] * beta[h], kb[h] * eg[h]], axis=1) for h in heads]
        uw = [_dot(tmat[h], rhs[h]) for h in heads]
        s = [state_ref[h] for h in heads]
        lhs = [jnp.concatenate([uw[h][:, dk:], q[h] * eg[h]], axis=0) for h in heads]
        ws_qs = [_dot(lhs[h], s[h]) for h in heads]
        v_new = [uw[h][:, :dk] - ws_qs[h][:c] for h in heads]
        g_last = [gc[h][c - 1:c, :] for h in heads]
        k_dec = [k[h] * jnp.exp(g_last[h] - gc[h]) for h in heads]
        av = [_dot(attn[h], v_new[h]) for h in heads]
        kv = [_dot_tn(k_dec[h], v_new[h]) for h in heads]
        for h in heads:
            state_ref[h] = s[h] * jnp.exp(g_last[h]) + kv[h]
            o = ws_qs[h][c:] + av[h]
            o = o * lax.rsqrt(jnp.mean(o * o, axis=-1, keepdims=True) + NORM_EPS) * onorm_ref[...]
            zz = z_ref[0, pl.ds(r0, c), h * dk:(h + 1) * dk]
            o_ref[0, pl.ds(r0, c), h * dk:(h + 1) * dk] = (o * _silu(zz)).astype(o_ref.dtype)
        return carry

    lax.fori_loop(0, n_ch, chunk_body, 0)
    xpad_ref[0:8, :] = xpad_ref[ts:ts + 8, :]


def _gated_deltanet_core(qkv, z, ab, conv_w, alog, dtb, onorm):
    b, s, qkv_dim = qkv.shape
    n_heads = GDN_HEADS
    ts = min(GDN_TIME_BLOCK, s)
    vd = z.shape[-1]
    kern = functools.partial(_gdn_kernel, n_heads, ts)
    return pl.pallas_call(
        kern,
        grid=(b, s // ts),
        in_specs=[
            pl.BlockSpec((1, ts, qkv_dim), lambda i, j: (i, j, 0)),
            pl.BlockSpec((1, ts, vd), lambda i, j: (i, j, 0)),
            pl.BlockSpec((1, ts, LANES), lambda i, j: (i, j, 0)),
            pl.BlockSpec(conv_w.shape, lambda i, j: (0, 0)),
            pl.BlockSpec((1, LANES), lambda i, j: (0, 0)),
            pl.BlockSpec((1, LANES), lambda i, j: (0, 0)),
            pl.BlockSpec((1, GDN_HEAD_DIM), lambda i, j: (0, 0)),
        ],
        out_specs=pl.BlockSpec((1, ts, vd), lambda i, j: (i, j, 0)),
        out_shape=jax.ShapeDtypeStruct((b, s, vd), BF16),
        scratch_shapes=[
            pltpu.VMEM((ts + 8, qkv_dim), F32),
            pltpu.VMEM((n_heads, GDN_HEAD_DIM, GDN_HEAD_DIM), F32),
            pltpu.VMEM((n_heads, ts, LANES), F32),
            pltpu.VMEM((n_heads, ts, LANES), F32),
        ],
        compiler_params=_params("parallel", "arbitrary"), name="gdn_delta_rule",
    )(qkv, z, ab, conv_w, alog, dtb, onorm)


def _sb_attn_kernel(q_ref, kt_ref, v_ref, o_ref, acc_ref, cs_ref, z_ref, p_ref):
    blk = SB_BLOCK
    dh = SB_HEAD_DIM
    grp = SB_GROUP
    i = pl.program_id(2)
    rows = grp * blk
    n_pairs = 1 + i // 2
    qb = q_ref[0]
    q4 = jnp.concatenate([qb[:, g * dh:(g + 1) * dh] for g in range(grp)], axis=0)
    krow = lax.broadcasted_iota(jnp.int32, (blk, blk), 0)
    kcol = lax.broadcasted_iota(jnp.int32, (blk, blk), 1)
    neg_suffix = jnp.where(krow >= kcol, -1.0, 0.0).astype(BF16)

    def pair_blocks(k):
        ja = i - 2 * k
        return jnp.maximum(ja, 0), jnp.maximum(ja - 1, 0), (ja >= 1).astype(F32)

    def logits(k):
        ja, jb, _ = pair_blocks(k)
        z_ref[0] = jnp.dot(q4, kt_ref[0, 0, ja], preferred_element_type=F32)
        z_ref[1] = jnp.dot(q4, kt_ref[0, 0, jb], preferred_element_type=F32)

    def weights(mask_a):
        masks = [mask_a, None]
        zs = [z_ref[0], z_ref[1]]
        sps = []
        for z, mask in zip(zs, masks):
            sp = jnp.maximum(z, 0.0) + jnp.log(1.0 + jnp.exp2(jnp.abs(z) * -LOG2E))
            if mask is not None:
                sp = jnp.where(mask, sp, 0.0)
            sps.append(sp)
        sufs = [jnp.dot(sp.astype(BF16), neg_suffix, preferred_element_type=F32) for sp in sps]
        rss = [jnp.sum(sp, axis=-1, keepdims=True) for sp in sps]
        cs = cs_ref[...]
        for n, (z, suf, rs, mask) in enumerate(zip(zs, sufs, rss, masks)):
            p = jnp.exp2(((z + suf) - cs) * LOG2E)
            if mask is not None:
                p = jnp.where(mask, p, 0.0)
            p_ref[n] = p.astype(BF16)
            cs = cs + rs
        cs_ref[...] = cs

    def weighted_values(k):
        ja, jb, valid_b = pair_blocks(k)
        vb = (v_ref[0, 0, jb].astype(F32) * valid_b).astype(BF16)
        acc_ref[...] += (jnp.dot(p_ref[0], v_ref[0, 0, ja], preferred_element_type=F32)
                         + jnp.dot(p_ref[1], vb, preferred_element_type=F32))

    acc_ref[...] = jnp.zeros_like(acc_ref)
    cs_ref[...] = jnp.zeros_like(cs_ref)
    t_in = lax.broadcasted_iota(jnp.int32, (rows, blk), 0) & (blk - 1)
    s_in = lax.broadcasted_iota(jnp.int32, (rows, blk), 1)
    logits(0)
    weights(s_in < t_in)
    logits(1)

    def body(k, c):
        weighted_values(k - 1)
        weights(None)
        logits(k + 1)
        return c

    lax.fori_loop(1, n_pairs, body, 0)
    weighted_values(n_pairs - 1)
    acc = acc_ref[...]
    for g in range(grp):
        o_ref[0, :, g * dh:(g + 1) * dh] = acc[g * blk:(g + 1) * blk, :].astype(o_ref.dtype)


def _stick_breaking_attention(q, kt, v):
    b, s, qd = q.shape
    nb = s // SB_BLOCK
    gw = SB_GROUP * SB_HEAD_DIM
    rows = SB_GROUP * SB_BLOCK
    return pl.pallas_call(
        _sb_attn_kernel,
        grid=(b, SB_KV_HEADS, nb),
        in_specs=[
            pl.BlockSpec((1, SB_BLOCK, gw), lambda bi, g, i: (bi, i, g)),
            pl.BlockSpec((1, 1, nb, SB_HEAD_DIM, SB_BLOCK), lambda bi, g, i: (bi, g, 0, 0, 0)),
            pl.BlockSpec((1, 1, nb, SB_BLOCK, SB_HEAD_DIM), lambda bi, g, i: (bi, g, 0, 0, 0)),
        ],
        out_specs=pl.BlockSpec((1, SB_BLOCK, gw), lambda bi, g, i: (bi, i, g)),
        out_shape=jax.ShapeDtypeStruct((b, s, qd), BF16),
        scratch_shapes=[pltpu.VMEM((rows, SB_HEAD_DIM), F32), pltpu.VMEM((rows, LANES), F32),
                        pltpu.VMEM((2, rows, SB_BLOCK), F32), pltpu.VMEM((2, rows, SB_BLOCK), BF16)],
        compiler_params=_params("parallel", "parallel", "arbitrary"), name="stick_breaking_attention",
    )(q, kt, v)


def _out_router_kernel(mix_ref, h_ref, wout_ref, gain_ref, wrh_ref, wrl_ref, br_ref,
                       h1_ref, xn_ref, rw_ref, ri_ref, cnt_ref, run_ref):
    h1 = h_ref[...] + jnp.dot(mix_ref[...], wout_ref[...], preferred_element_type=F32)
    h1_ref[...] = h1
    xn = _rms(h1, gain_ref[...])
    xh = xn.astype(BF16)
    xn_ref[...] = xh
    xl = (xn - xh.astype(F32)).astype(BF16)
    logits = (jnp.dot(xh, wrh_ref[...], preferred_element_type=F32)
              + jnp.dot(xl, wrh_ref[...], preferred_element_type=F32)
              + jnp.dot(xh, wrl_ref[...], preferred_element_type=F32)
              + br_ref[...])
    lane = lax.broadcasted_iota(jnp.int32, logits.shape, 1).astype(F32)
    neg = jnp.float32(-jnp.inf)
    big = jnp.float32(1e9)
    gl = jnp.where(lane < MOE_GROUPS, logits, neg)
    gmax = jnp.max(gl, axis=-1, keepdims=True)
    gidx = jnp.min(jnp.where(gl == gmax, lane, big), axis=-1, keepdims=True)
    gp = 1.0 / jnp.sum(jnp.exp(gl - gmax), axis=-1, keepdims=True)
    lo = MOE_GROUPS + gidx * MOE_EXPERTS_PER_GROUP
    el = jnp.where((lane >= lo) & (lane < lo + MOE_EXPERTS_PER_GROUP), logits, neg)
    m1 = jnp.max(el, axis=-1, keepdims=True)
    i1 = jnp.min(jnp.where(el == m1, lane, big), axis=-1, keepdims=True)
    el2 = jnp.where(lane == i1, neg, el)
    m2 = jnp.max(el2, axis=-1, keepdims=True)
    i2 = jnp.min(jnp.where(el2 == m2, lane, big), axis=-1, keepdims=True)
    e2 = jnp.exp(m2 - m1)
    w1 = gp / (1.0 + e2)
    w2 = gp * e2 / (1.0 + e2)
    rw_ref[...] = jnp.where(lane == 0, w1, jnp.where(lane == 1, w2, 0.0))

    @pl.when(pl.program_id(0) == 0)
    def _():
        run_ref[...] = jnp.zeros_like(run_ref)

    tm = logits.shape[0]
    trow = lax.broadcasted_iota(jnp.int32, (tm, tm), 0)
    tcol = lax.broadcasted_iota(jnp.int32, (tm, tm), 1)
    before = jnp.where(tcol < trow, 1.0, 0.0).astype(BF16)
    hot1 = lane == i1
    hot2 = lane == i2
    oh1 = jnp.where(hot1, 1.0, 0.0)
    oh2 = jnp.where(hot2, 1.0, 0.0)
    prior1 = jnp.dot(before, oh1.astype(BF16), preferred_element_type=F32)
    prior2 = jnp.dot(before, oh2.astype(BF16), preferred_element_type=F32)
    cnt1 = jnp.sum(oh1, axis=0, keepdims=True)
    cnt2 = jnp.sum(oh2, axis=0, keepdims=True)
    run = run_ref[...]
    rank1 = jnp.sum(jnp.where(hot1, prior1 + run, 0.0), axis=-1, keepdims=True)
    rank2 = jnp.sum(jnp.where(hot2, prior2 + (run + cnt1), 0.0), axis=-1, keepdims=True)
    run = run + cnt1 + cnt2
    run_ref[...] = run
    cnt_ref[...] = run
    ri = jnp.where(lane == 0, i1 - MOE_GROUPS, jnp.where(lane == 1, i2 - MOE_GROUPS,
                   jnp.where(lane == 2, rank1, jnp.where(lane == 3, rank2, 0.0))))
    ri_ref[...] = ri[:, 0:ROUTE_COLS].astype(jnp.int32)


def _out_router(mix, h, w_out, gain, w_router, b_router):
    t, d = h.shape
    tm = TOKEN_TILE
    kd = mix.shape[1]
    row = lambda i: (i, 0)
    fix = lambda i: (0, 0)
    wr_hi = w_router.astype(BF16)
    wr_lo = (w_router - wr_hi.astype(F32)).astype(BF16)
    return pl.pallas_call(
        _out_router_kernel,
        grid=(t // tm,),
        in_specs=[pl.BlockSpec((tm, kd), row), pl.BlockSpec((tm, d), row),
                  pl.BlockSpec(w_out.shape, fix), pl.BlockSpec((1, d), fix),
                  pl.BlockSpec(w_router.shape, fix), pl.BlockSpec(w_router.shape, fix),
                  pl.BlockSpec((1, LANES), fix)],
        out_specs=[pl.BlockSpec((tm, d), row), pl.BlockSpec((tm, d), row),
                   pl.BlockSpec((tm, LANES), row), pl.BlockSpec((tm, ROUTE_COLS), row),
                   pl.BlockSpec((1, LANES), fix)],
        out_shape=[jax.ShapeDtypeStruct((t, d), F32), jax.ShapeDtypeStruct((t, d), BF16),
                   jax.ShapeDtypeStruct((t, LANES), F32), jax.ShapeDtypeStruct((t, ROUTE_COLS), jnp.int32),
                   jax.ShapeDtypeStruct((1, LANES), F32)],
        scratch_shapes=[pltpu.VMEM((1, LANES), F32)],
        compiler_params=_params("arbitrary"), name="out_proj_router",
    )(mix, h, w_out, gain, wr_hi, wr_lo, b_router)


def _expert_ffn_kernel(te_ref, nu_ref, xs_ref, wg_ref, wu_ref, wd_ref, ys_ref):
    i = pl.program_id(0)

    @pl.when(i < nu_ref[0])
    def _():
        x = xs_ref[...]
        hg = jnp.dot(x, wg_ref[0].astype(BF16), preferred_element_type=F32)
        hu = jnp.dot(x, wu_ref[0].astype(BF16), preferred_element_type=F32)
        hh = _silu(hg) * hu
        ys_ref[...] = jnp.dot(hh.astype(BF16), wd_ref[0].astype(BF16),
                              preferred_element_type=F32).astype(ys_ref.dtype)

    @pl.when(i >= nu_ref[0])
    def _():
        ys_ref[...] = jnp.zeros_like(ys_ref)


def _expert_ffn(tile_expert, n_used, xs, w_gate, w_up, w_down):
    r, d = xs.shape
    tm = FFN_TILE
    f = w_gate.shape[-1]
    grid_spec = pltpu.PrefetchScalarGridSpec(
        num_scalar_prefetch=2,
        grid=(r // tm,),
        in_specs=[
            pl.BlockSpec((tm, d), lambda i, te, nu: (i, 0)),
            pl.BlockSpec((1, d, f), lambda i, te, nu: (te[i], 0, 0)),
            pl.BlockSpec((1, d, f), lambda i, te, nu: (te[i], 0, 0)),
            pl.BlockSpec((1, f, d), lambda i, te, nu: (te[i], 0, 0)),
        ],
        out_specs=pl.BlockSpec((tm, d), lambda i, te, nu: (i, 0)),
    )
    return pl.pallas_call(
        _expert_ffn_kernel, grid_spec=grid_spec,
        out_shape=jax.ShapeDtypeStruct((r, d), BF16),
        compiler_params=_params("arbitrary"), name="expert_ffn",
    )(tile_expert, n_used, xs, w_gate, w_up, w_down)


def _ple_kernel(final, h_ref, m0_ref, m1_ref, rw_ref, p_ref, gain_ref, wg_ref, wp_ref, fgain_ref, o_ref):
    rw = rw_ref[...]
    h2 = h_ref[...] + (rw[:, 0:1] * m0_ref[...].astype(F32) + rw[:, 1:2] * m1_ref[...].astype(F32))
    xn = _rms(h2, gain_ref[...]).astype(BF16)
    gate = _sigmoid(jnp.dot(xn, wg_ref[...], preferred_element_type=F32))
    emb = jnp.dot(p_ref[...].astype(BF16), wp_ref[...], preferred_element_type=F32)
    h3 = h2 + gate * emb
    if final:
        h3 = _rms(h3, fgain_ref[...])
    o_ref[...] = h3


def _ple(h, m0, m1, rw, p, gain, w_gate, w_proj, final_gain, final):
    t, d = h.shape
    tm = TOKEN_TILE
    pd = p.shape[1]
    row = lambda i: (i, 0)
    fix = lambda i: (0, 0)
    return pl.pallas_call(
        functools.partial(_ple_kernel, final),
        grid=(t // tm,),
        in_specs=[pl.BlockSpec((tm, d), row), pl.BlockSpec((tm, d), row), pl.BlockSpec((tm, d), row),
                  pl.BlockSpec((tm, LANES), row), pl.BlockSpec((tm, pd), row), pl.BlockSpec((1, d), fix),
                  pl.BlockSpec(w_gate.shape, fix), pl.BlockSpec(w_proj.shape, fix),
                  pl.BlockSpec((1, d), fix)],
        out_specs=pl.BlockSpec((tm, d), row),
        out_shape=jax.ShapeDtypeStruct((t, d), F32),
        compiler_params=_params("parallel"), name="moe_residual_ple",
    )(h, m0, m1, rw, p, gain, w_gate, w_proj, final_gain)


def _routing_tables(ri, counts):
    t = ri.shape[0]
    tm = FFN_TILE
    n_rows = 2 * t + MOE_N_EXPERTS * tm
    padded = ((counts + tm - 1) // tm) * tm
    ends = jnp.cumsum(padded)
    starts = ends - padded
    ids = ri[:, 0:2]
    onehot = ids[:, :, None] == jnp.arange(MOE_N_EXPERTS, dtype=jnp.int32)[None, None, :]
    dest = ri[:, 2:4] + jnp.sum(jnp.where(onehot, starts[None, None, :], 0), axis=-1)
    tok = jnp.broadcast_to(jnp.arange(t, dtype=jnp.int32)[:, None], (t, 2))
    src_tok = jnp.zeros((n_rows,), jnp.int32).at[dest.reshape(-1)].set(
        tok.reshape(-1), unique_indices=True, indices_are_sorted=False)
    tile_start = jnp.arange(n_rows // tm, dtype=jnp.int32) * tm
    tile_expert = jnp.minimum(jnp.sum((ends[None, :] <= tile_start[:, None]).astype(jnp.int32), axis=1),
                              MOE_N_EXPERTS - 1)
    n_used = (ends[-1] // tm).astype(jnp.int32).reshape(1)
    return src_tok, tile_expert, n_used, dest


def _moe_ple(mix, h, w_out, moe_gain, w_router, b_router, w_gate, w_up, w_down,
             p, ple_gain, ple_w_gate, ple_w_proj, final_gain, final):
    h1, xn, rw, ri, cnt = _out_router(mix, h, w_out, moe_gain, w_router, b_router)
    counts = cnt[0, MOE_GROUPS:MOE_GROUPS + MOE_N_EXPERTS].astype(jnp.int32)
    src_tok, tile_expert, n_used, dest = _routing_tables(ri, counts)
    xs = jnp.take(xn, src_tok, axis=0)
    ys = _expert_ffn(tile_expert, n_used, xs, w_gate, w_up, w_down)
    m0 = jnp.take(ys, dest[:, 0], axis=0)
    m1 = jnp.take(ys, dest[:, 1], axis=0)
    return _ple(h1, m0, m1, rw, p, ple_gain, ple_w_gate, ple_w_proj, final_gain, final)


def _pad_lanes(v):
    return jnp.zeros((1, LANES), F32).at[0, :v.shape[0]].set(v.astype(F32))


def kernel(x, p, attn_norm, moe_norm, ple_norm, gdn_w_in, gdn_conv, gdn_a_log, gdn_dt_bias, gdn_o_norm, gdn_w_out, kv_norm, w_kv, sb_w_q, sb_w_out, moe_w_group, moe_b_group, moe_w_expert, moe_b_expert, moe_w_gate, moe_w_up, moe_w_down, ple_w_gate, ple_w_proj, final_norm):
    d = x.shape[-1]
    kd = GDN_HEADS * GDN_HEAD_DIM
    qkv_dim = 3 * kd

    def router_params(i):
        wr = jnp.zeros((d, LANES), F32)
        wr = wr.at[:, :MOE_GROUPS].set(moe_w_group[i])
        wr = wr.at[:, MOE_GROUPS:MOE_GROUPS + MOE_N_EXPERTS].set(moe_w_expert[i])
        br = jnp.zeros((1, LANES), F32)
        br = br.at[0, :MOE_GROUPS].set(moe_b_group[i])
        br = br.at[0, MOE_GROUPS:MOE_GROUPS + MOE_N_EXPERTS].set(moe_b_expert[i])
        return wr, br

    def expert_params(i):
        return (moe_w_gate[i].reshape(MOE_N_EXPERTS, d, MOE_D_EXPERT),
                moe_w_up[i].reshape(MOE_N_EXPERTS, d, MOE_D_EXPERT),
                moe_w_down[i].reshape(MOE_N_EXPERTS, MOE_D_EXPERT, d))

    w_in = gdn_w_in[0]
    w_ab = jnp.zeros((d, LANES), F32).at[:, :2 * GDN_HEADS].set(w_in[:, qkv_dim + kd:])
    w_in_parts = [w_in[:, :qkv_dim].astype(BF16), w_in[:, qkv_dim:qkv_dim + kd].astype(BF16), w_ab.astype(BF16)]
    dtb = jnp.zeros((1, LANES), F32).at[0, :GDN_HEADS].set(gdn_dt_bias[0])
    alog = _pad_lanes(gdn_a_log[0])
    routers = [router_params(0), router_params(1)]
    experts = [expert_params(0), expert_params(1)]
    mix_w_out = [gdn_w_out[0].astype(BF16), sb_w_out[0].astype(BF16)]
    ple_wg = [ple_w_gate[0].astype(BF16), ple_w_gate[1].astype(BF16)]
    ple_wp = [ple_w_proj[0].astype(BF16), ple_w_proj[1].astype(BF16)]
    gains1 = jnp.stack([attn_norm[1], kv_norm], axis=0)
    w_q = sb_w_q[0].astype(BF16)
    w_kv_b = w_kv.astype(BF16)
    fgain = final_norm.reshape(1, d)

    def forward(xg, pg):
        b, s, _ = xg.shape
        t = b * s
        h = xg.reshape(t, d)
        pf = pg.reshape(pg.shape[0], t, pg.shape[-1])
        qkv, z, ab = _norm_matmul(h, attn_norm[0:1], w_in_parts,
                                  [(0, 1.0, F32), (0, 1.0, F32), (0, 1.0, F32)])
        og = _gated_deltanet_core(
            qkv.reshape(b, s, qkv_dim), z.reshape(b, s, kd), ab.reshape(b, s, LANES),
            gdn_conv[0], alog, dtb, gdn_o_norm[0].reshape(1, GDN_HEAD_DIM))
        h = _moe_ple(og.reshape(t, kd), h, mix_w_out[0], moe_norm[0:1], *routers[0], *experts[0],
                     pf[0], ple_norm[0:1], ple_wg[0], ple_wp[0], fgain, False)
        q, kv = _norm_matmul(h, gains1, [w_q, w_kv_b],
                             [(0, SB_HEAD_DIM ** -0.5, BF16), (1, 1.0, BF16)])
        nb = s // SB_BLOCK
        kvw = SB_KV_HEADS * SB_HEAD_DIM
        k5 = kv[:, :kvw].reshape(b, nb, SB_BLOCK, SB_KV_HEADS, SB_HEAD_DIM)
        v5 = kv[:, kvw:].reshape(b, nb, SB_BLOCK, SB_KV_HEADS, SB_HEAD_DIM)
        kt = k5.transpose(0, 3, 1, 4, 2)
        vv = v5.transpose(0, 3, 1, 2, 4)
        oa = _stick_breaking_attention(q.reshape(b, s, -1), kt, vv)
        out = _moe_ple(oa.reshape(t, -1), h, mix_w_out[1], moe_norm[1:2], *routers[1], *experts[1],
                       pf[1], ple_norm[1:2], ple_wg[1], ple_wp[1], fgain, True)
        return out.reshape(b, s, d)

    bt = x.shape[0]
    n_groups = BATCH_GROUPS if bt % BATCH_GROUPS == 0 else 1
    bg = bt // n_groups
    outs = [forward(x[g * bg:(g + 1) * bg], p[:, g * bg:(g + 1) * bg]) for g in range(n_groups)]
    return outs[0] if n_groups == 1 else jnp.concatenate(outs, axis=0)
```

```python
import functools

import jax
import jax.numpy as jnp
from jax import lax
from jax.experimental import pallas as pl
from jax.experimental.pallas import tpu as pltpu

NORM_EPS = 1e-6
LOG2E = 1.4426950408889634
LANES = 128
GDN_HEADS = 8
GDN_HEAD_DIM = 128
GDN_CONV = 4
GDN_CHUNK = 64
SB_Q_HEADS = 16
SB_KV_HEADS = 4
SB_GROUP = SB_Q_HEADS // SB_KV_HEADS
SB_HEAD_DIM = 64
SB_BLOCK = 128
MOE_GROUPS = 4
MOE_EXPERTS_PER_GROUP = 8
MOE_N_EXPERTS = MOE_GROUPS * MOE_EXPERTS_PER_GROUP
MOE_D_EXPERT = 256

VMEM_LIMIT = 56 * 1024 * 1024
TOKEN_TILE = 256
FFN_TILE = 256
GDN_TIME_BLOCK = 512
ROUTE_COLS = 8
BATCH_GROUPS = 1

F32 = jnp.float32
BF16 = jnp.bfloat16


def _params(*sem):
    return pltpu.CompilerParams(dimension_semantics=sem, vmem_limit_bytes=VMEM_LIMIT)


def _dot(a, b):
    return jnp.dot(a.astype(BF16), b.astype(BF16), preferred_element_type=F32)


def _dot_nt(a, b):
    return lax.dot_general(a.astype(BF16), b.astype(BF16), (((1,), (1,)), ((), ())),
                           preferred_element_type=F32)


def _dot_tn(a, b):
    return lax.dot_general(a.astype(BF16), b.astype(BF16), (((0,), (0,)), ((), ())),
                           preferred_element_type=F32)


def _dot_f32(a, b):
    return jnp.dot(a, b, precision=lax.Precision.HIGHEST, preferred_element_type=F32)


def _rms(x, gain):
    return x * lax.rsqrt(jnp.mean(x * x, axis=-1, keepdims=True) + NORM_EPS) * gain


def _silu(x):
    return x * (1.0 / (1.0 + jnp.exp(-x)))


def _sigmoid(x):
    return 1.0 / (1.0 + jnp.exp(-x))


def _softplus(x):
    return jnp.maximum(x, 0.0) + jnp.log(1.0 + jnp.exp(-jnp.abs(x)))


def _norm_matmul_kernel(plan, n_chunk, x_ref, gains_ref, *refs):
    n_w = len(plan)
    w_refs, o_refs = refs[:n_w], refs[n_w:]
    x = x_ref[...]
    inv = lax.rsqrt(jnp.mean(x * x, axis=-1, keepdims=True) + NORM_EPS)
    xn = {}
    for (g, _, _) in plan:
        if g not in xn:
            xn[g] = (x * inv * gains_ref[g:g + 1, :]).astype(BF16)
    for (g, scale, _), w_ref, o_ref in zip(plan, w_refs, o_refs):
        n = w_ref.shape[1]
        for n0 in range(0, n, n_chunk):
            n1 = min(n, n0 + n_chunk)
            acc = jnp.dot(xn[g], w_ref[:, n0:n1], preferred_element_type=F32)
            if scale != 1.0:
                acc = acc * scale
            o_ref[:, n0:n1] = acc.astype(o_ref.dtype)


def _norm_matmul(x, gains, ws, plan):
    t, d = x.shape
    tm = TOKEN_TILE
    in_specs = [pl.BlockSpec((tm, d), lambda i: (i, 0)),
                pl.BlockSpec(gains.shape, lambda i: (0, 0))]
    in_specs += [pl.BlockSpec(w.shape, lambda i: (0, 0)) for w in ws]
    out_specs = [pl.BlockSpec((tm, w.shape[1]), lambda i: (i, 0)) for w in ws]
    out_shape = [jax.ShapeDtypeStruct((t, w.shape[1]), p[2]) for w, p in zip(ws, plan)]
    return pl.pallas_call(
        functools.partial(_norm_matmul_kernel, tuple(plan), 512),
        grid=(t // tm,), in_specs=in_specs, out_specs=out_specs, out_shape=out_shape,
        compiler_params=_params("parallel"), name="norm_matmul",
    )(x, gains, *ws)


def _gdn_kernel(n_heads, ts, qkv_ref, z_ref, ab_ref, conv_ref, alog_ref, dtb_ref, onorm_ref,
                o_ref, xpad_ref, state_ref, g_ref, beta_ref):
    c = GDN_CHUNK
    dk = GDN_HEAD_DIM
    tb = pl.program_id(1)
    n_ch = ts // c
    qkv_dim = qkv_ref.shape[-1]

    @pl.when(tb == 0)
    def _():
        state_ref[...] = jnp.zeros_like(state_ref)
        xpad_ref[0:8, :] = jnp.zeros((8, qkv_dim), F32)

    xpad_ref[8:8 + ts, :] = qkv_ref[0]

    ab = ab_ref[0]
    g_all = -jnp.exp(alog_ref[...]) * _softplus(ab + dtb_ref[...])
    beta_all = _sigmoid(ab)
    rin = lax.broadcasted_iota(jnp.int32, (ts, LANES), 0) & (c - 1)
    for h in range(n_heads):
        gh = jnp.broadcast_to(g_all[:, h:h + 1], (ts, LANES))
        shift = 1
        while shift < c:
            gh = gh + jnp.where(rin >= shift, pltpu.roll(gh, shift, 0), 0.0)
            shift *= 2
        g_ref[h] = gh
        beta_ref[h] = jnp.broadcast_to(beta_all[:, n_heads + h:n_heads + h + 1], (ts, LANES))

    row = lax.broadcasted_iota(jnp.int32, (c, c), 0)
    col = lax.broadcasted_iota(jnp.int32, (c, c), 1)
    incl = row >= col
    strict = row > col

    def conv_silu(r0, c0):
        win = xpad_ref[pl.ds(r0, c + 8), c0:c0 + LANES]
        acc = jnp.zeros((c, LANES), F32)
        for j in range(GDN_CONV):
            off = 8 - (GDN_CONV - 1) + j
            acc = acc + win[off:off + c, :] * conv_ref[j:j + 1, c0:c0 + LANES]
        return _silu(acc)

    def l2n(t):
        return t * lax.rsqrt(jnp.sum(t * t, axis=-1, keepdims=True) + NORM_EPS)

    heads = range(n_heads)
    eye = (row == col).astype(F32)
    same_blk = (row >> 4) == (col >> 4)

    def chunk_body(n, carry):
        r0 = pl.multiple_of(n * c, c)
        q = [l2n(conv_silu(r0, h * dk)) * (dk ** -0.5) for h in heads]
        k = [l2n(conv_silu(r0, (n_heads + h) * dk)) for h in heads]
        v = [conv_silu(r0, (2 * n_heads + h) * dk) for h in heads]
        gc = [g_ref[h, pl.ds(r0, c), :] for h in heads]
        beta = [beta_ref[h, pl.ds(r0, c), :] for h in heads]
        kb = [k[h] * beta[h] for h in heads]
        kk = [_dot_nt(kb[h], k[h]) for h in heads]
        qk = [_dot_nt(q[h], k[h]) for h in heads]
        decay = []
        for h in heads:
            diff = gc[h][:, 0:c] - gc[h].T[0:c, :]
            decay.append(jnp.where(incl, jnp.exp(jnp.where(incl, diff, 0.0)), 0.0))
        lm = [jnp.where(strict, kk[h] * decay[h], 0.0) for h in heads]
        attn = [jnp.where(incl, qk[h] * decay[h], 0.0) for h in heads]
        ld = [jnp.where(same_blk, lm[h], 0.0) for h in heads]
        lo = [lm[h] - ld[h] for h in heads]
        p = [eye - ld[h] for h in heads]
        sq = [_dot(ld[h], ld[h]) for h in heads]
        for _ in range(2):
            pn = [_dot(p[h], sq[h]) for h in heads]
            sq2 = [_dot(sq[h], sq[h]) for h in heads]
            p = [p[h] + pn[h] for h in heads]
            sq = sq2
        dinv = [p[h] + _dot(p[h], sq[h]) for h in heads]
        m = [_dot(dinv[h], lo[h]) for h in heads]
        m2 = [_dot(m[h], m[h]) for h in heads]
        r = [(eye - m[h]) + _dot(eye - m[h], m2[h]) for h in heads]
        tmat = [_dot(r[h], dinv[h]) for h in heads]
        eg = [jnp.exp(gc[h]) for h in heads]
        rhs = [jnp.concatenate([v[h] * beta[h], kb[h] * eg[h]], axis=1) for h in heads]
        uw = [_dot(tmat[h], rhs[h]) for h in heads]
        s = [state_ref[h] for h in heads]
        lhs = [jnp.concatenate([uw[h][:, dk:], q[h] * eg[h]], axis=0) for h in heads]
        ws_qs = [_dot(lhs[h], s[h]) for h in heads]
        v_new = [uw[h][:, :dk] - ws_qs[h][:c] for h in heads]
        g_last = [gc[h][c - 1:c, :] for h in heads]
        k_dec = [k[h] * jnp.exp(g_last[h] - gc[h]) for h in heads]
        av = [_dot(attn[h], v_new[h]) for h in heads]
        kv = [_dot_tn(k_dec[h], v_new[h]) for h in heads]
        for h in heads:
            state_ref[h] = s[h] * jnp.exp(g_last[h]) + kv[h]
            o = ws_qs[h][c:] + av[h]
            o = o * lax.rsqrt(jnp.mean(o * o, axis=-1, keepdims=True) + NORM_EPS) * onorm_ref[...]
            zz = z_ref[0, pl.ds(r0, c), h * dk:(h + 1) * dk]
            o_ref[0, pl.ds(r0, c), h * dk:(h + 1) * dk] = (o * _silu(zz)).astype(o_ref.dtype)
        return carry

    lax.fori_loop(0, n_ch, chunk_body, 0)
    xpad_ref[0:8, :] = xpad_ref[ts:ts + 8, :]


def _gated_deltanet_core(qkv, z, ab, conv_w, alog, dtb, onorm):
    b, s, qkv_dim = qkv.shape
    n_heads = GDN_HEADS
    ts = min(GDN_TIME_BLOCK, s)
    vd = z.shape[-1]
    kern = functools.partial(_gdn_kernel, n_heads, ts)
    return pl.pallas_call(
        kern,
        grid=(b, s // ts),
        in_specs=[
            pl.BlockSpec((1, ts, qkv_dim), lambda i, j: (i, j, 0)),
            pl.BlockSpec((1, ts, vd), lambda i, j: (i, j, 0)),
            pl.BlockSpec((1, ts, LANES), lambda i, j: (i, j, 0)),
            pl.BlockSpec(conv_w.shape, lambda i, j: (0, 0)),
            pl.BlockSpec((1, LANES), lambda i, j: (0, 0)),
            pl.BlockSpec((1, LANES), lambda i, j: (0, 0)),
            pl.BlockSpec((1, GDN_HEAD_DIM), lambda i, j: (0, 0)),
        ],
        out_specs=pl.BlockSpec((1, ts, vd), lambda i, j: (i, j, 0)),
        out_shape=jax.ShapeDtypeStruct((b, s, vd), BF16),
        scratch_shapes=[
            pltpu.VMEM((ts + 8, qkv_dim), F32),
            pltpu.VMEM((n_heads, GDN_HEAD_DIM, GDN_HEAD_DIM), F32),
            pltpu.VMEM((n_heads, ts, LANES), F32),
            pltpu.VMEM((n_heads, ts, LANES), F32),
        ],
        compiler_params=_params("parallel", "arbitrary"), name="gdn_delta_rule",
    )(qkv, z, ab, conv_w, alog, dtb, onorm)


def _sb_attn_kernel(q_ref, kt_ref, v_ref, o_ref, acc_ref, cs_ref, z_ref, p_ref):
    blk = SB_BLOCK
    dh = SB_HEAD_DIM
    grp = SB_GROUP
    i = pl.program_id(2)
    rows = grp * blk
    n_pairs = 1 + i // 2
    qb = q_ref[0]
    q4 = jnp.concatenate([qb[:, g * dh:(g + 1) * dh] for g in range(grp)], axis=0)
    krow = lax.broadcasted_iota(jnp.int32, (blk, blk), 0)
    kcol = lax.broadcasted_iota(jnp.int32, (blk, blk), 1)
    neg_suffix = jnp.where(krow >= kcol, -1.0, 0.0).astype(BF16)

    def pair_blocks(k):
        ja = i - 2 * k
        return jnp.maximum(ja, 0), jnp.maximum(ja - 1, 0), (ja >= 1).astype(F32)

    def logits(k):
        ja, jb, _ = pair_blocks(k)
        z_ref[0] = jnp.dot(q4, kt_ref[0, 0, ja], preferred_element_type=F32)
        z_ref[1] = jnp.dot(q4, kt_ref[0, 0, jb], preferred_element_type=F32)

    def weights(mask_a):
        masks = [mask_a, None]
        zs = [z_ref[0], z_ref[1]]
        sps = []
        for z, mask in zip(zs, masks):
            sp = jnp.maximum(z, 0.0) + jnp.log(1.0 + jnp.exp2(jnp.abs(z) * -LOG2E))
            if mask is not None:
                sp = jnp.where(mask, sp, 0.0)
            sps.append(sp)
        sufs = [jnp.dot(sp.astype(BF16), neg_suffix, preferred_element_type=F32) for sp in sps]
        rss = [jnp.sum(sp, axis=-1, keepdims=True) for sp in sps]
        cs = cs_ref[...]
        for n, (z, suf, rs, mask) in enumerate(zip(zs, sufs, rss, masks)):
            p = jnp.exp2(((z + suf) - cs) * LOG2E)
            if mask is not None:
                p = jnp.where(mask, p, 0.0)
            p_ref[n] = p.astype(BF16)
            cs = cs + rs
        cs_ref[...] = cs

    def weighted_values(k):
        ja, jb, valid_b = pair_blocks(k)
        vb = (v_ref[0, 0, jb].astype(F32) * valid_b).astype(BF16)
        acc_ref[...] += (jnp.dot(p_ref[0], v_ref[0, 0, ja], preferred_element_type=F32)
                         + jnp.dot(p_ref[1], vb, preferred_element_type=F32))

    acc_ref[...] = jnp.zeros_like(acc_ref)
    cs_ref[...] = jnp.zeros_like(cs_ref)
    t_in = lax.broadcasted_iota(jnp.int32, (rows, blk), 0) & (blk - 1)
    s_in = lax.broadcasted_iota(jnp.int32, (rows, blk), 1)
    logits(0)
    weights(s_in < t_in)
    logits(1)

    def body(k, c):
        weighted_values(k - 1)
        weights(None)
        logits(k + 1)
        return c

    lax.fori_loop(1, n_pairs, body, 0)
    weighted_values(n_pairs - 1)
    acc = acc_ref[...]
    for g in range(grp):
        o_ref[0, :, g * dh:(g + 1) * dh] = acc[g * blk:(g + 1) * blk, :].astype(o_ref.dtype)


def _stick_breaking_attention(q, kt, v):
    b, s, qd = q.shape
    nb = s // SB_BLOCK
    gw = SB_GROUP * SB_HEAD_DIM
    rows = SB_GROUP * SB_BLOCK
    return pl.pallas_call(
        _sb_attn_kernel,
        grid=(b, SB_KV_HEADS, nb),
        in_specs=[
            pl.BlockSpec((1, SB_BLOCK, gw), lambda bi, g, i: (bi, i, g)),
            pl.BlockSpec((1, 1, nb, SB_HEAD_DIM, SB_BLOCK), lambda bi, g, i: (bi, g, 0, 0, 0)),
            pl.BlockSpec((1, 1, nb, SB_BLOCK, SB_HEAD_DIM), lambda bi, g, i: (bi, g, 0, 0, 0)),
        ],
        out_specs=pl.BlockSpec((1, SB_BLOCK, gw), lambda bi, g, i: (bi, i, g)),
        out_shape=jax.ShapeDtypeStruct((b, s, qd), BF16),
        scratch_shapes=[pltpu.VMEM((rows, SB_HEAD_DIM), F32), pltpu.VMEM((rows, LANES), F32),
                        pltpu.VMEM((2, rows, SB_BLOCK), F32), pltpu.VMEM((2, rows, SB_BLOCK), BF16)],
        compiler_params=_params("parallel", "parallel", "arbitrary"), name="stick_breaking_attention",
    )(q, kt, v)


def _out_router_kernel(mix_ref, h_ref, wout_ref, gain_ref, wrh_ref, wrl_ref, br_ref,
                       h1_ref, xn_ref, rw_ref, ri_ref, cnt_ref, run_ref):
    h1 = h_ref[...] + jnp.dot(mix_ref[...], wout_ref[...], preferred_element_type=F32)
    h1_ref[...] = h1
    xn = _rms(h1, gain_ref[...])
    xh = xn.astype(BF16)
    xn_ref[...] = xh
    xl = (xn - xh.astype(F32)).astype(BF16)
    logits = (jnp.dot(xh, wrh_ref[...], preferred_element_type=F32)
              + jnp.dot(xl, wrh_ref[...], preferred_element_type=F32)
              + jnp.dot(xh, wrl_ref[...], preferred_element_type=F32)
              + br_ref[...])
    lane = lax.broadcasted_iota(jnp.int32, logits.shape, 1).astype(F32)
    neg = jnp.float32(-jnp.inf)
    big = jnp.float32(1e9)
    gl = jnp.where(lane < MOE_GROUPS, logits, neg)
    gmax = jnp.max(gl, axis=-1, keepdims=True)
    gidx = jnp.min(jnp.where(gl == gmax, lane, big), axis=-1, keepdims=True)
    gp = 1.0 / jnp.sum(jnp.exp(gl - gmax), axis=-1, keepdims=True)
    lo = MOE_GROUPS + gidx * MOE_EXPERTS_PER_GROUP
    el = jnp.where((lane >= lo) & (lane < lo + MOE_EXPERTS_PER_GROUP), logits, neg)
    m1 = jnp.max(el, axis=-1, keepdims=True)
    i1 = jnp.min(jnp.where(el == m1, lane, big), axis=-1, keepdims=True)
    el2 = jnp.where(lane == i1, neg, el)
    m2 = jnp.max(el2, axis=-1, keepdims=True)
    i2 = jnp.min(jnp.where(el2 == m2, lane, big), axis=-1, keepdims=True)
    e2 = jnp.exp(m2 - m1)
    w1 = gp / (1.0 + e2)
    w2 = gp * e2 / (1.0 + e2)
    rw_ref[...] = jnp.where(lane == 0, w1, jnp.where(lane == 1, w2, 0.0))

    @pl.when(pl.program_id(0) == 0)
    def _():
        run_ref[...] = jnp.zeros_like(run_ref)

    tm = logits.shape[0]
    trow = lax.broadcasted_iota(jnp.int32, (tm, tm), 0)
    tcol = lax.broadcasted_iota(jnp.int32, (tm, tm), 1)
    before = jnp.where(tcol < trow, 1.0, 0.0).astype(BF16)
    hot1 = lane == i1
    hot2 = lane == i2
    oh1 = jnp.where(hot1, 1.0, 0.0)
    oh2 = jnp.where(hot2, 1.0, 0.0)
    prior1 = jnp.dot(before, oh1.astype(BF16), preferred_element_type=F32)
    prior2 = jnp.dot(before, oh2.astype(BF16), preferred_element_type=F32)
    cnt1 = jnp.sum(oh1, axis=0, keepdims=True)
    cnt2 = jnp.sum(oh2, axis=0, keepdims=True)
    run = run_ref[...]
    rank1 = jnp.sum(jnp.where(hot1, prior1 + run, 0.0), axis=-1, keepdims=True)
    rank2 = jnp.sum(jnp.where(hot2, prior2 + (run + cnt1), 0.0), axis=-1, keepdims=True)
    run = run + cnt1 + cnt2
    run_ref[...] = run
    cnt_ref[...] = run
    ri = jnp.where(lane == 0, i1 - MOE_GROUPS, jnp.where(lane == 1, i2 - MOE_GROUPS,
                   jnp.where(lane == 2, rank1, jnp.where(lane == 3, rank2, 0.0))))
    ri_ref[...] = ri.T[0:ROUTE_COLS, :].astype(jnp.int32)


def _out_router(mix, h, w_out, gain, w_router, b_router):
    t, d = h.shape
    tm = TOKEN_TILE
    kd = mix.shape[1]
    row = lambda i: (i, 0)
    fix = lambda i: (0, 0)
    wr_hi = w_router.astype(BF16)
    wr_lo = (w_router - wr_hi.astype(F32)).astype(BF16)
    return pl.pallas_call(
        _out_router_kernel,
        grid=(t // tm,),
        in_specs=[pl.BlockSpec((tm, kd), row), pl.BlockSpec((tm, d), row),
                  pl.BlockSpec(w_out.shape, fix), pl.BlockSpec((1, d), fix),
                  pl.BlockSpec(w_router.shape, fix), pl.BlockSpec(w_router.shape, fix),
                  pl.BlockSpec((1, LANES), fix)],
        out_specs=[pl.BlockSpec((tm, d), row), pl.BlockSpec((tm, d), row),
                   pl.BlockSpec((tm, LANES), row), pl.BlockSpec((ROUTE_COLS, tm), lambda i: (0, i)),
                   pl.BlockSpec((1, LANES), fix)],
        out_shape=[jax.ShapeDtypeStruct((t, d), F32), jax.ShapeDtypeStruct((t, d), BF16),
                   jax.ShapeDtypeStruct((t, LANES), F32), jax.ShapeDtypeStruct((ROUTE_COLS, t), jnp.int32),
                   jax.ShapeDtypeStruct((1, LANES), F32)],
        scratch_shapes=[pltpu.VMEM((1, LANES), F32)],
        compiler_params=_params("arbitrary"), name="out_proj_router",
    )(mix, h, w_out, gain, wr_hi, wr_lo, b_router)


def _expert_ffn_kernel(te_ref, nu_ref, xs_ref, wg_ref, wu_ref, wd_ref, ys_ref):
    i = pl.program_id(0)

    @pl.when(i < nu_ref[0])
    def _():
        x = xs_ref[...]
        hg = jnp.dot(x, wg_ref[0].astype(BF16), preferred_element_type=F32)
        hu = jnp.dot(x, wu_ref[0].astype(BF16), preferred_element_type=F32)
        hh = _silu(hg) * hu
        ys_ref[...] = jnp.dot(hh.astype(BF16), wd_ref[0].astype(BF16),
                              preferred_element_type=F32).astype(ys_ref.dtype)

    @pl.when(i >= nu_ref[0])
    def _():
        ys_ref[...] = jnp.zeros_like(ys_ref)


def _expert_ffn(tile_expert, n_used, xs, w_gate, w_up, w_down):
    r, d = xs.shape
    tm = FFN_TILE
    f = w_gate.shape[-1]
    grid_spec = pltpu.PrefetchScalarGridSpec(
        num_scalar_prefetch=2,
        grid=(r // tm,),
        in_specs=[
            pl.BlockSpec((tm, d), lambda i, te, nu: (i, 0)),
            pl.BlockSpec((1, d, f), lambda i, te, nu: (te[i], 0, 0)),
            pl.BlockSpec((1, d, f), lambda i, te, nu: (te[i], 0, 0)),
            pl.BlockSpec((1, f, d), lambda i, te, nu: (te[i], 0, 0)),
        ],
        out_specs=pl.BlockSpec((tm, d), lambda i, te, nu: (i, 0)),
    )
    return pl.pallas_call(
        _expert_ffn_kernel, grid_spec=grid_spec,
        out_shape=jax.ShapeDtypeStruct((r, d), BF16),
        compiler_params=_params("arbitrary"), name="expert_ffn",
    )(tile_expert, n_used, xs, w_gate, w_up, w_down)


def _ple_kernel(final, h_ref, m0_ref, m1_ref, rw_ref, p_ref, gain_ref, wg_ref, wp_ref, fgain_ref, o_ref):
    rw = rw_ref[...]
    h2 = h_ref[...] + (rw[:, 0:1] * m0_ref[...].astype(F32) + rw[:, 1:2] * m1_ref[...].astype(F32))
    xn = _rms(h2, gain_ref[...]).astype(BF16)
    gate = _sigmoid(jnp.dot(xn, wg_ref[...], preferred_element_type=F32))
    emb = jnp.dot(p_ref[...].astype(BF16), wp_ref[...], preferred_element_type=F32)
    h3 = h2 + gate * emb
    if final:
        h3 = _rms(h3, fgain_ref[...])
    o_ref[...] = h3


def _ple(h, m0, m1, rw, p, gain, w_gate, w_proj, final_gain, final):
    t, d = h.shape
    tm = TOKEN_TILE
    pd = p.shape[1]
    row = lambda i: (i, 0)
    fix = lambda i: (0, 0)
    return pl.pallas_call(
        functools.partial(_ple_kernel, final),
        grid=(t // tm,),
        in_specs=[pl.BlockSpec((tm, d), row), pl.BlockSpec((tm, d), row), pl.BlockSpec((tm, d), row),
                  pl.BlockSpec((tm, LANES), row), pl.BlockSpec((tm, pd), row), pl.BlockSpec((1, d), fix),
                  pl.BlockSpec(w_gate.shape, fix), pl.BlockSpec(w_proj.shape, fix),
                  pl.BlockSpec((1, d), fix)],
        out_specs=pl.BlockSpec((tm, d), row),
        out_shape=jax.ShapeDtypeStruct((t, d), F32),
        compiler_params=_params("parallel"), name="moe_residual_ple",
    )(h, m0, m1, rw, p, gain, w_gate, w_proj, final_gain)


def _routing_tables(ri, counts):
    t = ri.shape[1]
    tm = FFN_TILE
    n_rows = 2 * t + MOE_N_EXPERTS * tm
    padded = ((counts + tm - 1) // tm) * tm
    ends = jnp.cumsum(padded)
    starts = ends - padded
    ids = ri[0:2]
    offs = jnp.zeros_like(ids)
    for e in range(MOE_N_EXPERTS):
        offs = jnp.where(ids == e, starts[e], offs)
    dest = ri[2:4] + offs
    tok = jnp.broadcast_to(jnp.arange(t, dtype=jnp.int32)[None, :], (2, t))
    src_tok = jnp.zeros((n_rows,), jnp.int32).at[dest.reshape(-1)].set(
        tok.reshape(-1), unique_indices=True, indices_are_sorted=False)
    tile_start = jnp.arange(n_rows // tm, dtype=jnp.int32) * tm
    tile_expert = jnp.minimum(jnp.sum((ends[None, :] <= tile_start[:, None]).astype(jnp.int32), axis=1),
                              MOE_N_EXPERTS - 1)
    n_used = (ends[-1] // tm).astype(jnp.int32).reshape(1)
    return src_tok, tile_expert, n_used, dest


def _moe_ple(mix, h, w_out, moe_gain, w_router, b_router, w_gate, w_up, w_down,
             p, ple_gain, ple_w_gate, ple_w_proj, final_gain, final):
    h1, xn, rw, ri, cnt = _out_router(mix, h, w_out, moe_gain, w_router, b_router)
    counts = cnt[0, MOE_GROUPS:MOE_GROUPS + MOE_N_EXPERTS].astype(jnp.int32)
    src_tok, tile_expert, n_used, dest = _routing_tables(ri, counts)
    xs = jnp.take(xn, src_tok, axis=0)
    ys = _expert_ffn(tile_expert, n_used, xs, w_gate, w_up, w_down)
    m0 = jnp.take(ys, dest[0], axis=0)
    m1 = jnp.take(ys, dest[1], axis=0)
    return _ple(h1, m0, m1, rw, p, ple_gain, ple_w_gate, ple_w_proj, final_gain, final)


def _pad_lanes(v):
    return jnp.zeros((1, LANES), F32).at[0, :v.shape[0]].set(v.astype(F32))


def kernel(x, p, attn_norm, moe_norm, ple_norm, gdn_w_in, gdn_conv, gdn_a_log, gdn_dt_bias, gdn_o_norm, gdn_w_out, kv_norm, w_kv, sb_w_q, sb_w_out, moe_w_group, moe_b_group, moe_w_expert, moe_b_expert, moe_w_gate, moe_w_up, moe_w_down, ple_w_gate, ple_w_proj, final_norm):
    d = x.shape[-1]
    kd = GDN_HEADS * GDN_HEAD_DIM
    qkv_dim = 3 * kd

    def router_params(i):
        wr = jnp.zeros((d, LANES), F32)
        wr = wr.at[:, :MOE_GROUPS].set(moe_w_group[i])
        wr = wr.at[:, MOE_GROUPS:MOE_GROUPS + MOE_N_EXPERTS].set(moe_w_expert[i])
        br = jnp.zeros((1, LANES), F32)
        br = br.at[0, :MOE_GROUPS].set(moe_b_group[i])
        br = br.at[0, MOE_GROUPS:MOE_GROUPS + MOE_N_EXPERTS].set(moe_b_expert[i])
        return wr, br

    def expert_params(i):
        return (moe_w_gate[i].reshape(MOE_N_EXPERTS, d, MOE_D_EXPERT),
                moe_w_up[i].reshape(MOE_N_EXPERTS, d, MOE_D_EXPERT),
                moe_w_down[i].reshape(MOE_N_EXPERTS, MOE_D_EXPERT, d))

    w_in = gdn_w_in[0]
    w_ab = jnp.zeros((d, LANES), F32).at[:, :2 * GDN_HEADS].set(w_in[:, qkv_dim + kd:])
    w_in_parts = [w_in[:, :qkv_dim].astype(BF16), w_in[:, qkv_dim:qkv_dim + kd].astype(BF16), w_ab.astype(BF16)]
    dtb = jnp.zeros((1, LANES), F32).at[0, :GDN_HEADS].set(gdn_dt_bias[0])
    alog = _pad_lanes(gdn_a_log[0])
    routers = [router_params(0), router_params(1)]
    experts = [expert_params(0), expert_params(1)]
    mix_w_out = [gdn_w_out[0].astype(BF16), sb_w_out[0].astype(BF16)]
    ple_wg = [ple_w_gate[0].astype(BF16), ple_w_gate[1].astype(BF16)]
    ple_wp = [ple_w_proj[0].astype(BF16), ple_w_proj[1].astype(BF16)]
    gains1 = jnp.stack([attn_norm[1], kv_norm], axis=0)
    w_q = sb_w_q[0].astype(BF16)
    w_kv_b = w_kv.astype(BF16)
    fgain = final_norm.reshape(1, d)

    def forward(xg, pg):
        b, s, _ = xg.shape
        t = b * s
        h = xg.reshape(t, d)
        pf = pg.reshape(pg.shape[0], t, pg.shape[-1])
        qkv, z, ab = _norm_matmul(h, attn_norm[0:1], w_in_parts,
                                  [(0, 1.0, F32), (0, 1.0, F32), (0, 1.0, F32)])
        og = _gated_deltanet_core(
            qkv.reshape(b, s, qkv_dim), z.reshape(b, s, kd), ab.reshape(b, s, LANES),
            gdn_conv[0], alog, dtb, gdn_o_norm[0].reshape(1, GDN_HEAD_DIM))
        h = _moe_ple(og.reshape(t, kd), h, mix_w_out[0], moe_norm[0:1], *routers[0], *experts[0],
                     pf[0], ple_norm[0:1], ple_wg[0], ple_wp[0], fgain, False)
        q, kv = _norm_matmul(h, gains1, [w_q, w_kv_b],
                             [(0, SB_HEAD_DIM ** -0.5, BF16), (1, 1.0, BF16)])
        nb = s // SB_BLOCK
        kvw = SB_KV_HEADS * SB_HEAD_DIM
        k5 = kv[:, :kvw].reshape(b, nb, SB_BLOCK, SB_KV_HEADS, SB_HEAD_DIM)
        v5 = kv[:, kvw:].reshape(b, nb, SB_BLOCK, SB_KV_HEADS, SB_HEAD_DIM)
        kt = k5.transpose(0, 3, 1, 4, 2)
        vv = v5.transpose(0, 3, 1, 2, 4)
        oa = _stick_breaking_attention(q.reshape(b, s, -1), kt, vv)
        out = _moe_ple(oa.reshape(t, -1), h, mix_w_out[1], moe_norm[1:2], *routers[1], *experts[1],
                       pf[1], ple_norm[1:2], ple_wg[1], ple_wp[1], fgain, True)
        return out.reshape(b, s, d)

    bt = x.shape[0]
    n_groups = BATCH_GROUPS if bt % BATCH_GROUPS == 0 else 1
    bg = bt // n_groups
    outs = [forward(x[g * bg:(g + 1) * bg], p[:, g * bg:(g + 1) * bg]) for g in range(n_groups)]
    return outs[0] if n_groups == 1 else jnp.concatenate(outs, axis=0)
```

```python
import functools

import jax
import jax.numpy as jnp
from jax import lax
from jax.experimental import pallas as pl
from jax.experimental.pallas import tpu as pltpu
from jax.experimental.pallas import tpu_sc as plsc

NORM_EPS = 1e-6
LOG2E = 1.4426950408889634
LANES = 128
GDN_HEADS = 8
GDN_HEAD_DIM = 128
GDN_CONV = 4
GDN_CHUNK = 64
SB_Q_HEADS = 16
SB_KV_HEADS = 4
SB_GROUP = SB_Q_HEADS // SB_KV_HEADS
SB_HEAD_DIM = 64
SB_BLOCK = 128
MOE_GROUPS = 4
MOE_EXPERTS_PER_GROUP = 8
MOE_N_EXPERTS = MOE_GROUPS * MOE_EXPERTS_PER_GROUP
MOE_D_EXPERT = 256

VMEM_LIMIT = 56 * 1024 * 1024
TOKEN_TILE = 256
FFN_TILE = 256
GDN_TIME_BLOCK = 512
ROUTE_COLS = 8
BATCH_GROUPS = 1
SC_GATHER_ROWS = 128

F32 = jnp.float32
BF16 = jnp.bfloat16


def _params(*sem):
    return pltpu.CompilerParams(dimension_semantics=sem, vmem_limit_bytes=VMEM_LIMIT)


def _dot(a, b):
    return jnp.dot(a.astype(BF16), b.astype(BF16), preferred_element_type=F32)


def _dot_nt(a, b):
    return lax.dot_general(a.astype(BF16), b.astype(BF16), (((1,), (1,)), ((), ())),
                           preferred_element_type=F32)


def _dot_tn(a, b):
    return lax.dot_general(a.astype(BF16), b.astype(BF16), (((0,), (0,)), ((), ())),
                           preferred_element_type=F32)


def _dot_f32(a, b):
    return jnp.dot(a, b, precision=lax.Precision.HIGHEST, preferred_element_type=F32)


def _rms(x, gain):
    return x * lax.rsqrt(jnp.mean(x * x, axis=-1, keepdims=True) + NORM_EPS) * gain


def _silu(x):
    return x * (1.0 / (1.0 + jnp.exp(-x)))


def _sigmoid(x):
    return 1.0 / (1.0 + jnp.exp(-x))


def _softplus(x):
    return jnp.maximum(x, 0.0) + jnp.log(1.0 + jnp.exp(-jnp.abs(x)))


def _pack_halves(x):
    w = x.shape[1] // 2
    hi = pltpu.bitcast(x[:, :w].astype(BF16).astype(F32), jnp.uint32)
    lo = pltpu.bitcast(x[:, w:].astype(BF16).astype(F32), jnp.uint32)
    return (hi & jnp.uint32(0xFFFF0000)) | (lo >> 16)


def _unpack_halves(u):
    hi = pltpu.bitcast(u & jnp.uint32(0xFFFF0000), F32)
    lo = pltpu.bitcast(u << 16, F32)
    return hi, lo


def _sc_gather_rows(table, idx):
    _, w = table.shape
    m = idx.shape[0]
    info = plsc.get_sparse_core_info()
    nc, ns = info.num_cores, info.num_subcores
    assert m % (nc * ns * SC_GATHER_ROWS) == 0
    per_w = m // (nc * ns)
    n_chunks = per_w // SC_GATHER_ROWS
    mesh = plsc.VectorSubcoreMesh(core_axis_name="c", subcore_axis_name="s")

    @functools.partial(
        pl.kernel, mesh=mesh,
        out_type=jax.ShapeDtypeStruct((m, w), table.dtype),
        scratch_types=[pltpu.VMEM((SC_GATHER_ROWS,), jnp.int32),
                       pltpu.VMEM((SC_GATHER_ROWS, w), table.dtype),
                       pltpu.SemaphoreType.DMA],
    )
    def gather(table_hbm, idx_hbm, out_hbm, idx_v, rows_v, sem):
        wid = lax.axis_index("s") * nc + lax.axis_index("c")
        base = wid * per_w

        @pl.loop(0, n_chunks)
        def _(ci):
            off = pl.multiple_of(base + ci * SC_GATHER_ROWS, SC_GATHER_ROWS)
            pltpu.sync_copy(idx_hbm.at[pl.ds(off, SC_GATHER_ROWS)], idx_v)
            pltpu.async_copy(table_hbm.at[idx_v], rows_v, sem).wait()
            pltpu.sync_copy(rows_v, out_hbm.at[pl.ds(off, SC_GATHER_ROWS)])

    return gather(table, idx)


def _norm_matmul_kernel(plan, n_chunk, x_ref, gains_ref, *refs):
    n_w = len(plan)
    w_refs, o_refs = refs[:n_w], refs[n_w:]
    x = x_ref[...]
    inv = lax.rsqrt(jnp.mean(x * x, axis=-1, keepdims=True) + NORM_EPS)
    xn = {}
    for (g, _, _) in plan:
        if g not in xn:
            xn[g] = (x * inv * gains_ref[g:g + 1, :]).astype(BF16)
    for (g, scale, _), w_ref, o_ref in zip(plan, w_refs, o_refs):
        n = w_ref.shape[1]
        for n0 in range(0, n, n_chunk):
            n1 = min(n, n0 + n_chunk)
            acc = jnp.dot(xn[g], w_ref[:, n0:n1], preferred_element_type=F32)
            if scale != 1.0:
                acc = acc * scale
            o_ref[:, n0:n1] = acc.astype(o_ref.dtype)


def _norm_matmul(x, gains, ws, plan):
    t, d = x.shape
    tm = TOKEN_TILE
    in_specs = [pl.BlockSpec((tm, d), lambda i: (i, 0)),
                pl.BlockSpec(gains.shape, lambda i: (0, 0))]
    in_specs += [pl.BlockSpec(w.shape, lambda i: (0, 0)) for w in ws]
    out_specs = [pl.BlockSpec((tm, w.shape[1]), lambda i: (i, 0)) for w in ws]
    out_shape = [jax.ShapeDtypeStruct((t, w.shape[1]), p[2]) for w, p in zip(ws, plan)]
    return pl.pallas_call(
        functools.partial(_norm_matmul_kernel, tuple(plan), 512),
        grid=(t // tm,), in_specs=in_specs, out_specs=out_specs, out_shape=out_shape,
        compiler_params=_params("parallel"), name="norm_matmul",
    )(x, gains, *ws)


def _gdn_kernel(n_heads, ts, qkv_ref, z_ref, ab_ref, conv_ref, alog_ref, dtb_ref, onorm_ref,
                o_ref, xpad_ref, state_ref, g_ref, beta_ref):
    c = GDN_CHUNK
    dk = GDN_HEAD_DIM
    tb = pl.program_id(1)
    n_ch = ts // c
    qkv_dim = qkv_ref.shape[-1]

    @pl.when(tb == 0)
    def _():
        state_ref[...] = jnp.zeros_like(state_ref)
        xpad_ref[0:8, :] = jnp.zeros((8, qkv_dim), F32)

    xpad_ref[8:8 + ts, :] = qkv_ref[0]

    ab = ab_ref[0]
    g_all = -jnp.exp(alog_ref[...]) * _softplus(ab + dtb_ref[...])
    beta_all = _sigmoid(ab)
    rin = lax.broadcasted_iota(jnp.int32, (ts, LANES), 0) & (c - 1)
    for h in range(n_heads):
        gh = jnp.broadcast_to(g_all[:, h:h + 1], (ts, LANES))
        shift = 1
        while shift < c:
            gh = gh + jnp.where(rin >= shift, pltpu.roll(gh, shift, 0), 0.0)
            shift *= 2
        g_ref[h] = gh
        beta_ref[h] = jnp.broadcast_to(beta_all[:, n_heads + h:n_heads + h + 1], (ts, LANES))

    row = lax.broadcasted_iota(jnp.int32, (c, c), 0)
    col = lax.broadcasted_iota(jnp.int32, (c, c), 1)
    incl = row >= col
    strict = row > col

    def conv_silu(r0, c0):
        win = xpad_ref[pl.ds(r0, c + 8), c0:c0 + LANES]
        acc = jnp.zeros((c, LANES), F32)
        for j in range(GDN_CONV):
            off = 8 - (GDN_CONV - 1) + j
            acc = acc + win[off:off + c, :] * conv_ref[j:j + 1, c0:c0 + LANES]
        return _silu(acc)

    def l2n(t):
        return t * lax.rsqrt(jnp.sum(t * t, axis=-1, keepdims=True) + NORM_EPS)

    heads = range(n_heads)
    eye = (row == col).astype(F32)
    same_blk = (row >> 4) == (col >> 4)

    def chunk_body(n, carry):
        r0 = pl.multiple_of(n * c, c)
        q = [l2n(conv_silu(r0, h * dk)) * (dk ** -0.5) for h in heads]
        k = [l2n(conv_silu(r0, (n_heads + h) * dk)) for h in heads]
        v = [conv_silu(r0, (2 * n_heads + h) * dk) for h in heads]
        gc = [g_ref[h, pl.ds(r0, c), :] for h in heads]
        beta = [beta_ref[h, pl.ds(r0, c), :] for h in heads]
        kb = [k[h] * beta[h] for h in heads]
        kk = [_dot_nt(kb[h], k[h]) for h in heads]
        qk = [_dot_nt(q[h], k[h]) for h in heads]
        decay = []
        for h in heads:
            diff = gc[h][:, 0:c] - gc[h].T[0:c, :]
            decay.append(jnp.where(incl, jnp.exp(jnp.where(incl, diff, 0.0)), 0.0))
        lm = [jnp.where(strict, kk[h] * decay[h], 0.0) for h in heads]
        attn = [jnp.where(incl, qk[h] * decay[h], 0.0) for h in heads]
        ld = [jnp.where(same_blk, lm[h], 0.0) for h in heads]
        lo = [lm[h] - ld[h] for h in heads]
        p = [eye - ld[h] for h in heads]
        sq = [_dot(ld[h], ld[h]) for h in heads]
        for _ in range(2):
            pn = [_dot(p[h], sq[h]) for h in heads]
            sq2 = [_dot(sq[h], sq[h]) for h in heads]
            p = [p[h] + pn[h] for h in heads]
            sq = sq2
        dinv = [p[h] + _dot(p[h], sq[h]) for h in heads]
        m = [_dot(dinv[h], lo[h]) for h in heads]
        m2 = [_dot(m[h], m[h]) for h in heads]
        r = [(eye - m[h]) + _dot(eye - m[h], m2[h]) for h in heads]
        tmat = [_dot(r[h], dinv[h]) for h in heads]
        eg = [jnp.exp(gc[h]) for h in heads]
        rhs = [jnp.concatenate([v[h] * beta[h], kb[h] * eg[h]], axis=1) for h in heads]
        uw = [_dot(tmat[h], rhs[h]) for h in heads]
        s = [state_ref[h] for h in heads]
        lhs = [jnp.concatenate([uw[h][:, dk:], q[h] * eg[h]], axis=0) for h in heads]
        ws_qs = [_dot(lhs[h], s[h]) for h in heads]
        v_new = [uw[h][:, :dk] - ws_qs[h][:c] for h in heads]
        g_last = [gc[h][c - 1:c, :] for h in heads]
        k_dec = [k[h] * jnp.exp(g_last[h] - gc[h]) for h in heads]
        av = [_dot(attn[h], v_new[h]) for h in heads]
        kv = [_dot_tn(k_dec[h], v_new[h]) for h in heads]
        for h in heads:
            state_ref[h] = s[h] * jnp.exp(g_last[h]) + kv[h]
            o = ws_qs[h][c:] + av[h]
            o = o * lax.rsqrt(jnp.mean(o * o, axis=-1, keepdims=True) + NORM_EPS) * onorm_ref[...]
            zz = z_ref[0, pl.ds(r0, c), h * dk:(h + 1) * dk]
            o_ref[0, pl.ds(r0, c), h * dk:(h + 1) * dk] = (o * _silu(zz)).astype(o_ref.dtype)
        return carry

    lax.fori_loop(0, n_ch, chunk_body, 0)
    xpad_ref[0:8, :] = xpad_ref[ts:ts + 8, :]


def _gated_deltanet_core(qkv, z, ab, conv_w, alog, dtb, onorm):
    b, s, qkv_dim = qkv.shape
    n_heads = GDN_HEADS
    ts = min(GDN_TIME_BLOCK, s)
    vd = z.shape[-1]
    kern = functools.partial(_gdn_kernel, n_heads, ts)
    return pl.pallas_call(
        kern,
        grid=(b, s // ts),
        in_specs=[
            pl.BlockSpec((1, ts, qkv_dim), lambda i, j: (i, j, 0)),
            pl.BlockSpec((1, ts, vd), lambda i, j: (i, j, 0)),
            pl.BlockSpec((1, ts, LANES), lambda i, j: (i, j, 0)),
            pl.BlockSpec(conv_w.shape, lambda i, j: (0, 0)),
            pl.BlockSpec((1, LANES), lambda i, j: (0, 0)),
            pl.BlockSpec((1, LANES), lambda i, j: (0, 0)),
            pl.BlockSpec((1, GDN_HEAD_DIM), lambda i, j: (0, 0)),
        ],
        out_specs=pl.BlockSpec((1, ts, vd), lambda i, j: (i, j, 0)),
        out_shape=jax.ShapeDtypeStruct((b, s, vd), BF16),
        scratch_shapes=[
            pltpu.VMEM((ts + 8, qkv_dim), F32),
            pltpu.VMEM((n_heads, GDN_HEAD_DIM, GDN_HEAD_DIM), F32),
            pltpu.VMEM((n_heads, ts, LANES), F32),
            pltpu.VMEM((n_heads, ts, LANES), F32),
        ],
        compiler_params=_params("parallel", "arbitrary"), name="gdn_delta_rule",
    )(qkv, z, ab, conv_w, alog, dtb, onorm)


def _sb_attn_kernel(q_ref, kt_ref, v_ref, o_ref, acc_ref, cs_ref, z_ref, p_ref):
    blk = SB_BLOCK
    dh = SB_HEAD_DIM
    grp = SB_GROUP
    i = pl.program_id(2)
    rows = grp * blk
    n_pairs = 1 + i // 2
    qb = q_ref[0]
    q4 = jnp.concatenate([qb[:, g * dh:(g + 1) * dh] for g in range(grp)], axis=0)
    krow = lax.broadcasted_iota(jnp.int32, (blk, blk), 0)
    kcol = lax.broadcasted_iota(jnp.int32, (blk, blk), 1)
    neg_suffix = jnp.where(krow >= kcol, -1.0, 0.0).astype(BF16)

    def pair_blocks(k):
        ja = i - 2 * k
        return jnp.maximum(ja, 0), jnp.maximum(ja - 1, 0), (ja >= 1).astype(F32)

    def logits(k):
        ja, jb, _ = pair_blocks(k)
        z_ref[0] = jnp.dot(q4, kt_ref[0, 0, ja], preferred_element_type=F32)
        z_ref[1] = jnp.dot(q4, kt_ref[0, 0, jb], preferred_element_type=F32)

    def weights(mask_a):
        masks = [mask_a, None]
        zs = [z_ref[0], z_ref[1]]
        sps = []
        for z, mask in zip(zs, masks):
            sp = jnp.maximum(z, 0.0) + jnp.log(1.0 + jnp.exp2(jnp.abs(z) * -LOG2E))
            if mask is not None:
                sp = jnp.where(mask, sp, 0.0)
            sps.append(sp)
        sufs = [jnp.dot(sp.astype(BF16), neg_suffix, preferred_element_type=F32) for sp in sps]
        rss = [jnp.sum(sp, axis=-1, keepdims=True) for sp in sps]
        cs = cs_ref[...]
        for n, (z, suf, rs, mask) in enumerate(zip(zs, sufs, rss, masks)):
            p = jnp.exp2(((z + suf) - cs) * LOG2E)
            if mask is not None:
                p = jnp.where(mask, p, 0.0)
            p_ref[n] = p.astype(BF16)
            cs = cs + rs
        cs_ref[...] = cs

    def weighted_values(k):
        ja, jb, valid_b = pair_blocks(k)
        vb = (v_ref[0, 0, jb].astype(F32) * valid_b).astype(BF16)
        acc_ref[...] += (jnp.dot(p_ref[0], v_ref[0, 0, ja], preferred_element_type=F32)
                         + jnp.dot(p_ref[1], vb, preferred_element_type=F32))

    acc_ref[...] = jnp.zeros_like(acc_ref)
    cs_ref[...] = jnp.zeros_like(cs_ref)
    t_in = lax.broadcasted_iota(jnp.int32, (rows, blk), 0) & (blk - 1)
    s_in = lax.broadcasted_iota(jnp.int32, (rows, blk), 1)
    logits(0)
    weights(s_in < t_in)
    logits(1)

    def body(k, c):
        weighted_values(k - 1)
        weights(None)
        logits(k + 1)
        return c

    lax.fori_loop(1, n_pairs, body, 0)
    weighted_values(n_pairs - 1)
    acc = acc_ref[...]
    for g in range(grp):
        o_ref[0, :, g * dh:(g + 1) * dh] = acc[g * blk:(g + 1) * blk, :].astype(o_ref.dtype)


def _stick_breaking_attention(q, kt, v):
    b, s, qd = q.shape
    nb = s // SB_BLOCK
    gw = SB_GROUP * SB_HEAD_DIM
    rows = SB_GROUP * SB_BLOCK
    return pl.pallas_call(
        _sb_attn_kernel,
        grid=(b, SB_KV_HEADS, nb),
        in_specs=[
            pl.BlockSpec((1, SB_BLOCK, gw), lambda bi, g, i: (bi, i, g)),
            pl.BlockSpec((1, 1, nb, SB_HEAD_DIM, SB_BLOCK), lambda bi, g, i: (bi, g, 0, 0, 0)),
            pl.BlockSpec((1, 1, nb, SB_BLOCK, SB_HEAD_DIM), lambda bi, g, i: (bi, g, 0, 0, 0)),
        ],
        out_specs=pl.BlockSpec((1, SB_BLOCK, gw), lambda bi, g, i: (bi, i, g)),
        out_shape=jax.ShapeDtypeStruct((b, s, qd), BF16),
        scratch_shapes=[pltpu.VMEM((rows, SB_HEAD_DIM), F32), pltpu.VMEM((rows, LANES), F32),
                        pltpu.VMEM((2, rows, SB_BLOCK), F32), pltpu.VMEM((2, rows, SB_BLOCK), BF16)],
        compiler_params=_params("parallel", "parallel", "arbitrary"), name="stick_breaking_attention",
    )(q, kt, v)


def _out_router_kernel(mix_ref, h_ref, wout_ref, gain_ref, wrh_ref, wrl_ref, br_ref,
                       h1_ref, xn_ref, rw_ref, ri_ref, cnt_ref, run_ref):
    h1 = h_ref[...] + jnp.dot(mix_ref[...], wout_ref[...], preferred_element_type=F32)
    h1_ref[...] = h1
    xn = _rms(h1, gain_ref[...])
    xh = xn.astype(BF16)
    xn_ref[...] = _pack_halves(xn)
    xl = (xn - xh.astype(F32)).astype(BF16)
    logits = (jnp.dot(xh, wrh_ref[...], preferred_element_type=F32)
              + jnp.dot(xl, wrh_ref[...], preferred_element_type=F32)
              + jnp.dot(xh, wrl_ref[...], preferred_element_type=F32)
              + br_ref[...])
    lane = lax.broadcasted_iota(jnp.int32, logits.shape, 1).astype(F32)
    neg = jnp.float32(-jnp.inf)
    big = jnp.float32(1e9)
    gl = jnp.where(lane < MOE_GROUPS, logits, neg)
    gmax = jnp.max(gl, axis=-1, keepdims=True)
    gidx = jnp.min(jnp.where(gl == gmax, lane, big), axis=-1, keepdims=True)
    gp = 1.0 / jnp.sum(jnp.exp(gl - gmax), axis=-1, keepdims=True)
    lo = MOE_GROUPS + gidx * MOE_EXPERTS_PER_GROUP
    el = jnp.where((lane >= lo) & (lane < lo + MOE_EXPERTS_PER_GROUP), logits, neg)
    m1 = jnp.max(el, axis=-1, keepdims=True)
    i1 = jnp.min(jnp.where(el == m1, lane, big), axis=-1, keepdims=True)
    el2 = jnp.where(lane == i1, neg, el)
    m2 = jnp.max(el2, axis=-1, keepdims=True)
    i2 = jnp.min(jnp.where(el2 == m2, lane, big), axis=-1, keepdims=True)
    e2 = jnp.exp(m2 - m1)
    w1 = gp / (1.0 + e2)
    w2 = gp * e2 / (1.0 + e2)
    rw_ref[...] = jnp.where(lane == 0, w1, jnp.where(lane == 1, w2, 0.0))

    @pl.when(pl.program_id(0) == 0)
    def _():
        run_ref[...] = jnp.zeros_like(run_ref)

    tm = logits.shape[0]
    trow = lax.broadcasted_iota(jnp.int32, (tm, tm), 0)
    tcol = lax.broadcasted_iota(jnp.int32, (tm, tm), 1)
    before = jnp.where(tcol < trow, 1.0, 0.0).astype(BF16)
    hot1 = lane == i1
    hot2 = lane == i2
    oh1 = jnp.where(hot1, 1.0, 0.0)
    oh2 = jnp.where(hot2, 1.0, 0.0)
    prior1 = jnp.dot(before, oh1.astype(BF16), preferred_element_type=F32)
    prior2 = jnp.dot(before, oh2.astype(BF16), preferred_element_type=F32)
    cnt1 = jnp.sum(oh1, axis=0, keepdims=True)
    cnt2 = jnp.sum(oh2, axis=0, keepdims=True)
    run = run_ref[...]
    rank1 = jnp.sum(jnp.where(hot1, prior1 + run, 0.0), axis=-1, keepdims=True)
    rank2 = jnp.sum(jnp.where(hot2, prior2 + (run + cnt1), 0.0), axis=-1, keepdims=True)
    run = run + cnt1 + cnt2
    run_ref[...] = run
    cnt_ref[...] = run
    ri = jnp.where(lane == 0, i1 - MOE_GROUPS, jnp.where(lane == 1, i2 - MOE_GROUPS,
                   jnp.where(lane == 2, rank1, jnp.where(lane == 3, rank2, 0.0))))
    ri_ref[...] = ri.T[0:ROUTE_COLS, :].astype(jnp.int32)


def _out_router(mix, h, w_out, gain, w_router, b_router):
    t, d = h.shape
    tm = TOKEN_TILE
    kd = mix.shape[1]
    row = lambda i: (i, 0)
    fix = lambda i: (0, 0)
    wr_hi = w_router.astype(BF16)
    wr_lo = (w_router - wr_hi.astype(F32)).astype(BF16)
    return pl.pallas_call(
        _out_router_kernel,
        grid=(t // tm,),
        in_specs=[pl.BlockSpec((tm, kd), row), pl.BlockSpec((tm, d), row),
                  pl.BlockSpec(w_out.shape, fix), pl.BlockSpec((1, d), fix),
                  pl.BlockSpec(w_router.shape, fix), pl.BlockSpec(w_router.shape, fix),
                  pl.BlockSpec((1, LANES), fix)],
        out_specs=[pl.BlockSpec((tm, d), row), pl.BlockSpec((tm, d // 2), row),
                   pl.BlockSpec((tm, LANES), row), pl.BlockSpec((ROUTE_COLS, tm), lambda i: (0, i)),
                   pl.BlockSpec((1, LANES), fix)],
        out_shape=[jax.ShapeDtypeStruct((t, d), F32), jax.ShapeDtypeStruct((t, d // 2), jnp.uint32),
                   jax.ShapeDtypeStruct((t, LANES), F32), jax.ShapeDtypeStruct((ROUTE_COLS, t), jnp.int32),
                   jax.ShapeDtypeStruct((1, LANES), F32)],
        scratch_shapes=[pltpu.VMEM((1, LANES), F32)],
        compiler_params=_params("arbitrary"), name="out_proj_router",
    )(mix, h, w_out, gain, wr_hi, wr_lo, b_router)


def _expert_ffn_kernel(te_ref, nu_ref, xs_ref, wg_ref, wu_ref, wd_ref, ys_ref):
    i = pl.program_id(0)

    @pl.when(i < nu_ref[0])
    def _():
        xa, xb = _unpack_halves(xs_ref[...])
        xa = xa.astype(BF16)
        xb = xb.astype(BF16)
        half = xa.shape[1]
        wg = wg_ref[0].astype(BF16)
        wu = wu_ref[0].astype(BF16)
        hg = (jnp.dot(xa, wg[:half], preferred_element_type=F32)
              + jnp.dot(xb, wg[half:], preferred_element_type=F32))
        hu = (jnp.dot(xa, wu[:half], preferred_element_type=F32)
              + jnp.dot(xb, wu[half:], preferred_element_type=F32))
        hh = _silu(hg) * hu
        ys_ref[...] = _pack_halves(jnp.dot(hh.astype(BF16), wd_ref[0].astype(BF16),
                                           preferred_element_type=F32))

    @pl.when(i >= nu_ref[0])
    def _():
        ys_ref[...] = jnp.zeros_like(ys_ref)


def _expert_ffn(tile_expert, n_used, xs, w_gate, w_up, w_down):
    r, dw = xs.shape
    d = 2 * dw
    tm = FFN_TILE
    f = w_gate.shape[-1]
    grid_spec = pltpu.PrefetchScalarGridSpec(
        num_scalar_prefetch=2,
        grid=(r // tm,),
        in_specs=[
            pl.BlockSpec((tm, dw), lambda i, te, nu: (i, 0)),
            pl.BlockSpec((1, d, f), lambda i, te, nu: (te[i], 0, 0)),
            pl.BlockSpec((1, d, f), lambda i, te, nu: (te[i], 0, 0)),
            pl.BlockSpec((1, f, d), lambda i, te, nu: (te[i], 0, 0)),
        ],
        out_specs=pl.BlockSpec((tm, dw), lambda i, te, nu: (i, 0)),
    )
    return pl.pallas_call(
        _expert_ffn_kernel, grid_spec=grid_spec,
        out_shape=jax.ShapeDtypeStruct((r, dw), jnp.uint32),
        compiler_params=_params("arbitrary"), name="expert_ffn",
    )(tile_expert, n_used, xs, w_gate, w_up, w_down)


def _ple_kernel(final, h_ref, m0_ref, m1_ref, rw_ref, p_ref, gain_ref, wg_ref, wp_ref, fgain_ref, o_ref):
    rw = rw_ref[...]
    a0, b0 = _unpack_halves(m0_ref[...])
    a1, b1 = _unpack_halves(m1_ref[...])
    w0 = rw[:, 0:1]
    w1 = rw[:, 1:2]
    moe = jnp.concatenate([w0 * a0 + w1 * a1, w0 * b0 + w1 * b1], axis=1)
    h2 = h_ref[...] + moe
    xn = _rms(h2, gain_ref[...]).astype(BF16)
    gate = _sigmoid(jnp.dot(xn, wg_ref[...], preferred_element_type=F32))
    emb = jnp.dot(p_ref[...].astype(BF16), wp_ref[...], preferred_element_type=F32)
    h3 = h2 + gate * emb
    if final:
        h3 = _rms(h3, fgain_ref[...])
    o_ref[...] = h3


def _ple(h, m01, rw, p, gain, w_gate, w_proj, final_gain, final):
    t, d = h.shape
    tm = TOKEN_TILE
    pd = p.shape[1]
    nt = t // tm
    row = lambda i: (i, 0)
    fix = lambda i: (0, 0)
    return pl.pallas_call(
        functools.partial(_ple_kernel, final),
        grid=(nt,),
        in_specs=[pl.BlockSpec((tm, d), row), pl.BlockSpec((tm, d // 2), row),
                  pl.BlockSpec((tm, d // 2), lambda i: (i + nt, 0)),
                  pl.BlockSpec((tm, LANES), row), pl.BlockSpec((tm, pd), row), pl.BlockSpec((1, d), fix),
                  pl.BlockSpec(w_gate.shape, fix), pl.BlockSpec(w_proj.shape, fix),
                  pl.BlockSpec((1, d), fix)],
        out_specs=pl.BlockSpec((tm, d), row),
        out_shape=jax.ShapeDtypeStruct((t, d), F32),
        compiler_params=_params("parallel"), name="moe_residual_ple",
    )(h, m01, m01, rw, p, gain, w_gate, w_proj, final_gain)


def _routing_tables(ri, counts):
    t = ri.shape[1]
    tm = FFN_TILE
    n_rows = 2 * t + MOE_N_EXPERTS * tm
    padded = ((counts + tm - 1) // tm) * tm
    ends = jnp.cumsum(padded)
    starts = ends - padded
    ids = ri[0:2]
    offs = jnp.zeros_like(ids)
    for e in range(MOE_N_EXPERTS):
        offs = jnp.where(ids == e, starts[e], offs)
    dest = ri[2:4] + offs
    tok = jnp.broadcast_to(jnp.arange(t, dtype=jnp.int32)[None, :], (2, t))
    src_tok = jnp.zeros((n_rows,), jnp.int32).at[dest.reshape(-1)].set(
        tok.reshape(-1), unique_indices=True, indices_are_sorted=False)
    tile_start = jnp.arange(n_rows // tm, dtype=jnp.int32) * tm
    tile_expert = jnp.minimum(jnp.sum((ends[None, :] <= tile_start[:, None]).astype(jnp.int32), axis=1),
                              MOE_N_EXPERTS - 1)
    n_used = (ends[-1] // tm).astype(jnp.int32).reshape(1)
    return src_tok, tile_expert, n_used, dest


def _moe_ple(mix, h, w_out, moe_gain, w_router, b_router, w_gate, w_up, w_down,
             p, ple_gain, ple_w_gate, ple_w_proj, final_gain, final):
    h1, xn, rw, ri, cnt = _out_router(mix, h, w_out, moe_gain, w_router, b_router)
    counts = cnt[0, MOE_GROUPS:MOE_GROUPS + MOE_N_EXPERTS].astype(jnp.int32)
    src_tok, tile_expert, n_used, dest = _routing_tables(ri, counts)
    xs = _sc_gather_rows(xn, src_tok)
    ys = _expert_ffn(tile_expert, n_used, xs, w_gate, w_up, w_down)
    m01 = _sc_gather_rows(ys, dest.reshape(-1))
    return _ple(h1, m01, rw, p, ple_gain, ple_w_gate, ple_w_proj, final_gain, final)


def _pad_lanes(v):
    return jnp.zeros((1, LANES), F32).at[0, :v.shape[0]].set(v.astype(F32))


def kernel(x, p, attn_norm, moe_norm, ple_norm, gdn_w_in, gdn_conv, gdn_a_log, gdn_dt_bias, gdn_o_norm, gdn_w_out, kv_norm, w_kv, sb_w_q, sb_w_out, moe_w_group, moe_b_group, moe_w_expert, moe_b_expert, moe_w_gate, moe_w_up, moe_w_down, ple_w_gate, ple_w_proj, final_norm):
    d = x.shape[-1]
    kd = GDN_HEADS * GDN_HEAD_DIM
    qkv_dim = 3 * kd

    def router_params(i):
        wr = jnp.zeros((d, LANES), F32)
        wr = wr.at[:, :MOE_GROUPS].set(moe_w_group[i])
        wr = wr.at[:, MOE_GROUPS:MOE_GROUPS + MOE_N_EXPERTS].set(moe_w_expert[i])
        br = jnp.zeros((1, LANES), F32)
        br = br.at[0, :MOE_GROUPS].set(moe_b_group[i])
        br = br.at[0, MOE_GROUPS:MOE_GROUPS + MOE_N_EXPERTS].set(moe_b_expert[i])
        return wr, br

    def expert_params(i):
        return (moe_w_gate[i].reshape(MOE_N_EXPERTS, d, MOE_D_EXPERT),
                moe_w_up[i].reshape(MOE_N_EXPERTS, d, MOE_D_EXPERT),
                moe_w_down[i].reshape(MOE_N_EXPERTS, MOE_D_EXPERT, d))

    w_in = gdn_w_in[0]
    w_ab = jnp.zeros((d, LANES), F32).at[:, :2 * GDN_HEADS].set(w_in[:, qkv_dim + kd:])
    w_in_parts = [w_in[:, :qkv_dim].astype(BF16), w_in[:, qkv_dim:qkv_dim + kd].astype(BF16), w_ab.astype(BF16)]
    dtb = jnp.zeros((1, LANES), F32).at[0, :GDN_HEADS].set(gdn_dt_bias[0])
    alog = _pad_lanes(gdn_a_log[0])
    routers = [router_params(0), router_params(1)]
    experts = [expert_params(0), expert_params(1)]
    mix_w_out = [gdn_w_out[0].astype(BF16), sb_w_out[0].astype(BF16)]
    ple_wg = [ple_w_gate[0].astype(BF16), ple_w_gate[1].astype(BF16)]
    ple_wp = [ple_w_proj[0].astype(BF16), ple_w_proj[1].astype(BF16)]
    gains1 = jnp.stack([attn_norm[1], kv_norm], axis=0)
    w_q = sb_w_q[0].astype(BF16)
    w_kv_b = w_kv.astype(BF16)
    fgain = final_norm.reshape(1, d)

    def forward(xg, pg):
        b, s, _ = xg.shape
        t = b * s
        h = xg.reshape(t, d)
        pf = pg.reshape(pg.shape[0], t, pg.shape[-1])
        qkv, z, ab = _norm_matmul(h, attn_norm[0:1], w_in_parts,
                                  [(0, 1.0, F32), (0, 1.0, F32), (0, 1.0, F32)])
        og = _gated_deltanet_core(
            qkv.reshape(b, s, qkv_dim), z.reshape(b, s, kd), ab.reshape(b, s, LANES),
            gdn_conv[0], alog, dtb, gdn_o_norm[0].reshape(1, GDN_HEAD_DIM))
        h = _moe_ple(og.reshape(t, kd), h, mix_w_out[0], moe_norm[0:1], *routers[0], *experts[0],
                     pf[0], ple_norm[0:1], ple_wg[0], ple_wp[0], fgain, False)
        q, kv = _norm_matmul(h, gains1, [w_q, w_kv_b],
                             [(0, SB_HEAD_DIM ** -0.5, BF16), (1, 1.0, BF16)])
        nb = s // SB_BLOCK
        kvw = SB_KV_HEADS * SB_HEAD_DIM
        k5 = kv[:, :kvw].reshape(b, nb, SB_BLOCK, SB_KV_HEADS, SB_HEAD_DIM)
        v5 = kv[:, kvw:].reshape(b, nb, SB_BLOCK, SB_KV_HEADS, SB_HEAD_DIM)
        kt = k5.transpose(0, 3, 1, 4, 2)
        vv = v5.transpose(0, 3, 1, 2, 4)
        oa = _stick_breaking_attention(q.reshape(b, s, -1), kt, vv)
        out = _moe_ple(oa.reshape(t, -1), h, mix_w_out[1], moe_norm[1:2], *routers[1], *experts[1],
                       pf[1], ple_norm[1:2], ple_wg[1], ple_wp[1], fgain, True)
        return out.reshape(b, s, d)

    bt = x.shape[0]
    n_groups = BATCH_GROUPS if bt % BATCH_GROUPS == 0 else 1
    bg = bt // n_groups
    outs = [forward(x[g * bg:(g + 1) * bg], p[:, g * bg:(g + 1) * bg]) for g in range(n_groups)]
    return outs[0] if n_groups == 1 else jnp.concatenate(outs, axis=0)
```

```python
import functools

import jax
import jax.numpy as jnp
from jax import lax
from jax.experimental import pallas as pl
from jax.experimental.pallas import tpu as pltpu
from jax.experimental.pallas import tpu_sc as plsc

NORM_EPS = 1e-6
LOG2E = 1.4426950408889634
LANES = 128
GDN_HEADS = 8
GDN_HEAD_DIM = 128
GDN_CONV = 4
GDN_CHUNK = 64
SB_Q_HEADS = 16
SB_KV_HEADS = 4
SB_GROUP = SB_Q_HEADS // SB_KV_HEADS
SB_HEAD_DIM = 64
SB_BLOCK = 128
MOE_GROUPS = 4
MOE_EXPERTS_PER_GROUP = 8
MOE_N_EXPERTS = MOE_GROUPS * MOE_EXPERTS_PER_GROUP
MOE_D_EXPERT = 256

VMEM_LIMIT = 56 * 1024 * 1024
TOKEN_TILE = 256
FFN_TILE = 256
GDN_TIME_BLOCK = 512
ROUTE_COLS = 8
SC_GATHER_ROWS = 64
SC_LANES = 16
SC_INDEX_CHUNK = 4096

F32 = jnp.float32
BF16 = jnp.bfloat16


def _params(*sem):
    return pltpu.CompilerParams(dimension_semantics=sem, vmem_limit_bytes=VMEM_LIMIT)


def _dot(a, b):
    return jnp.dot(a.astype(BF16), b.astype(BF16), preferred_element_type=F32)


def _dot_nt(a, b):
    return lax.dot_general(a.astype(BF16), b.astype(BF16), (((1,), (1,)), ((), ())),
                           preferred_element_type=F32)


def _dot_tn(a, b):
    return lax.dot_general(a.astype(BF16), b.astype(BF16), (((0,), (0,)), ((), ())),
                           preferred_element_type=F32)


def _dot_f32(a, b):
    return jnp.dot(a, b, precision=lax.Precision.HIGHEST, preferred_element_type=F32)


def _rms(x, gain):
    return x * lax.rsqrt(jnp.mean(x * x, axis=-1, keepdims=True) + NORM_EPS) * gain


def _silu(x):
    return x * (1.0 / (1.0 + jnp.exp(-x)))


def _sigmoid(x):
    return 1.0 / (1.0 + jnp.exp(-x))


def _softplus(x):
    return jnp.maximum(x, 0.0) + jnp.log(1.0 + jnp.exp(-jnp.abs(x)))


def _pack_halves(x):
    w = x.shape[1] // 2
    hi = pltpu.bitcast(x[:, :w].astype(BF16).astype(F32), jnp.uint32)
    lo = pltpu.bitcast(x[:, w:].astype(BF16).astype(F32), jnp.uint32)
    return (hi & jnp.uint32(0xFFFF0000)) | (lo >> 16)


def _unpack_halves(u):
    hi = pltpu.bitcast(u & jnp.uint32(0xFFFF0000), F32)
    lo = pltpu.bitcast(u << 16, F32)
    return hi, lo


def _sc_gather_rows(table, idx):
    _, w = table.shape
    m = idx.shape[0]
    info = plsc.get_sparse_core_info()
    nc, ns = info.num_cores, info.num_subcores
    rows = SC_GATHER_ROWS
    assert m % (nc * ns * rows * 2) == 0
    per_w = m // (nc * ns)
    n_pairs = per_w // (2 * rows)
    mesh = plsc.VectorSubcoreMesh(core_axis_name="c", subcore_axis_name="s")

    @functools.partial(
        pl.kernel, mesh=mesh,
        out_type=jax.ShapeDtypeStruct((m, w), table.dtype),
        scratch_types=[pltpu.VMEM((2, rows), jnp.int32),
                       pltpu.VMEM((2, rows, w), table.dtype),
                       pltpu.SemaphoreType.DMA((2,)),
                       pltpu.SemaphoreType.DMA((2,))],
    )
    def gather(table_hbm, idx_hbm, out_hbm, idx_v, rows_v, gsem, wsem):
        wid = lax.axis_index("s") * nc + lax.axis_index("c")
        base = wid * per_w

        def fetch(chunk, slot):
            off = pl.multiple_of(base + chunk * rows, rows)
            pltpu.sync_copy(idx_hbm.at[pl.ds(off, rows)], idx_v.at[slot])
            return pltpu.async_copy(table_hbm.at[idx_v.at[slot]], rows_v.at[slot], gsem.at[slot])

        def write(chunk, slot):
            off = pl.multiple_of(base + chunk * rows, rows)
            return pltpu.async_copy(rows_v.at[slot], out_hbm.at[pl.ds(off, rows)], wsem.at[slot])

        @pl.loop(0, n_pairs)
        def _(pi):
            g0 = fetch(2 * pi, 0)
            g1 = fetch(2 * pi + 1, 1)
            g0.wait()
            w0 = write(2 * pi, 0)
            g1.wait()
            w1 = write(2 * pi + 1, 1)
            w0.wait()
            w1.wait()

    return gather(table, idx)


def _sc_invert_slots(dest, n_rows, n_tokens):
    n_pairs = dest.shape[0]
    ch = SC_INDEX_CHUNK
    assert n_pairs % ch == 0 and n_rows % SC_LANES == 0
    nc = plsc.get_sparse_core_info().num_cores
    mesh = plsc.VectorSubcoreMesh(core_axis_name="c", subcore_axis_name="s")

    @functools.partial(
        pl.kernel, mesh=mesh,
        out_type=jax.ShapeDtypeStruct((n_rows,), jnp.int32),
        scratch_types=[pltpu.VMEM((n_rows,), jnp.int32), pltpu.VMEM((ch,), jnp.int32)],
        compiler_params=pltpu.CompilerParams(needs_layout_passes=False),
    )
    def invert(dest_hbm, out_hbm, tab_v, dest_v):
        wid = lax.axis_index("s") * nc + lax.axis_index("c")

        @pl.when(wid == 0)
        def _():
            zeros = jnp.zeros((SC_LANES,), jnp.int32)
            lanes = lax.iota(jnp.int32, SC_LANES)

            @pl.loop(0, n_rows // SC_LANES)
            def _(i):
                tab_v[pl.ds(pl.multiple_of(i * SC_LANES, SC_LANES), SC_LANES)] = zeros

            @pl.loop(0, n_pairs // ch)
            def _(c):
                pltpu.sync_copy(dest_hbm.at[pl.ds(pl.multiple_of(c * ch, ch), ch)], dest_v)

                @pl.loop(0, ch // SC_LANES)
                def _(j):
                    d = dest_v[pl.ds(pl.multiple_of(j * SC_LANES, SC_LANES), SC_LANES)]
                    pair = c * ch + j * SC_LANES + lanes
                    plsc.store_scatter(tab_v, [d], lax.rem(pair, n_tokens))

            pltpu.sync_copy(tab_v, out_hbm)

    return invert(dest)


def _norm_matmul_kernel(plan, n_chunk, x_ref, gains_ref, *refs):
    n_w = len(plan)
    w_refs, o_refs = refs[:n_w], refs[n_w:]
    x = x_ref[...]
    inv = lax.rsqrt(jnp.mean(x * x, axis=-1, keepdims=True) + NORM_EPS)
    xn = {}
    for (g, _, _) in plan:
        if g not in xn:
            xn[g] = (x * inv * gains_ref[g:g + 1, :]).astype(BF16)
    for (g, scale, _), w_ref, o_ref in zip(plan, w_refs, o_refs):
        n = w_ref.shape[1]
        for n0 in range(0, n, n_chunk):
            n1 = min(n, n0 + n_chunk)
            acc = jnp.dot(xn[g], w_ref[:, n0:n1], preferred_element_type=F32)
            if scale != 1.0:
                acc = acc * scale
            o_ref[:, n0:n1] = acc.astype(o_ref.dtype)


def _norm_matmul(x, gains, ws, plan):
    t, d = x.shape
    tm = TOKEN_TILE
    in_specs = [pl.BlockSpec((tm, d), lambda i: (i, 0)),
                pl.BlockSpec(gains.shape, lambda i: (0, 0))]
    in_specs += [pl.BlockSpec(w.shape, lambda i: (0, 0)) for w in ws]
    out_specs = [pl.BlockSpec((tm, w.shape[1]), lambda i: (i, 0)) for w in ws]
    out_shape = [jax.ShapeDtypeStruct((t, w.shape[1]), p[2]) for w, p in zip(ws, plan)]
    return pl.pallas_call(
        functools.partial(_norm_matmul_kernel, tuple(plan), 512),
        grid=(t // tm,), in_specs=in_specs, out_specs=out_specs, out_shape=out_shape,
        compiler_params=_params("parallel"), name="norm_matmul",
    )(x, gains, *ws)


def _gdn_kernel(n_heads, ts, qkv_ref, z_ref, ab_ref, conv_ref, alog_ref, dtb_ref, onorm_ref,
                o_ref, xpad_ref, state_ref, g_ref, beta_ref):
    c = GDN_CHUNK
    dk = GDN_HEAD_DIM
    tb = pl.program_id(1)
    n_ch = ts // c
    qkv_dim = qkv_ref.shape[-1]

    @pl.when(tb == 0)
    def _():
        state_ref[...] = jnp.zeros_like(state_ref)
        xpad_ref[0:8, :] = jnp.zeros((8, qkv_dim), F32)

    xpad_ref[8:8 + ts, :] = qkv_ref[0]

    ab = ab_ref[0]
    g_all = -jnp.exp(alog_ref[...]) * _softplus(ab + dtb_ref[...])
    beta_all = _sigmoid(ab)
    rin = lax.broadcasted_iota(jnp.int32, (ts, LANES), 0) & (c - 1)
    for h in range(n_heads):
        gh = jnp.broadcast_to(g_all[:, h:h + 1], (ts, LANES))
        shift = 1
        while shift < c:
            gh = gh + jnp.where(rin >= shift, pltpu.roll(gh, shift, 0), 0.0)
            shift *= 2
        g_ref[h] = gh
        beta_ref[h] = jnp.broadcast_to(beta_all[:, n_heads + h:n_heads + h + 1], (ts, LANES))

    row = lax.broadcasted_iota(jnp.int32, (c, c), 0)
    col = lax.broadcasted_iota(jnp.int32, (c, c), 1)
    incl = row >= col
    strict = row > col

    def conv_silu(r0, c0):
        win = xpad_ref[pl.ds(r0, c + 8), c0:c0 + LANES]
        acc = jnp.zeros((c, LANES), F32)
        for j in range(GDN_CONV):
            off = 8 - (GDN_CONV - 1) + j
            acc = acc + win[off:off + c, :] * conv_ref[j:j + 1, c0:c0 + LANES]
        return _silu(acc)

    def l2n(t):
        return t * lax.rsqrt(jnp.sum(t * t, axis=-1, keepdims=True) + NORM_EPS)

    heads = range(n_heads)
    eye = (row == col).astype(F32)
    same_blk = (row >> 4) == (col >> 4)

    def chunk_body(n, carry):
        r0 = pl.multiple_of(n * c, c)
        q = [l2n(conv_silu(r0, h * dk)) * (dk ** -0.5) for h in heads]
        k = [l2n(conv_silu(r0, (n_heads + h) * dk)) for h in heads]
        v = [conv_silu(r0, (2 * n_heads + h) * dk) for h in heads]
        gc = [g_ref[h, pl.ds(r0, c), :] for h in heads]
        beta = [beta_ref[h, pl.ds(r0, c), :] for h in heads]
        kb = [k[h] * beta[h] for h in heads]
        kk = [_dot_nt(kb[h], k[h]) for h in heads]
        qk = [_dot_nt(q[h], k[h]) for h in heads]
        decay = []
        for h in heads:
            diff = gc[h][:, 0:c] - gc[h].T[0:c, :]
            decay.append(jnp.where(incl, jnp.exp(jnp.where(incl, diff, 0.0)), 0.0))
        lm = [jnp.where(strict, kk[h] * decay[h], 0.0) for h in heads]
        attn = [jnp.where(incl, qk[h] * decay[h], 0.0) for h in heads]
        ld = [jnp.where(same_blk, lm[h], 0.0) for h in heads]
        lo = [lm[h] - ld[h] for h in heads]
        p = [eye - ld[h] for h in heads]
        sq = [_dot(ld[h], ld[h]) for h in heads]
        for _ in range(2):
            pn = [_dot(p[h], sq[h]) for h in heads]
            sq2 = [_dot(sq[h], sq[h]) for h in heads]
            p = [p[h] + pn[h] for h in heads]
            sq = sq2
        dinv = [p[h] + _dot(p[h], sq[h]) for h in heads]
        m = [_dot(dinv[h], lo[h]) for h in heads]
        m2 = [_dot(m[h], m[h]) for h in heads]
        r = [(eye - m[h]) + _dot(eye - m[h], m2[h]) for h in heads]
        tmat = [_dot(r[h], dinv[h]) for h in heads]
        eg = [jnp.exp(gc[h]) for h in heads]
        rhs = [jnp.concatenate([v[h] * beta[h], kb[h] * eg[h]], axis=1) for h in heads]
        uw = [_dot(tmat[h], rhs[h]) for h in heads]
        s = [state_ref[h] for h in heads]
        lhs = [jnp.concatenate([uw[h][:, dk:], q[h] * eg[h]], axis=0) for h in heads]
        ws_qs = [_dot(lhs[h], s[h]) for h in heads]
        v_new = [uw[h][:, :dk] - ws_qs[h][:c] for h in heads]
        g_last = [gc[h][c - 1:c, :] for h in heads]
        k_dec = [k[h] * jnp.exp(g_last[h] - gc[h]) for h in heads]
        av = [_dot(attn[h], v_new[h]) for h in heads]
        kv = [_dot_tn(k_dec[h], v_new[h]) for h in heads]
        for h in heads:
            state_ref[h] = s[h] * jnp.exp(g_last[h]) + kv[h]
            o = ws_qs[h][c:] + av[h]
            o = o * lax.rsqrt(jnp.mean(o * o, axis=-1, keepdims=True) + NORM_EPS) * onorm_ref[...]
            zz = z_ref[0, pl.ds(r0, c), h * dk:(h + 1) * dk]
            o_ref[0, pl.ds(r0, c), h * dk:(h + 1) * dk] = (o * _silu(zz)).astype(o_ref.dtype)
        return carry

    lax.fori_loop(0, n_ch, chunk_body, 0)
    xpad_ref[0:8, :] = xpad_ref[ts:ts + 8, :]


def _gated_deltanet_core(qkv, z, ab, conv_w, alog, dtb, onorm):
    b, s, qkv_dim = qkv.shape
    n_heads = GDN_HEADS
    ts = min(GDN_TIME_BLOCK, s)
    vd = z.shape[-1]
    kern = functools.partial(_gdn_kernel, n_heads, ts)
    return pl.pallas_call(
        kern,
        grid=(b, s // ts),
        in_specs=[
            pl.BlockSpec((1, ts, qkv_dim), lambda i, j: (i, j, 0)),
            pl.BlockSpec((1, ts, vd), lambda i, j: (i, j, 0)),
            pl.BlockSpec((1, ts, LANES), lambda i, j: (i, j, 0)),
            pl.BlockSpec(conv_w.shape, lambda i, j: (0, 0)),
            pl.BlockSpec((1, LANES), lambda i, j: (0, 0)),
            pl.BlockSpec((1, LANES), lambda i, j: (0, 0)),
            pl.BlockSpec((1, GDN_HEAD_DIM), lambda i, j: (0, 0)),
        ],
        out_specs=pl.BlockSpec((1, ts, vd), lambda i, j: (i, j, 0)),
        out_shape=jax.ShapeDtypeStruct((b, s, vd), BF16),
        scratch_shapes=[
            pltpu.VMEM((ts + 8, qkv_dim), F32),
            pltpu.VMEM((n_heads, GDN_HEAD_DIM, GDN_HEAD_DIM), F32),
            pltpu.VMEM((n_heads, ts, LANES), F32),
            pltpu.VMEM((n_heads, ts, LANES), F32),
        ],
        compiler_params=_params("parallel", "arbitrary"), name="gdn_delta_rule",
    )(qkv, z, ab, conv_w, alog, dtb, onorm)


def _sb_attn_kernel(q_ref, kt_ref, v_ref, o_ref, acc_ref, cs_ref, z_ref, p_ref):
    blk = SB_BLOCK
    dh = SB_HEAD_DIM
    grp = SB_GROUP
    i = pl.program_id(2)
    rows = grp * blk
    n_pairs = 1 + i // 2
    qb = q_ref[0]
    q4 = jnp.concatenate([qb[:, g * dh:(g + 1) * dh] for g in range(grp)], axis=0)
    krow = lax.broadcasted_iota(jnp.int32, (blk, blk), 0)
    kcol = lax.broadcasted_iota(jnp.int32, (blk, blk), 1)
    neg_suffix = jnp.where(krow >= kcol, -1.0, 0.0).astype(BF16)

    def pair_blocks(k):
        ja = i - 2 * k
        return jnp.maximum(ja, 0), jnp.maximum(ja - 1, 0), (ja >= 1).astype(F32)

    def logits(k):
        ja, jb, _ = pair_blocks(k)
        z_ref[0] = jnp.dot(q4, kt_ref[0, 0, ja], preferred_element_type=F32)
        z_ref[1] = jnp.dot(q4, kt_ref[0, 0, jb], preferred_element_type=F32)

    def weights(mask_a):
        masks = [mask_a, None]
        zs = [z_ref[0], z_ref[1]]
        sps = []
        for z, mask in zip(zs, masks):
            sp = jnp.maximum(z, 0.0) + jnp.log(1.0 + jnp.exp2(jnp.abs(z) * -LOG2E))
            if mask is not None:
                sp = jnp.where(mask, sp, 0.0)
            sps.append(sp)
        sufs = [jnp.dot(sp.astype(BF16), neg_suffix, preferred_element_type=F32) for sp in sps]
        rss = [jnp.sum(sp, axis=-1, keepdims=True) for sp in sps]
        cs = cs_ref[...]
        for n, (z, suf, rs, mask) in enumerate(zip(zs, sufs, rss, masks)):
            p = jnp.exp2(((z + suf) - cs) * LOG2E)
            if mask is not None:
                p = jnp.where(mask, p, 0.0)
            p_ref[n] = p.astype(BF16)
            cs = cs + rs
        cs_ref[...] = cs

    def weighted_values(k):
        ja, jb, valid_b = pair_blocks(k)
        vb = (v_ref[0, 0, jb].astype(F32) * valid_b).astype(BF16)
        acc_ref[...] += (jnp.dot(p_ref[0], v_ref[0, 0, ja], preferred_element_type=F32)
                         + jnp.dot(p_ref[1], vb, preferred_element_type=F32))

    acc_ref[...] = jnp.zeros_like(acc_ref)
    cs_ref[...] = jnp.zeros_like(cs_ref)
    t_in = lax.broadcasted_iota(jnp.int32, (rows, blk), 0) & (blk - 1)
    s_in = lax.broadcasted_iota(jnp.int32, (rows, blk), 1)
    logits(0)
    weights(s_in < t_in)
    logits(1)

    def body(k, c):
        weighted_values(k - 1)
        weights(None)
        logits(k + 1)
        return c

    lax.fori_loop(1, n_pairs, body, 0)
    weighted_values(n_pairs - 1)
    acc = acc_ref[...]
    for g in range(grp):
        o_ref[0, :, g * dh:(g + 1) * dh] = acc[g * blk:(g + 1) * blk, :].astype(o_ref.dtype)


def _stick_breaking_attention(q, kt, v):
    b, s, qd = q.shape
    nb = s // SB_BLOCK
    gw = SB_GROUP * SB_HEAD_DIM
    rows = SB_GROUP * SB_BLOCK
    return pl.pallas_call(
        _sb_attn_kernel,
        grid=(b, SB_KV_HEADS, nb),
        in_specs=[
            pl.BlockSpec((1, SB_BLOCK, gw), lambda bi, g, i: (bi, i, g)),
            pl.BlockSpec((1, 1, nb, SB_HEAD_DIM, SB_BLOCK), lambda bi, g, i: (bi, g, 0, 0, 0)),
            pl.BlockSpec((1, 1, nb, SB_BLOCK, SB_HEAD_DIM), lambda bi, g, i: (bi, g, 0, 0, 0)),
        ],
        out_specs=pl.BlockSpec((1, SB_BLOCK, gw), lambda bi, g, i: (bi, i, g)),
        out_shape=jax.ShapeDtypeStruct((b, s, qd), BF16),
        scratch_shapes=[pltpu.VMEM((rows, SB_HEAD_DIM), F32), pltpu.VMEM((rows, LANES), F32),
                        pltpu.VMEM((2, rows, SB_BLOCK), F32), pltpu.VMEM((2, rows, SB_BLOCK), BF16)],
        compiler_params=_params("parallel", "parallel", "arbitrary"), name="stick_breaking_attention",
    )(q, kt, v)


def _out_router_kernel(mix_ref, h_ref, wout_ref, gain_ref, wrh_ref, wrl_ref, br_ref,
                       h1_ref, xn_ref, rw_ref, ri_ref, cnt_ref, run_ref):
    h1 = h_ref[...] + jnp.dot(mix_ref[...], wout_ref[...], preferred_element_type=F32)
    h1_ref[...] = h1
    xn = _rms(h1, gain_ref[...])
    xh = xn.astype(BF16)
    xn_ref[...] = _pack_halves(xn)
    xl = (xn - xh.astype(F32)).astype(BF16)
    logits = (jnp.dot(xh, wrh_ref[...], preferred_element_type=F32)
              + jnp.dot(xl, wrh_ref[...], preferred_element_type=F32)
              + jnp.dot(xh, wrl_ref[...], preferred_element_type=F32)
              + br_ref[...])
    lane = lax.broadcasted_iota(jnp.int32, logits.shape, 1).astype(F32)
    neg = jnp.float32(-jnp.inf)
    big = jnp.float32(1e9)
    gl = jnp.where(lane < MOE_GROUPS, logits, neg)
    gmax = jnp.max(gl, axis=-1, keepdims=True)
    gidx = jnp.min(jnp.where(gl == gmax, lane, big), axis=-1, keepdims=True)
    gp = 1.0 / jnp.sum(jnp.exp(gl - gmax), axis=-1, keepdims=True)
    lo = MOE_GROUPS + gidx * MOE_EXPERTS_PER_GROUP
    el = jnp.where((lane >= lo) & (lane < lo + MOE_EXPERTS_PER_GROUP), logits, neg)
    m1 = jnp.max(el, axis=-1, keepdims=True)
    i1 = jnp.min(jnp.where(el == m1, lane, big), axis=-1, keepdims=True)
    el2 = jnp.where(lane == i1, neg, el)
    m2 = jnp.max(el2, axis=-1, keepdims=True)
    i2 = jnp.min(jnp.where(el2 == m2, lane, big), axis=-1, keepdims=True)
    e2 = jnp.exp(m2 - m1)
    w1 = gp / (1.0 + e2)
    w2 = gp * e2 / (1.0 + e2)
    rw_ref[...] = jnp.where(lane == 0, w1, jnp.where(lane == 1, w2, 0.0))

    @pl.when(pl.program_id(0) == 0)
    def _():
        run_ref[...] = jnp.zeros_like(run_ref)

    tm = logits.shape[0]
    trow = lax.broadcasted_iota(jnp.int32, (tm, tm), 0)
    tcol = lax.broadcasted_iota(jnp.int32, (tm, tm), 1)
    before = jnp.where(tcol < trow, 1.0, 0.0).astype(BF16)
    hot1 = lane == i1
    hot2 = lane == i2
    oh1 = jnp.where(hot1, 1.0, 0.0)
    oh2 = jnp.where(hot2, 1.0, 0.0)
    prior1 = jnp.dot(before, oh1.astype(BF16), preferred_element_type=F32)
    prior2 = jnp.dot(before, oh2.astype(BF16), preferred_element_type=F32)
    cnt1 = jnp.sum(oh1, axis=0, keepdims=True)
    cnt2 = jnp.sum(oh2, axis=0, keepdims=True)
    run = run_ref[...]
    rank1 = jnp.sum(jnp.where(hot1, prior1 + run, 0.0), axis=-1, keepdims=True)
    rank2 = jnp.sum(jnp.where(hot2, prior2 + (run + cnt1), 0.0), axis=-1, keepdims=True)
    run = run + cnt1 + cnt2
    run_ref[...] = run
    cnt_ref[...] = run
    ri = jnp.where(lane == 0, i1 - MOE_GROUPS, jnp.where(lane == 1, i2 - MOE_GROUPS,
                   jnp.where(lane == 2, rank1, jnp.where(lane == 3, rank2, 0.0))))
    ri_ref[...] = ri.T[0:ROUTE_COLS, :].astype(jnp.int32)


def _out_router(mix, h, w_out, gain, w_router, b_router):
    t, d = h.shape
    tm = TOKEN_TILE
    kd = mix.shape[1]
    row = lambda i: (i, 0)
    fix = lambda i: (0, 0)
    wr_hi = w_router.astype(BF16)
    wr_lo = (w_router - wr_hi.astype(F32)).astype(BF16)
    return pl.pallas_call(
        _out_router_kernel,
        grid=(t // tm,),
        in_specs=[pl.BlockSpec((tm, kd), row), pl.BlockSpec((tm, d), row),
                  pl.BlockSpec(w_out.shape, fix), pl.BlockSpec((1, d), fix),
                  pl.BlockSpec(w_router.shape, fix), pl.BlockSpec(w_router.shape, fix),
                  pl.BlockSpec((1, LANES), fix)],
        out_specs=[pl.BlockSpec((tm, d), row), pl.BlockSpec((tm, d // 2), row),
                   pl.BlockSpec((tm, LANES), row), pl.BlockSpec((ROUTE_COLS, tm), lambda i: (0, i)),
                   pl.BlockSpec((1, LANES), fix)],
        out_shape=[jax.ShapeDtypeStruct((t, d), F32), jax.ShapeDtypeStruct((t, d // 2), jnp.uint32),
                   jax.ShapeDtypeStruct((t, LANES), F32), jax.ShapeDtypeStruct((ROUTE_COLS, t), jnp.int32),
                   jax.ShapeDtypeStruct((1, LANES), F32)],
        scratch_shapes=[pltpu.VMEM((1, LANES), F32)],
        compiler_params=_params("arbitrary"), name="out_proj_router",
    )(mix, h, w_out, gain, wr_hi, wr_lo, b_router)


def _expert_ffn_kernel(te_ref, nu_ref, xs_ref, wg_ref, wu_ref, wd_ref, ys_ref):
    i = pl.program_id(0)

    @pl.when(i < nu_ref[0])
    def _():
        xa, xb = _unpack_halves(xs_ref[...])
        xa = xa.astype(BF16)
        xb = xb.astype(BF16)
        half = xa.shape[1]
        wg = wg_ref[0].astype(BF16)
        wu = wu_ref[0].astype(BF16)
        hg = (jnp.dot(xa, wg[:half], preferred_element_type=F32)
              + jnp.dot(xb, wg[half:], preferred_element_type=F32))
        hu = (jnp.dot(xa, wu[:half], preferred_element_type=F32)
              + jnp.dot(xb, wu[half:], preferred_element_type=F32))
        hh = _silu(hg) * hu
        ys_ref[...] = _pack_halves(jnp.dot(hh.astype(BF16), wd_ref[0].astype(BF16),
                                           preferred_element_type=F32))

    @pl.when(i >= nu_ref[0])
    def _():
        ys_ref[...] = jnp.zeros_like(ys_ref)


def _expert_ffn(tile_expert, n_used, xs, w_gate, w_up, w_down):
    r, dw = xs.shape
    d = 2 * dw
    tm = FFN_TILE
    f = w_gate.shape[-1]
    grid_spec = pltpu.PrefetchScalarGridSpec(
        num_scalar_prefetch=2,
        grid=(r // tm,),
        in_specs=[
            pl.BlockSpec((tm, dw), lambda i, te, nu: (i, 0)),
            pl.BlockSpec((1, d, f), lambda i, te, nu: (te[i], 0, 0)),
            pl.BlockSpec((1, d, f), lambda i, te, nu: (te[i], 0, 0)),
            pl.BlockSpec((1, f, d), lambda i, te, nu: (te[i], 0, 0)),
        ],
        out_specs=pl.BlockSpec((tm, dw), lambda i, te, nu: (i, 0)),
    )
    return pl.pallas_call(
        _expert_ffn_kernel, grid_spec=grid_spec,
        out_shape=jax.ShapeDtypeStruct((r, dw), jnp.uint32),
        compiler_params=_params("arbitrary"), name="expert_ffn",
    )(tile_expert, n_used, xs, w_gate, w_up, w_down)


def _ple_kernel(final, h_ref, m0_ref, m1_ref, rw_ref, p_ref, gain_ref, wg_ref, wp_ref, fgain_ref, o_ref):
    rw = rw_ref[...]
    a0, b0 = _unpack_halves(m0_ref[...])
    a1, b1 = _unpack_halves(m1_ref[...])
    w0 = rw[:, 0:1]
    w1 = rw[:, 1:2]
    moe = jnp.concatenate([w0 * a0 + w1 * a1, w0 * b0 + w1 * b1], axis=1)
    h2 = h_ref[...] + moe
    xn = _rms(h2, gain_ref[...]).astype(BF16)
    gate = _sigmoid(jnp.dot(xn, wg_ref[...], preferred_element_type=F32))
    emb = jnp.dot(p_ref[...].astype(BF16), wp_ref[...], preferred_element_type=F32)
    h3 = h2 + gate * emb
    if final:
        h3 = _rms(h3, fgain_ref[...])
    o_ref[...] = h3


def _ple(h, m01, rw, p, layer, gain, w_gate, w_proj, final_gain, final):
    t, d = h.shape
    tm = TOKEN_TILE
    pd = p.shape[1]
    nt = t // tm
    row = lambda i: (i, 0)
    fix = lambda i: (0, 0)
    return pl.pallas_call(
        functools.partial(_ple_kernel, final),
        grid=(nt,),
        in_specs=[pl.BlockSpec((tm, d), row), pl.BlockSpec((tm, d // 2), row),
                  pl.BlockSpec((tm, d // 2), lambda i: (i + nt, 0)),
                  pl.BlockSpec((tm, LANES), row), pl.BlockSpec((tm, pd), lambda i: (i + layer * nt, 0)),
                  pl.BlockSpec((1, d), fix),
                  pl.BlockSpec(w_gate.shape, fix), pl.BlockSpec(w_proj.shape, fix),
                  pl.BlockSpec((1, d), fix)],
        out_specs=pl.BlockSpec((tm, d), row),
        out_shape=jax.ShapeDtypeStruct((t, d), F32),
        compiler_params=_params("parallel"), name="moe_residual_ple",
    )(h, m01, m01, rw, p, gain, w_gate, w_proj, final_gain)


def _routing_tables(ri, counts):
    t = ri.shape[1]
    tm = FFN_TILE
    n_rows = 2 * t + MOE_N_EXPERTS * tm
    padded = ((counts + tm - 1) // tm) * tm
    ends = jnp.cumsum(padded)
    starts = ends - padded
    ids = ri[0:2]
    offs = jnp.zeros_like(ids)
    for e in range(MOE_N_EXPERTS):
        offs = jnp.where(ids == e, starts[e], offs)
    dest = (ri[2:4] + offs).reshape(-1)
    src_tok = _sc_invert_slots(dest, n_rows, t)
    tile_start = jnp.arange(n_rows // tm, dtype=jnp.int32) * tm
    tile_expert = jnp.minimum(jnp.sum((ends[None, :] <= tile_start[:, None]).astype(jnp.int32), axis=1),
                              MOE_N_EXPERTS - 1)
    n_used = (ends[-1] // tm).astype(jnp.int32).reshape(1)
    return src_tok, tile_expert, n_used, dest


def _moe_ple(layer, mix, h, w_out, moe_gain, w_router, b_router, w_gate, w_up, w_down,
             p, ple_gain, ple_w_gate, ple_w_proj, final_gain, final):
    h1, xn, rw, ri, cnt = _out_router(mix, h, w_out, moe_gain, w_router, b_router)
    counts = cnt[0, MOE_GROUPS:MOE_GROUPS + MOE_N_EXPERTS].astype(jnp.int32)
    src_tok, tile_expert, n_used, dest = _routing_tables(ri, counts)
    xs = _sc_gather_rows(xn, src_tok)
    ys = _expert_ffn(tile_expert + layer * MOE_N_EXPERTS, n_used, xs, w_gate, w_up, w_down)
    m01 = _sc_gather_rows(ys, dest)
    return _ple(h1, m01, rw, p, layer, ple_gain, ple_w_gate, ple_w_proj, final_gain, final)


def _pad_lanes(v):
    return jnp.zeros((1, LANES), F32).at[0, :v.shape[0]].set(v.astype(F32))


def kernel(x, p, attn_norm, moe_norm, ple_norm, gdn_w_in, gdn_conv, gdn_a_log, gdn_dt_bias, gdn_o_norm, gdn_w_out, kv_norm, w_kv, sb_w_q, sb_w_out, moe_w_group, moe_b_group, moe_w_expert, moe_b_expert, moe_w_gate, moe_w_up, moe_w_down, ple_w_gate, ple_w_proj, final_norm):
    d = x.shape[-1]
    kd = GDN_HEADS * GDN_HEAD_DIM
    qkv_dim = 3 * kd

    def router_params(i):
        wr = jnp.zeros((d, LANES), F32)
        wr = wr.at[:, :MOE_GROUPS].set(moe_w_group[i])
        wr = wr.at[:, MOE_GROUPS:MOE_GROUPS + MOE_N_EXPERTS].set(moe_w_expert[i])
        br = jnp.zeros((1, LANES), F32)
        br = br.at[0, :MOE_GROUPS].set(moe_b_group[i])
        br = br.at[0, MOE_GROUPS:MOE_GROUPS + MOE_N_EXPERTS].set(moe_b_expert[i])
        return wr, br

    experts = (moe_w_gate.reshape(-1, d, MOE_D_EXPERT), moe_w_up.reshape(-1, d, MOE_D_EXPERT),
               moe_w_down.reshape(-1, MOE_D_EXPERT, d))

    w_in = gdn_w_in[0]
    w_ab = jnp.zeros((d, LANES), F32).at[:, :2 * GDN_HEADS].set(w_in[:, qkv_dim + kd:])
    w_in_parts = [w_in[:, :qkv_dim].astype(BF16), w_in[:, qkv_dim:qkv_dim + kd].astype(BF16), w_ab.astype(BF16)]
    dtb = jnp.zeros((1, LANES), F32).at[0, :GDN_HEADS].set(gdn_dt_bias[0])
    alog = _pad_lanes(gdn_a_log[0])
    routers = [router_params(0), router_params(1)]
    mix_w_out = [gdn_w_out[0].astype(BF16), sb_w_out[0].astype(BF16)]
    ple_wg = [ple_w_gate[0].astype(BF16), ple_w_gate[1].astype(BF16)]
    ple_wp = [ple_w_proj[0].astype(BF16), ple_w_proj[1].astype(BF16)]
    gains1 = jnp.stack([attn_norm[1], kv_norm], axis=0)
    w_q = sb_w_q[0].astype(BF16)
    w_kv_b = w_kv.astype(BF16)
    fgain = final_norm.reshape(1, d)

    b, s, _ = x.shape
    t = b * s
    h = x.reshape(t, d)
    p_all = p.reshape(-1, p.shape[-1])
    qkv, z, ab = _norm_matmul(h, attn_norm[0:1], w_in_parts,
                              [(0, 1.0, F32), (0, 1.0, F32), (0, 1.0, F32)])
    og = _gated_deltanet_core(
        qkv.reshape(b, s, qkv_dim), z.reshape(b, s, kd), ab.reshape(b, s, LANES),
        gdn_conv[0], alog, dtb, gdn_o_norm[0].reshape(1, GDN_HEAD_DIM))
    h = _moe_ple(0, og.reshape(t, kd), h, mix_w_out[0], moe_norm[0:1], *routers[0], *experts,
                 p_all, ple_norm[0:1], ple_wg[0], ple_wp[0], fgain, False)
    q, kv = _norm_matmul(h, gains1, [w_q, w_kv_b],
                         [(0, SB_HEAD_DIM ** -0.5, BF16), (1, 1.0, BF16)])
    nb = s // SB_BLOCK
    kvw = SB_KV_HEADS * SB_HEAD_DIM
    k5 = kv[:, :kvw].reshape(b, nb, SB_BLOCK, SB_KV_HEADS, SB_HEAD_DIM)
    v5 = kv[:, kvw:].reshape(b, nb, SB_BLOCK, SB_KV_HEADS, SB_HEAD_DIM)
    kt = k5.transpose(0, 3, 1, 4, 2)
    vv = v5.transpose(0, 3, 1, 2, 4)
    oa = _stick_breaking_attention(q.reshape(b, s, -1), kt, vv)
    out = _moe_ple(1, oa.reshape(t, -1), h, mix_w_out[1], moe_norm[1:2], *routers[1], *experts,
                   p_all, ple_norm[1:2], ple_wg[1], ple_wp[1], fgain, True)
    return out.reshape(b, s, d)
```

```python
import functools

import jax
import jax.numpy as jnp
from jax import lax
from jax.experimental import pallas as pl
from jax.experimental.pallas import tpu as pltpu
from jax.experimental.pallas import tpu_sc as plsc

NORM_EPS = 1e-6
LOG2E = 1.4426950408889634
LANES = 128
GDN_HEADS = 8
GDN_HEAD_DIM = 128
GDN_CONV = 4
GDN_CHUNK = 64
SB_Q_HEADS = 16
SB_KV_HEADS = 4
SB_GROUP = SB_Q_HEADS // SB_KV_HEADS
SB_HEAD_DIM = 64
SB_BLOCK = 128
MOE_GROUPS = 4
MOE_EXPERTS_PER_GROUP = 8
MOE_N_EXPERTS = MOE_GROUPS * MOE_EXPERTS_PER_GROUP
MOE_D_EXPERT = 256

VMEM_LIMIT = 56 * 1024 * 1024
TOKEN_TILE = 256
FFN_TILE = 256
GDN_TIME_BLOCK = 512
ROUTE_COLS = 8
SC_GATHER_ROWS = 64
SC_LANES = 16
SC_INDEX_CHUNK = 4096

F32 = jnp.float32
BF16 = jnp.bfloat16


def _params(*sem):
    return pltpu.CompilerParams(dimension_semantics=sem, vmem_limit_bytes=VMEM_LIMIT)


def _dot(a, b):
    return jnp.dot(a.astype(BF16), b.astype(BF16), preferred_element_type=F32)


def _dot_nt(a, b):
    return lax.dot_general(a.astype(BF16), b.astype(BF16), (((1,), (1,)), ((), ())),
                           preferred_element_type=F32)


def _dot_tn(a, b):
    return lax.dot_general(a.astype(BF16), b.astype(BF16), (((0,), (0,)), ((), ())),
                           preferred_element_type=F32)


def _dot_f32(a, b):
    return jnp.dot(a, b, precision=lax.Precision.HIGHEST, preferred_element_type=F32)


def _rms(x, gain):
    return x * lax.rsqrt(jnp.mean(x * x, axis=-1, keepdims=True) + NORM_EPS) * gain


def _silu(x):
    return x * (1.0 / (1.0 + jnp.exp(-x)))


def _sigmoid(x):
    return 1.0 / (1.0 + jnp.exp(-x))


def _softplus(x):
    return jnp.maximum(x, 0.0) + jnp.log(1.0 + jnp.exp(-jnp.abs(x)))


def _pack_halves(x):
    w = x.shape[1] // 2
    hi = pltpu.bitcast(x[:, :w].astype(BF16).astype(F32), jnp.uint32)
    lo = pltpu.bitcast(x[:, w:].astype(BF16).astype(F32), jnp.uint32)
    return (hi & jnp.uint32(0xFFFF0000)) | (lo >> 16)


def _unpack_halves(u):
    hi = pltpu.bitcast(u & jnp.uint32(0xFFFF0000), F32)
    lo = pltpu.bitcast(u << 16, F32)
    return hi, lo


def _sc_gather_rows(table, idx):
    _, w = table.shape
    m = idx.shape[0]
    info = plsc.get_sparse_core_info()
    nc, ns = info.num_cores, info.num_subcores
    rows = SC_GATHER_ROWS
    assert m % (nc * ns * rows * 2) == 0
    per_w = m // (nc * ns)
    n_pairs = per_w // (2 * rows)
    mesh = plsc.VectorSubcoreMesh(core_axis_name="c", subcore_axis_name="s")

    @functools.partial(
        pl.kernel, mesh=mesh,
        out_type=jax.ShapeDtypeStruct((m, w), table.dtype),
        scratch_types=[pltpu.VMEM((2, rows), jnp.int32),
                       pltpu.VMEM((2, rows, w), table.dtype),
                       pltpu.SemaphoreType.DMA((2,)),
                       pltpu.SemaphoreType.DMA((2,))],
    )
    def gather(table_hbm, idx_hbm, out_hbm, idx_v, rows_v, gsem, wsem):
        wid = lax.axis_index("s") * nc + lax.axis_index("c")
        base = wid * per_w

        def fetch(chunk, slot):
            off = pl.multiple_of(base + chunk * rows, rows)
            pltpu.sync_copy(idx_hbm.at[pl.ds(off, rows)], idx_v.at[slot])
            return pltpu.async_copy(table_hbm.at[idx_v.at[slot]], rows_v.at[slot], gsem.at[slot])

        def write(chunk, slot):
            off = pl.multiple_of(base + chunk * rows, rows)
            return pltpu.async_copy(rows_v.at[slot], out_hbm.at[pl.ds(off, rows)], wsem.at[slot])

        @pl.loop(0, n_pairs)
        def _(pi):
            g0 = fetch(2 * pi, 0)
            g1 = fetch(2 * pi + 1, 1)
            g0.wait()
            w0 = write(2 * pi, 0)
            g1.wait()
            w1 = write(2 * pi + 1, 1)
            w0.wait()
            w1.wait()

    return gather(table, idx)


def _sc_invert_slots(dest, n_rows, n_tokens):
    n_pairs = dest.shape[0]
    ch = SC_INDEX_CHUNK
    assert n_pairs % ch == 0 and n_rows % SC_LANES == 0
    nc = plsc.get_sparse_core_info().num_cores
    mesh = plsc.VectorSubcoreMesh(core_axis_name="c", subcore_axis_name="s")

    @functools.partial(
        pl.kernel, mesh=mesh,
        out_type=jax.ShapeDtypeStruct((n_rows,), jnp.int32),
        scratch_types=[pltpu.VMEM((n_rows,), jnp.int32), pltpu.VMEM((ch,), jnp.int32)],
        compiler_params=pltpu.CompilerParams(needs_layout_passes=False),
    )
    def invert(dest_hbm, out_hbm, tab_v, dest_v):
        wid = lax.axis_index("s") * nc + lax.axis_index("c")

        @pl.when(wid == 0)
        def _():
            lanes = lax.iota(jnp.int32, SC_LANES)

            @pl.loop(0, n_rows // SC_LANES)
            def _(i):
                tab_v[pl.ds(pl.multiple_of(i * SC_LANES, SC_LANES), SC_LANES)] = lax.rem(
                    i * SC_LANES + lanes, n_tokens)

            @pl.loop(0, n_pairs // ch)
            def _(c):
                pltpu.sync_copy(dest_hbm.at[pl.ds(pl.multiple_of(c * ch, ch), ch)], dest_v)

                @pl.loop(0, ch // SC_LANES)
                def _(j):
                    d = dest_v[pl.ds(pl.multiple_of(j * SC_LANES, SC_LANES), SC_LANES)]
                    pair = c * ch + j * SC_LANES + lanes
                    plsc.store_scatter(tab_v, [d], lax.rem(pair, n_tokens))

            pltpu.sync_copy(tab_v, out_hbm)

    return invert(dest)


def _norm_matmul_kernel(plan, n_chunk, x_ref, gains_ref, *refs):
    n_w = len(plan)
    w_refs, o_refs = refs[:n_w], refs[n_w:]
    x = x_ref[...]
    inv = lax.rsqrt(jnp.mean(x * x, axis=-1, keepdims=True) + NORM_EPS)
    xn = {}
    for (g, _, _) in plan:
        if g not in xn:
            xn[g] = (x * inv * gains_ref[g:g + 1, :]).astype(BF16)
    for (g, scale, _), w_ref, o_ref in zip(plan, w_refs, o_refs):
        n = w_ref.shape[1]
        for n0 in range(0, n, n_chunk):
            n1 = min(n, n0 + n_chunk)
            acc = jnp.dot(xn[g], w_ref[:, n0:n1], preferred_element_type=F32)
            if scale != 1.0:
                acc = acc * scale
            o_ref[:, n0:n1] = acc.astype(o_ref.dtype)


def _norm_matmul(x, gains, ws, plan):
    t, d = x.shape
    tm = TOKEN_TILE
    in_specs = [pl.BlockSpec((tm, d), lambda i: (i, 0)),
                pl.BlockSpec(gains.shape, lambda i: (0, 0))]
    in_specs += [pl.BlockSpec(w.shape, lambda i: (0, 0)) for w in ws]
    out_specs = [pl.BlockSpec((tm, w.shape[1]), lambda i: (i, 0)) for w in ws]
    out_shape = [jax.ShapeDtypeStruct((t, w.shape[1]), p[2]) for w, p in zip(ws, plan)]
    return pl.pallas_call(
        functools.partial(_norm_matmul_kernel, tuple(plan), 512),
        grid=(t // tm,), in_specs=in_specs, out_specs=out_specs, out_shape=out_shape,
        compiler_params=_params("parallel"), name="norm_matmul",
    )(x, gains, *ws)


def _gdn_kernel(n_heads, ts, qkv_ref, z_ref, ab_ref, conv_ref, alog_ref, dtb_ref, onorm_ref,
                o_ref, xpad_ref, state_ref, g_ref, beta_ref):
    c = GDN_CHUNK
    dk = GDN_HEAD_DIM
    tb = pl.program_id(1)
    n_ch = ts // c
    qkv_dim = qkv_ref.shape[-1]

    @pl.when(tb == 0)
    def _():
        state_ref[...] = jnp.zeros_like(state_ref)
        xpad_ref[0:8, :] = jnp.zeros((8, qkv_dim), F32)

    xpad_ref[8:8 + ts, :] = qkv_ref[0].astype(F32)

    ab = ab_ref[0]
    g_all = -jnp.exp(alog_ref[...]) * _softplus(ab + dtb_ref[...])
    beta_all = _sigmoid(ab)
    rin = lax.broadcasted_iota(jnp.int32, (ts, LANES), 0) & (c - 1)
    for h in range(n_heads):
        gh = jnp.broadcast_to(g_all[:, h:h + 1], (ts, LANES))
        shift = 1
        while shift < c:
            gh = gh + jnp.where(rin >= shift, pltpu.roll(gh, shift, 0), 0.0)
            shift *= 2
        g_ref[h] = gh
        beta_ref[h] = jnp.broadcast_to(beta_all[:, n_heads + h:n_heads + h + 1], (ts, LANES))

    row = lax.broadcasted_iota(jnp.int32, (c, c), 0)
    col = lax.broadcasted_iota(jnp.int32, (c, c), 1)
    incl = row >= col
    strict = row > col

    def conv_silu(r0, c0):
        win = xpad_ref[pl.ds(r0, c + 8), c0:c0 + LANES]
        acc = jnp.zeros((c, LANES), F32)
        for j in range(GDN_CONV):
            off = 8 - (GDN_CONV - 1) + j
            acc = acc + win[off:off + c, :] * conv_ref[j:j + 1, c0:c0 + LANES]
        return _silu(acc)

    def l2n(t):
        return t * lax.rsqrt(jnp.sum(t * t, axis=-1, keepdims=True) + NORM_EPS)

    heads = range(n_heads)
    eye = (row == col).astype(F32)
    same_blk = (row >> 4) == (col >> 4)

    def chunk_body(n, carry):
        r0 = pl.multiple_of(n * c, c)
        q = [l2n(conv_silu(r0, h * dk)) * (dk ** -0.5) for h in heads]
        k = [l2n(conv_silu(r0, (n_heads + h) * dk)) for h in heads]
        v = [conv_silu(r0, (2 * n_heads + h) * dk) for h in heads]
        gc = [g_ref[h, pl.ds(r0, c), :] for h in heads]
        beta = [beta_ref[h, pl.ds(r0, c), :] for h in heads]
        kb = [k[h] * beta[h] for h in heads]
        kk = [_dot_nt(kb[h], k[h]) for h in heads]
        qk = [_dot_nt(q[h], k[h]) for h in heads]
        decay = []
        for h in heads:
            diff = gc[h][:, 0:c] - gc[h].T[0:c, :]
            decay.append(jnp.where(incl, jnp.exp(jnp.where(incl, diff, 0.0)), 0.0))
        lm = [jnp.where(strict, kk[h] * decay[h], 0.0) for h in heads]
        attn = [jnp.where(incl, qk[h] * decay[h], 0.0) for h in heads]
        ld = [jnp.where(same_blk, lm[h], 0.0) for h in heads]
        lo = [lm[h] - ld[h] for h in heads]
        p = [eye - ld[h] for h in heads]
        sq = [_dot(ld[h], ld[h]) for h in heads]
        for _ in range(2):
            pn = [_dot(p[h], sq[h]) for h in heads]
            sq2 = [_dot(sq[h], sq[h]) for h in heads]
            p = [p[h] + pn[h] for h in heads]
            sq = sq2
        dinv = [p[h] + _dot(p[h], sq[h]) for h in heads]
        m = [_dot(dinv[h], lo[h]) for h in heads]
        m2 = [_dot(m[h], m[h]) for h in heads]
        r = [(eye - m[h]) + _dot(eye - m[h], m2[h]) for h in heads]
        tmat = [_dot(r[h], dinv[h]) for h in heads]
        eg = [jnp.exp(gc[h]) for h in heads]
        rhs = [jnp.concatenate([v[h] * beta[h], kb[h] * eg[h]], axis=1) for h in heads]
        uw = [_dot(tmat[h], rhs[h]) for h in heads]
        s = [state_ref[h] for h in heads]
        lhs = [jnp.concatenate([uw[h][:, dk:], q[h] * eg[h]], axis=0) for h in heads]
        ws_qs = [_dot(lhs[h], s[h]) for h in heads]
        v_new = [uw[h][:, :dk] - ws_qs[h][:c] for h in heads]
        g_last = [gc[h][c - 1:c, :] for h in heads]
        k_dec = [k[h] * jnp.exp(g_last[h] - gc[h]) for h in heads]
        av = [_dot(attn[h], v_new[h]) for h in heads]
        kv = [_dot_tn(k_dec[h], v_new[h]) for h in heads]
        for h in heads:
            state_ref[h] = s[h] * jnp.exp(g_last[h]) + kv[h]
            o = ws_qs[h][c:] + av[h]
            o = o * lax.rsqrt(jnp.mean(o * o, axis=-1, keepdims=True) + NORM_EPS) * onorm_ref[...]
            zz = z_ref[0, pl.ds(r0, c), h * dk:(h + 1) * dk].astype(F32)
            o_ref[0, pl.ds(r0, c), h * dk:(h + 1) * dk] = (o * _silu(zz)).astype(o_ref.dtype)
        return carry

    lax.fori_loop(0, n_ch, chunk_body, 0)
    xpad_ref[0:8, :] = xpad_ref[ts:ts + 8, :]


def _gated_deltanet_core(qkv, z, ab, conv_w, alog, dtb, onorm):
    b, s, qkv_dim = qkv.shape
    n_heads = GDN_HEADS
    ts = min(GDN_TIME_BLOCK, s)
    vd = z.shape[-1]
    kern = functools.partial(_gdn_kernel, n_heads, ts)
    return pl.pallas_call(
        kern,
        grid=(b, s // ts),
        in_specs=[
            pl.BlockSpec((1, ts, qkv_dim), lambda i, j: (i, j, 0)),
            pl.BlockSpec((1, ts, vd), lambda i, j: (i, j, 0)),
            pl.BlockSpec((1, ts, LANES), lambda i, j: (i, j, 0)),
            pl.BlockSpec(conv_w.shape, lambda i, j: (0, 0)),
            pl.BlockSpec((1, LANES), lambda i, j: (0, 0)),
            pl.BlockSpec((1, LANES), lambda i, j: (0, 0)),
            pl.BlockSpec((1, GDN_HEAD_DIM), lambda i, j: (0, 0)),
        ],
        out_specs=pl.BlockSpec((1, ts, vd), lambda i, j: (i, j, 0)),
        out_shape=jax.ShapeDtypeStruct((b, s, vd), BF16),
        scratch_shapes=[
            pltpu.VMEM((ts + 8, qkv_dim), F32),
            pltpu.VMEM((n_heads, GDN_HEAD_DIM, GDN_HEAD_DIM), F32),
            pltpu.VMEM((n_heads, ts, LANES), F32),
            pltpu.VMEM((n_heads, ts, LANES), F32),
        ],
        compiler_params=_params("parallel", "arbitrary"), name="gdn_delta_rule",
    )(qkv, z, ab, conv_w, alog, dtb, onorm)


def _sb_attn_kernel(q_ref, kt_ref, v_ref, o_ref, acc_ref, cs_ref, z_ref, p_ref):
    blk = SB_BLOCK
    dh = SB_HEAD_DIM
    grp = SB_GROUP
    i = pl.program_id(2)
    rows = grp * blk
    n_pairs = 1 + i // 2
    qb = q_ref[0]
    q4 = jnp.concatenate([qb[:, g * dh:(g + 1) * dh] for g in range(grp)], axis=0)
    krow = lax.broadcasted_iota(jnp.int32, (blk, blk), 0)
    kcol = lax.broadcasted_iota(jnp.int32, (blk, blk), 1)
    neg_suffix = jnp.where(krow >= kcol, -1.0, 0.0).astype(BF16)

    def pair_blocks(k):
        ja = i - 2 * k
        return jnp.maximum(ja, 0), jnp.maximum(ja - 1, 0), (ja >= 1).astype(F32)

    def logits(k):
        ja, jb, _ = pair_blocks(k)
        z_ref[0] = jnp.dot(q4, kt_ref[0, 0, ja], preferred_element_type=F32)
        z_ref[1] = jnp.dot(q4, kt_ref[0, 0, jb], preferred_element_type=F32)

    def weights(mask_a):
        masks = [mask_a, None]
        zs = [z_ref[0], z_ref[1]]
        sps = []
        for z, mask in zip(zs, masks):
            sp = jnp.maximum(z, 0.0) + jnp.log(1.0 + jnp.exp2(jnp.abs(z) * -LOG2E))
            if mask is not None:
                sp = jnp.where(mask, sp, 0.0)
            sps.append(sp)
        sufs = [jnp.dot(sp.astype(BF16), neg_suffix, preferred_element_type=F32) for sp in sps]
        rss = [jnp.sum(sp, axis=-1, keepdims=True) for sp in sps]
        cs = cs_ref[...]
        for n, (z, suf, rs, mask) in enumerate(zip(zs, sufs, rss, masks)):
            p = jnp.exp2(((z + suf) - cs) * LOG2E)
            if mask is not None:
                p = jnp.where(mask, p, 0.0)
            p_ref[n] = p.astype(BF16)
            cs = cs + rs
        cs_ref[...] = cs

    def weighted_values(k):
        ja, jb, valid_b = pair_blocks(k)
        vb = (v_ref[0, 0, jb].astype(F32) * valid_b).astype(BF16)
        acc_ref[...] += (jnp.dot(p_ref[0], v_ref[0, 0, ja], preferred_element_type=F32)
                         + jnp.dot(p_ref[1], vb, preferred_element_type=F32))

    acc_ref[...] = jnp.zeros_like(acc_ref)
    cs_ref[...] = jnp.zeros_like(cs_ref)
    t_in = lax.broadcasted_iota(jnp.int32, (rows, blk), 0) & (blk - 1)
    s_in = lax.broadcasted_iota(jnp.int32, (rows, blk), 1)
    logits(0)
    weights(s_in < t_in)
    logits(1)

    def body(k, c):
        weighted_values(k - 1)
        weights(None)
        logits(k + 1)
        return c

    lax.fori_loop(1, n_pairs, body, 0)
    weighted_values(n_pairs - 1)
    acc = acc_ref[...]
    for g in range(grp):
        o_ref[0, :, g * dh:(g + 1) * dh] = acc[g * blk:(g + 1) * blk, :].astype(o_ref.dtype)


def _stick_breaking_attention(q, kt, v):
    b, s, qd = q.shape
    nb = s // SB_BLOCK
    gw = SB_GROUP * SB_HEAD_DIM
    rows = SB_GROUP * SB_BLOCK
    return pl.pallas_call(
        _sb_attn_kernel,
        grid=(b, SB_KV_HEADS, nb),
        in_specs=[
            pl.BlockSpec((1, SB_BLOCK, gw), lambda bi, g, i: (bi, i, g)),
            pl.BlockSpec((1, 1, nb, SB_HEAD_DIM, SB_BLOCK), lambda bi, g, i: (bi, g, 0, 0, 0)),
            pl.BlockSpec((1, 1, nb, SB_BLOCK, SB_HEAD_DIM), lambda bi, g, i: (bi, g, 0, 0, 0)),
        ],
        out_specs=pl.BlockSpec((1, SB_BLOCK, gw), lambda bi, g, i: (bi, i, g)),
        out_shape=jax.ShapeDtypeStruct((b, s, qd), BF16),
        scratch_shapes=[pltpu.VMEM((rows, SB_HEAD_DIM), F32), pltpu.VMEM((rows, LANES), F32),
                        pltpu.VMEM((2, rows, SB_BLOCK), F32), pltpu.VMEM((2, rows, SB_BLOCK), BF16)],
        compiler_params=_params("parallel", "parallel", "arbitrary"), name="stick_breaking_attention",
    )(q, kt, v)


def _out_router_kernel(mix_ref, h_ref, wout_ref, gain_ref, wrh_ref, wrl_ref, br_ref,
                       h1_ref, xn_ref, rw_ref, ri_ref, cnt_ref, run_ref):
    h1 = h_ref[...] + jnp.dot(mix_ref[...], wout_ref[...], preferred_element_type=F32)
    h1_ref[...] = h1
    xn = _rms(h1, gain_ref[...])
    xh = xn.astype(BF16)
    xn_ref[...] = _pack_halves(xn)
    xl = (xn - xh.astype(F32)).astype(BF16)
    logits = (jnp.dot(xh, wrh_ref[...], preferred_element_type=F32)
              + jnp.dot(xl, wrh_ref[...], preferred_element_type=F32)
              + jnp.dot(xh, wrl_ref[...], preferred_element_type=F32)
              + br_ref[...])
    lane = lax.broadcasted_iota(jnp.int32, logits.shape, 1).astype(F32)
    neg = jnp.float32(-jnp.inf)
    big = jnp.float32(1e9)
    gl = jnp.where(lane < MOE_GROUPS, logits, neg)
    gmax = jnp.max(gl, axis=-1, keepdims=True)
    gidx = jnp.min(jnp.where(gl == gmax, lane, big), axis=-1, keepdims=True)
    gp = 1.0 / jnp.sum(jnp.exp(gl - gmax), axis=-1, keepdims=True)
    lo = MOE_GROUPS + gidx * MOE_EXPERTS_PER_GROUP
    el = jnp.where((lane >= lo) & (lane < lo + MOE_EXPERTS_PER_GROUP), logits, neg)
    m1 = jnp.max(el, axis=-1, keepdims=True)
    i1 = jnp.min(jnp.where(el == m1, lane, big), axis=-1, keepdims=True)
    el2 = jnp.where(lane == i1, neg, el)
    m2 = jnp.max(el2, axis=-1, keepdims=True)
    i2 = jnp.min(jnp.where(el2 == m2, lane, big), axis=-1, keepdims=True)
    e2 = jnp.exp(m2 - m1)
    w1 = gp / (1.0 + e2)
    w2 = gp * e2 / (1.0 + e2)
    rw_ref[...] = jnp.where(lane == 0, w1, jnp.where(lane == 1, w2, 0.0))

    @pl.when(pl.program_id(0) == 0)
    def _():
        run_ref[...] = jnp.zeros_like(run_ref)

    tm = logits.shape[0]
    trow = lax.broadcasted_iota(jnp.int32, (tm, tm), 0)
    tcol = lax.broadcasted_iota(jnp.int32, (tm, tm), 1)
    before = jnp.where(tcol < trow, 1.0, 0.0).astype(BF16)
    hot1 = lane == i1
    hot2 = lane == i2
    oh1 = jnp.where(hot1, 1.0, 0.0)
    oh2 = jnp.where(hot2, 1.0, 0.0)
    prior1 = jnp.dot(before, oh1.astype(BF16), preferred_element_type=F32)
    prior2 = jnp.dot(before, oh2.astype(BF16), preferred_element_type=F32)
    cnt1 = jnp.sum(oh1, axis=0, keepdims=True)
    cnt2 = jnp.sum(oh2, axis=0, keepdims=True)
    run = run_ref[...]
    rank1 = jnp.sum(jnp.where(hot1, prior1 + run, 0.0), axis=-1, keepdims=True)
    rank2 = jnp.sum(jnp.where(hot2, prior2 + (run + cnt1), 0.0), axis=-1, keepdims=True)
    run = run + cnt1 + cnt2
    run_ref[...] = run
    cnt_ref[...] = run
    ri = jnp.where(lane == 0, i1 - MOE_GROUPS, jnp.where(lane == 1, i2 - MOE_GROUPS,
                   jnp.where(lane == 2, rank1, jnp.where(lane == 3, rank2, 0.0))))
    ri_ref[...] = ri.T[0:ROUTE_COLS, :].astype(jnp.int32)


def _out_router(mix, h, w_out, gain, w_router, b_router):
    t, d = h.shape
    tm = TOKEN_TILE
    kd = mix.shape[1]
    row = lambda i: (i, 0)
    fix = lambda i: (0, 0)
    wr_hi = w_router.astype(BF16)
    wr_lo = (w_router - wr_hi.astype(F32)).astype(BF16)
    return pl.pallas_call(
        _out_router_kernel,
        grid=(t // tm,),
        in_specs=[pl.BlockSpec((tm, kd), row), pl.BlockSpec((tm, d), row),
                  pl.BlockSpec(w_out.shape, fix), pl.BlockSpec((1, d), fix),
                  pl.BlockSpec(w_router.shape, fix), pl.BlockSpec(w_router.shape, fix),
                  pl.BlockSpec((1, LANES), fix)],
        out_specs=[pl.BlockSpec((tm, d), row), pl.BlockSpec((tm, d // 2), row),
                   pl.BlockSpec((tm, LANES), row), pl.BlockSpec((ROUTE_COLS, tm), lambda i: (0, i)),
                   pl.BlockSpec((1, LANES), fix)],
        out_shape=[jax.ShapeDtypeStruct((t, d), F32), jax.ShapeDtypeStruct((t, d // 2), jnp.uint32),
                   jax.ShapeDtypeStruct((t, LANES), F32), jax.ShapeDtypeStruct((ROUTE_COLS, t), jnp.int32),
                   jax.ShapeDtypeStruct((1, LANES), F32)],
        scratch_shapes=[pltpu.VMEM((1, LANES), F32)],
        compiler_params=_params("arbitrary"), name="out_proj_router",
    )(mix, h, w_out, gain, wr_hi, wr_lo, b_router)


def _expert_ffn_kernel(te_ref, nu_ref, xs_ref, wg_ref, wu_ref, wd_ref, ys_ref):
    i = pl.program_id(0)

    @pl.when(i < nu_ref[0])
    def _():
        xa, xb = _unpack_halves(xs_ref[...])
        xa = xa.astype(BF16)
        xb = xb.astype(BF16)
        half = xa.shape[1]
        wg = wg_ref[0].astype(BF16)
        wu = wu_ref[0].astype(BF16)
        hg = (jnp.dot(xa, wg[:half], preferred_element_type=F32)
              + jnp.dot(xb, wg[half:], preferred_element_type=F32))
        hu = (jnp.dot(xa, wu[:half], preferred_element_type=F32)
              + jnp.dot(xb, wu[half:], preferred_element_type=F32))
        hh = _silu(hg) * hu
        ys_ref[...] = _pack_halves(jnp.dot(hh.astype(BF16), wd_ref[0].astype(BF16),
                                           preferred_element_type=F32))

    @pl.when(i >= nu_ref[0])
    def _():
        ys_ref[...] = jnp.zeros_like(ys_ref)


def _expert_ffn(tile_expert, n_used, xs, w_gate, w_up, w_down):
    r, dw = xs.shape
    d = 2 * dw
    tm = FFN_TILE
    f = w_gate.shape[-1]
    grid_spec = pltpu.PrefetchScalarGridSpec(
        num_scalar_prefetch=2,
        grid=(r // tm,),
        in_specs=[
            pl.BlockSpec((tm, dw), lambda i, te, nu: (i, 0)),
            pl.BlockSpec((1, d, f), lambda i, te, nu: (te[i], 0, 0)),
            pl.BlockSpec((1, d, f), lambda i, te, nu: (te[i], 0, 0)),
            pl.BlockSpec((1, f, d), lambda i, te, nu: (te[i], 0, 0)),
        ],
        out_specs=pl.BlockSpec((tm, dw), lambda i, te, nu: (i, 0)),
    )
    return pl.pallas_call(
        _expert_ffn_kernel, grid_spec=grid_spec,
        out_shape=jax.ShapeDtypeStruct((r, dw), jnp.uint32),
        compiler_params=_params("arbitrary"), name="expert_ffn",
    )(tile_expert, n_used, xs, w_gate, w_up, w_down)


def _ple_kernel(final, h_ref, m0_ref, m1_ref, rw_ref, p_ref, gain_ref, wg_ref, wp_ref, fgain_ref, o_ref):
    rw = rw_ref[...]
    a0, b0 = _unpack_halves(m0_ref[...])
    a1, b1 = _unpack_halves(m1_ref[...])
    w0 = rw[:, 0:1]
    w1 = rw[:, 1:2]
    moe = jnp.concatenate([w0 * a0 + w1 * a1, w0 * b0 + w1 * b1], axis=1)
    h2 = h_ref[...] + moe
    xn = _rms(h2, gain_ref[...]).astype(BF16)
    gate = _sigmoid(jnp.dot(xn, wg_ref[...], preferred_element_type=F32))
    emb = jnp.dot(p_ref[...].astype(BF16), wp_ref[...], preferred_element_type=F32)
    h3 = h2 + gate * emb
    if final:
        h3 = _rms(h3, fgain_ref[...])
    o_ref[...] = h3


def _ple(h, m01, rw, p, layer, gain, w_gate, w_proj, final_gain, final):
    t, d = h.shape
    tm = TOKEN_TILE
    pd = p.shape[1]
    nt = t // tm
    row = lambda i: (i, 0)
    fix = lambda i: (0, 0)
    return pl.pallas_call(
        functools.partial(_ple_kernel, final),
        grid=(nt,),
        in_specs=[pl.BlockSpec((tm, d), row), pl.BlockSpec((tm, d // 2), row),
                  pl.BlockSpec((tm, d // 2), lambda i: (i + nt, 0)),
                  pl.BlockSpec((tm, LANES), row), pl.BlockSpec((tm, pd), lambda i: (i + layer * nt, 0)),
                  pl.BlockSpec((1, d), fix),
                  pl.BlockSpec(w_gate.shape, fix), pl.BlockSpec(w_proj.shape, fix),
                  pl.BlockSpec((1, d), fix)],
        out_specs=pl.BlockSpec((tm, d), row),
        out_shape=jax.ShapeDtypeStruct((t, d), F32),
        compiler_params=_params("parallel"), name="moe_residual_ple",
    )(h, m01, m01, rw, p, gain, w_gate, w_proj, final_gain)


def _routing_tables(ri, counts):
    t = ri.shape[1]
    tm = FFN_TILE
    n_rows = 2 * t + MOE_N_EXPERTS * tm
    padded = ((counts + tm - 1) // tm) * tm
    ends = jnp.cumsum(padded)
    starts = ends - padded
    ids = ri[0:2]
    offs = jnp.zeros_like(ids)
    for e in range(MOE_N_EXPERTS):
        offs = jnp.where(ids == e, starts[e], offs)
    dest = (ri[2:4] + offs).reshape(-1)
    src_tok = _sc_invert_slots(dest, n_rows, t)
    tile_start = jnp.arange(n_rows // tm, dtype=jnp.int32) * tm
    tile_expert = jnp.minimum(jnp.sum((ends[None, :] <= tile_start[:, None]).astype(jnp.int32), axis=1),
                              MOE_N_EXPERTS - 1)
    n_used = (ends[-1] // tm).astype(jnp.int32).reshape(1)
    return src_tok, tile_expert, n_used, dest


def _moe_ple(layer, mix, h, w_out, moe_gain, w_router, b_router, w_gate, w_up, w_down,
             p, ple_gain, ple_w_gate, ple_w_proj, final_gain, final):
    h1, xn, rw, ri, cnt = _out_router(mix, h, w_out, moe_gain, w_router, b_router)
    counts = cnt[0, MOE_GROUPS:MOE_GROUPS + MOE_N_EXPERTS].astype(jnp.int32)
    src_tok, tile_expert, n_used, dest = _routing_tables(ri, counts)
    xs = _sc_gather_rows(xn, src_tok)
    ys = _expert_ffn(tile_expert + layer * MOE_N_EXPERTS, n_used, xs, w_gate, w_up, w_down)
    m01 = _sc_gather_rows(ys, dest)
    return _ple(h1, m01, rw, p, layer, ple_gain, ple_w_gate, ple_w_proj, final_gain, final)


def _pad_lanes(v):
    return jnp.zeros((1, LANES), F32).at[0, :v.shape[0]].set(v.astype(F32))


def kernel(x, p, attn_norm, moe_norm, ple_norm, gdn_w_in, gdn_conv, gdn_a_log, gdn_dt_bias, gdn_o_norm, gdn_w_out, kv_norm, w_kv, sb_w_q, sb_w_out, moe_w_group, moe_b_group, moe_w_expert, moe_b_expert, moe_w_gate, moe_w_up, moe_w_down, ple_w_gate, ple_w_proj, final_norm):
    d = x.shape[-1]
    kd = GDN_HEADS * GDN_HEAD_DIM
    qkv_dim = 3 * kd

    def router_params(i):
        wr = jnp.zeros((d, LANES), F32)
        wr = wr.at[:, :MOE_GROUPS].set(moe_w_group[i])
        wr = wr.at[:, MOE_GROUPS:MOE_GROUPS + MOE_N_EXPERTS].set(moe_w_expert[i])
        br = jnp.zeros((1, LANES), F32)
        br = br.at[0, :MOE_GROUPS].set(moe_b_group[i])
        br = br.at[0, MOE_GROUPS:MOE_GROUPS + MOE_N_EXPERTS].set(moe_b_expert[i])
        return wr, br

    experts = (moe_w_gate.reshape(-1, d, MOE_D_EXPERT), moe_w_up.reshape(-1, d, MOE_D_EXPERT),
               moe_w_down.reshape(-1, MOE_D_EXPERT, d))

    w_in = gdn_w_in[0]
    w_ab = jnp.zeros((d, LANES), F32).at[:, :2 * GDN_HEADS].set(w_in[:, qkv_dim + kd:])
    w_in_parts = [w_in[:, :qkv_dim].astype(BF16), w_in[:, qkv_dim:qkv_dim + kd].astype(BF16), w_ab.astype(BF16)]
    dtb = jnp.zeros((1, LANES), F32).at[0, :GDN_HEADS].set(gdn_dt_bias[0])
    alog = _pad_lanes(gdn_a_log[0])
    routers = [router_params(0), router_params(1)]
    mix_w_out = [gdn_w_out[0].astype(BF16), sb_w_out[0].astype(BF16)]
    ple_wg = [ple_w_gate[0].astype(BF16), ple_w_gate[1].astype(BF16)]
    ple_wp = [ple_w_proj[0].astype(BF16), ple_w_proj[1].astype(BF16)]
    gains1 = jnp.stack([attn_norm[1], kv_norm], axis=0)
    w_q = sb_w_q[0].astype(BF16)
    w_kv_b = w_kv.astype(BF16)
    fgain = final_norm.reshape(1, d)

    b, s, _ = x.shape
    t = b * s
    h = x.reshape(t, d)
    p_all = p.reshape(-1, p.shape[-1])
    qkv, z, ab = _norm_matmul(h, attn_norm[0:1], w_in_parts,
                              [(0, 1.0, BF16), (0, 1.0, BF16), (0, 1.0, F32)])
    og = _gated_deltanet_core(
        qkv.reshape(b, s, qkv_dim), z.reshape(b, s, kd), ab.reshape(b, s, LANES),
        gdn_conv[0], alog, dtb, gdn_o_norm[0].reshape(1, GDN_HEAD_DIM))
    h = _moe_ple(0, og.reshape(t, kd), h, mix_w_out[0], moe_norm[0:1], *routers[0], *experts,
                 p_all, ple_norm[0:1], ple_wg[0], ple_wp[0], fgain, False)
    q, kv = _norm_matmul(h, gains1, [w_q, w_kv_b],
                         [(0, SB_HEAD_DIM ** -0.5, BF16), (1, 1.0, BF16)])
    nb = s // SB_BLOCK
    kvw = SB_KV_HEADS * SB_HEAD_DIM
    k5 = kv[:, :kvw].reshape(b, nb, SB_BLOCK, SB_KV_HEADS, SB_HEAD_DIM)
    v5 = kv[:, kvw:].reshape(b, nb, SB_BLOCK, SB_KV_HEADS, SB_HEAD_DIM)
    kt = k5.transpose(0, 3, 1, 4, 2)
    vv = v5.transpose(0, 3, 1, 2, 4)
    oa = _stick_breaking_attention(q.reshape(b, s, -1), kt, vv)
    out = _moe_ple(1, oa.reshape(t, -1), h, mix_w_out[1], moe_norm[1:2], *routers[1], *experts,
                   p_all, ple_norm[1:2], ple_wg[1], ple_wp[1], fgain, True)
    return out.reshape(b, s, d)
```

```python
import functools

import jax
import jax.numpy as jnp
from jax import lax
from jax.experimental import pallas as pl
from jax.experimental.pallas import tpu as pltpu
from jax.experimental.pallas import tpu_sc as plsc

NORM_EPS = 1e-6
LOG2E = 1.4426950408889634
LANES = 128
GDN_HEADS = 8
GDN_HEAD_DIM = 128
GDN_CONV = 4
GDN_CHUNK = 64
GDN_HALO = 16
SB_Q_HEADS = 16
SB_KV_HEADS = 4
SB_GROUP = SB_Q_HEADS // SB_KV_HEADS
SB_HEAD_DIM = 64
SB_BLOCK = 128
MOE_GROUPS = 4
MOE_EXPERTS_PER_GROUP = 8
MOE_N_EXPERTS = MOE_GROUPS * MOE_EXPERTS_PER_GROUP
MOE_D_EXPERT = 256

VMEM_LIMIT = 56 * 1024 * 1024
TOKEN_TILE = 512
FFN_TILE = 256
GDN_TIME_BLOCK = 512
ROUTE_COLS = 8
SC_GATHER_ROWS = 64
SC_LANES = 16
SC_INDEX_CHUNK = 4096

F32 = jnp.float32
BF16 = jnp.bfloat16


def _params(*sem):
    return pltpu.CompilerParams(dimension_semantics=sem, vmem_limit_bytes=VMEM_LIMIT)


def _dot(a, b):
    return jnp.dot(a.astype(BF16), b.astype(BF16), preferred_element_type=F32)


def _dot_nt(a, b):
    return lax.dot_general(a.astype(BF16), b.astype(BF16), (((1,), (1,)), ((), ())),
                           preferred_element_type=F32)


def _dot_tn(a, b):
    return lax.dot_general(a.astype(BF16), b.astype(BF16), (((0,), (0,)), ((), ())),
                           preferred_element_type=F32)


def _dot_f32(a, b):
    return jnp.dot(a, b, precision=lax.Precision.HIGHEST, preferred_element_type=F32)


def _rms(x, gain):
    return x * lax.rsqrt(jnp.mean(x * x, axis=-1, keepdims=True) + NORM_EPS) * gain


def _silu(x):
    return x * (1.0 / (1.0 + jnp.exp(-x)))


def _sigmoid(x):
    return 1.0 / (1.0 + jnp.exp(-x))


def _softplus(x):
    return jnp.maximum(x, 0.0) + jnp.log(1.0 + jnp.exp(-jnp.abs(x)))


def _pack_halves(x):
    w = x.shape[1] // 2
    hi = pltpu.bitcast(x[:, :w].astype(BF16).astype(F32), jnp.uint32)
    lo = pltpu.bitcast(x[:, w:].astype(BF16).astype(F32), jnp.uint32)
    return (hi & jnp.uint32(0xFFFF0000)) | (lo >> 16)


def _unpack_halves(u):
    hi = pltpu.bitcast(u & jnp.uint32(0xFFFF0000), F32)
    lo = pltpu.bitcast(u << 16, F32)
    return hi, lo


def _sc_gather_rows(table, idx):
    _, w = table.shape
    m = idx.shape[0]
    info = plsc.get_sparse_core_info()
    nc, ns = info.num_cores, info.num_subcores
    rows = SC_GATHER_ROWS
    assert m % (nc * ns * rows * 2) == 0
    per_w = m // (nc * ns)
    n_pairs = per_w // (2 * rows)
    mesh = plsc.VectorSubcoreMesh(core_axis_name="c", subcore_axis_name="s")

    @functools.partial(
        pl.kernel, mesh=mesh,
        out_type=jax.ShapeDtypeStruct((m, w), table.dtype),
        scratch_types=[pltpu.VMEM((2, rows), jnp.int32),
                       pltpu.VMEM((2, rows, w), table.dtype),
                       pltpu.SemaphoreType.DMA((2,)),
                       pltpu.SemaphoreType.DMA((2,))],
    )
    def gather(table_hbm, idx_hbm, out_hbm, idx_v, rows_v, gsem, wsem):
        wid = lax.axis_index("s") * nc + lax.axis_index("c")
        base = wid * per_w

        def fetch(chunk, slot):
            off = pl.multiple_of(base + chunk * rows, rows)
            pltpu.sync_copy(idx_hbm.at[pl.ds(off, rows)], idx_v.at[slot])
            return pltpu.async_copy(table_hbm.at[idx_v.at[slot]], rows_v.at[slot], gsem.at[slot])

        def write(chunk, slot):
            off = pl.multiple_of(base + chunk * rows, rows)
            return pltpu.async_copy(rows_v.at[slot], out_hbm.at[pl.ds(off, rows)], wsem.at[slot])

        @pl.loop(0, n_pairs)
        def _(pi):
            g0 = fetch(2 * pi, 0)
            g1 = fetch(2 * pi + 1, 1)
            g0.wait()
            w0 = write(2 * pi, 0)
            g1.wait()
            w1 = write(2 * pi + 1, 1)
            w0.wait()
            w1.wait()

    return gather(table, idx)


def _sc_invert_slots(dest, n_rows, n_tokens):
    n_pairs = dest.shape[0]
    ch = SC_INDEX_CHUNK
    assert n_pairs % ch == 0 and n_rows % SC_LANES == 0
    nc = plsc.get_sparse_core_info().num_cores
    mesh = plsc.VectorSubcoreMesh(core_axis_name="c", subcore_axis_name="s")

    @functools.partial(
        pl.kernel, mesh=mesh,
        out_type=jax.ShapeDtypeStruct((n_rows,), jnp.int32),
        scratch_types=[pltpu.VMEM((n_rows,), jnp.int32), pltpu.VMEM((ch,), jnp.int32)],
        compiler_params=pltpu.CompilerParams(needs_layout_passes=False),
    )
    def invert(dest_hbm, out_hbm, tab_v, dest_v):
        wid = lax.axis_index("s") * nc + lax.axis_index("c")

        @pl.when(wid == 0)
        def _():
            lanes = lax.iota(jnp.int32, SC_LANES)

            @pl.loop(0, n_rows // SC_LANES)
            def _(i):
                tab_v[pl.ds(pl.multiple_of(i * SC_LANES, SC_LANES), SC_LANES)] = lax.rem(
                    i * SC_LANES + lanes, n_tokens)

            @pl.loop(0, n_pairs // ch)
            def _(c):
                pltpu.sync_copy(dest_hbm.at[pl.ds(pl.multiple_of(c * ch, ch), ch)], dest_v)

                @pl.loop(0, ch // SC_LANES)
                def _(j):
                    d = dest_v[pl.ds(pl.multiple_of(j * SC_LANES, SC_LANES), SC_LANES)]
                    pair = c * ch + j * SC_LANES + lanes
                    plsc.store_scatter(tab_v, [d], lax.rem(pair, n_tokens))

            pltpu.sync_copy(tab_v, out_hbm)

    return invert(dest)


def _norm_matmul_kernel(plan, n_chunk, x_ref, gains_ref, *refs):
    n_w = len(plan)
    w_refs, o_refs = refs[:n_w], refs[n_w:]
    x = x_ref[...]
    inv = lax.rsqrt(jnp.mean(x * x, axis=-1, keepdims=True) + NORM_EPS)
    xn = {}
    for (g, _, _) in plan:
        if g not in xn:
            xn[g] = (x * inv * gains_ref[g:g + 1, :]).astype(BF16)
    for (g, scale, _), w_ref, o_ref in zip(plan, w_refs, o_refs):
        n = w_ref.shape[1]
        for n0 in range(0, n, n_chunk):
            n1 = min(n, n0 + n_chunk)
            acc = jnp.dot(xn[g], w_ref[:, n0:n1], preferred_element_type=F32)
            if scale != 1.0:
                acc = acc * scale
            o_ref[:, n0:n1] = acc.astype(o_ref.dtype)


def _norm_matmul(x, gains, ws, plan):
    t, d = x.shape
    tm = TOKEN_TILE
    in_specs = [pl.BlockSpec((tm, d), lambda i: (i, 0)),
                pl.BlockSpec(gains.shape, lambda i: (0, 0))]
    in_specs += [pl.BlockSpec(w.shape, lambda i: (0, 0)) for w in ws]
    out_specs = [pl.BlockSpec((tm, w.shape[1]), lambda i: (i, 0)) for w in ws]
    out_shape = [jax.ShapeDtypeStruct((t, w.shape[1]), p[2]) for w, p in zip(ws, plan)]
    return pl.pallas_call(
        functools.partial(_norm_matmul_kernel, tuple(plan), 512),
        grid=(t // tm,), in_specs=in_specs, out_specs=out_specs, out_shape=out_shape,
        compiler_params=_params("parallel"), name="norm_matmul",
    )(x, gains, *ws)


def _gdn_kernel(n_heads, ts, qkv_ref, z_ref, ab_ref, conv_ref, alog_ref, dtb_ref, onorm_ref,
                o_ref, xpad_ref, state_ref, g_ref, beta_ref):
    c = GDN_CHUNK
    dk = GDN_HEAD_DIM
    tb = pl.program_id(1)
    n_ch = ts // c
    qkv_dim = qkv_ref.shape[-1]

    @pl.when(tb == 0)
    def _():
        state_ref[...] = jnp.zeros_like(state_ref)
        xpad_ref[0:GDN_HALO, :] = jnp.zeros((GDN_HALO, qkv_dim), xpad_ref.dtype)

    xpad_ref[GDN_HALO:GDN_HALO + ts, :] = qkv_ref[0]

    ab = ab_ref[0]
    g_all = -jnp.exp(alog_ref[...]) * _softplus(ab + dtb_ref[...])
    beta_all = _sigmoid(ab)
    rin = lax.broadcasted_iota(jnp.int32, (ts, LANES), 0) & (c - 1)
    for h in range(n_heads):
        gh = jnp.broadcast_to(g_all[:, h:h + 1], (ts, LANES))
        shift = 1
        while shift < c:
            gh = gh + jnp.where(rin >= shift, pltpu.roll(gh, shift, 0), 0.0)
            shift *= 2
        g_ref[h] = gh
        beta_ref[h] = jnp.broadcast_to(beta_all[:, n_heads + h:n_heads + h + 1], (ts, LANES))

    row = lax.broadcasted_iota(jnp.int32, (c, c), 0)
    col = lax.broadcasted_iota(jnp.int32, (c, c), 1)
    incl = row >= col
    strict = row > col

    srow = lax.broadcasted_iota(jnp.int32, (GDN_CONV * c, c + GDN_HALO), 0)
    scol = lax.broadcasted_iota(jnp.int32, (GDN_CONV * c, c + GDN_HALO), 1)
    shift_sel = jnp.where(scol == (srow & (c - 1)) + (srow >> (c.bit_length() - 1)) + (GDN_HALO - GDN_CONV + 1),
                          1.0, 0.0).astype(BF16)

    def conv_silu(r0, c0):
        win = xpad_ref[pl.ds(r0, c + GDN_HALO), c0:c0 + LANES]
        taps = jnp.dot(shift_sel, win, preferred_element_type=F32)
        acc = taps[0:c] * conv_ref[0:1, c0:c0 + LANES]
        for j in range(1, GDN_CONV):
            acc = acc + taps[j * c:(j + 1) * c] * conv_ref[j:j + 1, c0:c0 + LANES]
        return _silu(acc)

    def l2n(t):
        return t * lax.rsqrt(jnp.sum(t * t, axis=-1, keepdims=True) + NORM_EPS)

    heads = range(n_heads)
    eye = (row == col).astype(F32)
    same_blk = (row >> 4) == (col >> 4)

    def chunk_body(n, carry):
        r0 = pl.multiple_of(n * c, c)
        q = [l2n(conv_silu(r0, h * dk)) * (dk ** -0.5) for h in heads]
        k = [l2n(conv_silu(r0, (n_heads + h) * dk)) for h in heads]
        v = [conv_silu(r0, (2 * n_heads + h) * dk) for h in heads]
        gc = [g_ref[h, pl.ds(r0, c), :] for h in heads]
        beta = [beta_ref[h, pl.ds(r0, c), :] for h in heads]
        kb = [k[h] * beta[h] for h in heads]
        kk = [_dot_nt(kb[h], k[h]) for h in heads]
        qk = [_dot_nt(q[h], k[h]) for h in heads]
        decay = []
        for h in heads:
            diff = gc[h][:, 0:c] - gc[h].T[0:c, :]
            decay.append(jnp.where(incl, jnp.exp(jnp.where(incl, diff, 0.0)), 0.0))
        lm = [jnp.where(strict, kk[h] * decay[h], 0.0) for h in heads]
        attn = [jnp.where(incl, qk[h] * decay[h], 0.0) for h in heads]
        ld = [jnp.where(same_blk, lm[h], 0.0) for h in heads]
        lo = [lm[h] - ld[h] for h in heads]
        p = [eye - ld[h] for h in heads]
        sq = [_dot(ld[h], ld[h]) for h in heads]
        for _ in range(2):
            pn = [_dot(p[h], sq[h]) for h in heads]
            sq2 = [_dot(sq[h], sq[h]) for h in heads]
            p = [p[h] + pn[h] for h in heads]
            sq = sq2
        dinv = [p[h] + _dot(p[h], sq[h]) for h in heads]
        m = [_dot(dinv[h], lo[h]) for h in heads]
        m2 = [_dot(m[h], m[h]) for h in heads]
        r = [(eye - m[h]) + _dot(eye - m[h], m2[h]) for h in heads]
        tmat = [_dot(r[h], dinv[h]) for h in heads]
        eg = [jnp.exp(gc[h]) for h in heads]
        rhs = [jnp.concatenate([v[h] * beta[h], kb[h] * eg[h]], axis=1) for h in heads]
        uw = [_dot(tmat[h], rhs[h]) for h in heads]
        s = [state_ref[h] for h in heads]
        lhs = [jnp.concatenate([uw[h][:, dk:], q[h] * eg[h]], axis=0) for h in heads]
        ws_qs = [_dot(lhs[h], s[h]) for h in heads]
        v_new = [uw[h][:, :dk] - ws_qs[h][:c] for h in heads]
        g_last = [gc[h][c - 1:c, :] for h in heads]
        k_dec = [k[h] * jnp.exp(g_last[h] - gc[h]) for h in heads]
        av = [_dot(attn[h], v_new[h]) for h in heads]
        kv = [_dot_tn(k_dec[h], v_new[h]) for h in heads]
        for h in heads:
            state_ref[h] = s[h] * jnp.exp(g_last[h]) + kv[h]
            o = ws_qs[h][c:] + av[h]
            o = o * lax.rsqrt(jnp.mean(o * o, axis=-1, keepdims=True) + NORM_EPS) * onorm_ref[...]
            zz = z_ref[0, pl.ds(r0, c), h * dk:(h + 1) * dk].astype(F32)
            o_ref[0, pl.ds(r0, c), h * dk:(h + 1) * dk] = (o * _silu(zz)).astype(o_ref.dtype)
        return carry

    lax.fori_loop(0, n_ch, chunk_body, 0)
    xpad_ref[0:GDN_HALO, :] = xpad_ref[ts:ts + GDN_HALO, :]


def _gated_deltanet_core(qkv, z, ab, conv_w, alog, dtb, onorm):
    b, s, qkv_dim = qkv.shape
    n_heads = GDN_HEADS
    ts = min(GDN_TIME_BLOCK, s)
    vd = z.shape[-1]
    kern = functools.partial(_gdn_kernel, n_heads, ts)
    return pl.pallas_call(
        kern,
        grid=(b, s // ts),
        in_specs=[
            pl.BlockSpec((1, ts, qkv_dim), lambda i, j: (i, j, 0)),
            pl.BlockSpec((1, ts, vd), lambda i, j: (i, j, 0)),
            pl.BlockSpec((1, ts, LANES), lambda i, j: (i, j, 0)),
            pl.BlockSpec(conv_w.shape, lambda i, j: (0, 0)),
            pl.BlockSpec((1, LANES), lambda i, j: (0, 0)),
            pl.BlockSpec((1, LANES), lambda i, j: (0, 0)),
            pl.BlockSpec((1, GDN_HEAD_DIM), lambda i, j: (0, 0)),
        ],
        out_specs=pl.BlockSpec((1, ts, vd), lambda i, j: (i, j, 0)),
        out_shape=jax.ShapeDtypeStruct((b, s, vd), BF16),
        scratch_shapes=[
            pltpu.VMEM((ts + GDN_HALO, qkv_dim), BF16),
            pltpu.VMEM((n_heads, GDN_HEAD_DIM, GDN_HEAD_DIM), F32),
            pltpu.VMEM((n_heads, ts, LANES), F32),
            pltpu.VMEM((n_heads, ts, LANES), F32),
        ],
        compiler_params=_params("parallel", "arbitrary"), name="gdn_delta_rule",
    )(qkv, z, ab, conv_w, alog, dtb, onorm)


def _sb_attn_kernel(q_ref, kt_ref, v_ref, o_ref, acc_ref, cs_ref, z_ref, p_ref):
    blk = SB_BLOCK
    dh = SB_HEAD_DIM
    grp = SB_GROUP
    i = pl.program_id(2)
    rows = grp * blk
    n_pairs = 1 + i // 2
    qb = q_ref[0]
    q4 = jnp.concatenate([qb[:, g * dh:(g + 1) * dh] for g in range(grp)], axis=0)
    krow = lax.broadcasted_iota(jnp.int32, (blk, blk), 0)
    kcol = lax.broadcasted_iota(jnp.int32, (blk, blk), 1)
    neg_suffix = jnp.where(krow >= kcol, -1.0, 0.0).astype(BF16)

    def pair_blocks(k):
        ja = i - 2 * k
        return jnp.maximum(ja, 0), jnp.maximum(ja - 1, 0), (ja >= 1).astype(F32)

    def logits(k):
        ja, jb, _ = pair_blocks(k)
        z_ref[0] = jnp.dot(q4, kt_ref[0, 0, ja], preferred_element_type=F32)
        z_ref[1] = jnp.dot(q4, kt_ref[0, 0, jb], preferred_element_type=F32)

    def weights(mask_a):
        masks = [mask_a, None]
        zs = [z_ref[0], z_ref[1]]
        sps = []
        for z, mask in zip(zs, masks):
            sp = jnp.maximum(z, 0.0) + jnp.log(1.0 + jnp.exp2(jnp.abs(z) * -LOG2E))
            if mask is not None:
                sp = jnp.where(mask, sp, 0.0)
            sps.append(sp)
        sufs = [jnp.dot(sp.astype(BF16), neg_suffix, preferred_element_type=F32) for sp in sps]
        rss = [jnp.sum(sp, axis=-1, keepdims=True) for sp in sps]
        cs = cs_ref[...]
        for n, (z, suf, rs, mask) in enumerate(zip(zs, sufs, rss, masks)):
            p = jnp.exp2(((z + suf) - cs) * LOG2E)
            if mask is not None:
                p = jnp.where(mask, p, 0.0)
            p_ref[n] = p.astype(BF16)
            cs = cs + rs
        cs_ref[...] = cs

    def weighted_values(k):
        ja, jb, valid_b = pair_blocks(k)
        vb = (v_ref[0, 0, jb].astype(F32) * valid_b).astype(BF16)
        acc_ref[...] += (jnp.dot(p_ref[0], v_ref[0, 0, ja], preferred_element_type=F32)
                         + jnp.dot(p_ref[1], vb, preferred_element_type=F32))

    acc_ref[...] = jnp.zeros_like(acc_ref)
    cs_ref[...] = jnp.zeros_like(cs_ref)
    t_in = lax.broadcasted_iota(jnp.int32, (rows, blk), 0) & (blk - 1)
    s_in = lax.broadcasted_iota(jnp.int32, (rows, blk), 1)
    logits(0)
    weights(s_in < t_in)
    logits(1)

    def body(k, c):
        weighted_values(k - 1)
        weights(None)
        logits(k + 1)
        return c

    lax.fori_loop(1, n_pairs, body, 0)
    weighted_values(n_pairs - 1)
    acc = acc_ref[...]
    for g in range(grp):
        o_ref[0, :, g * dh:(g + 1) * dh] = acc[g * blk:(g + 1) * blk, :].astype(o_ref.dtype)


def _stick_breaking_attention(q, kt, v):
    b, s, qd = q.shape
    nb = s // SB_BLOCK
    gw = SB_GROUP * SB_HEAD_DIM
    rows = SB_GROUP * SB_BLOCK
    return pl.pallas_call(
        _sb_attn_kernel,
        grid=(b, SB_KV_HEADS, nb),
        in_specs=[
            pl.BlockSpec((1, SB_BLOCK, gw), lambda bi, g, i: (bi, i, g)),
            pl.BlockSpec((1, 1, nb, SB_HEAD_DIM, SB_BLOCK), lambda bi, g, i: (bi, g, 0, 0, 0)),
            pl.BlockSpec((1, 1, nb, SB_BLOCK, SB_HEAD_DIM), lambda bi, g, i: (bi, g, 0, 0, 0)),
        ],
        out_specs=pl.BlockSpec((1, SB_BLOCK, gw), lambda bi, g, i: (bi, i, g)),
        out_shape=jax.ShapeDtypeStruct((b, s, qd), BF16),
        scratch_shapes=[pltpu.VMEM((rows, SB_HEAD_DIM), F32), pltpu.VMEM((rows, LANES), F32),
                        pltpu.VMEM((2, rows, SB_BLOCK), F32), pltpu.VMEM((2, rows, SB_BLOCK), BF16)],
        compiler_params=_params("parallel", "parallel", "arbitrary"), name="stick_breaking_attention",
    )(q, kt, v)


def _out_router_kernel(mix_ref, h_ref, wout_ref, gain_ref, wrh_ref, wrl_ref, br_ref,
                       h1_ref, xn_ref, rw_ref, ri_ref, cnt_ref, run_ref):
    h1 = h_ref[...] + jnp.dot(mix_ref[...], wout_ref[...], preferred_element_type=F32)
    h1_ref[...] = h1
    xn = _rms(h1, gain_ref[...])
    xh = xn.astype(BF16)
    xn_ref[...] = _pack_halves(xn)
    xl = (xn - xh.astype(F32)).astype(BF16)
    logits = (jnp.dot(xh, wrh_ref[...], preferred_element_type=F32)
              + jnp.dot(xl, wrh_ref[...], preferred_element_type=F32)
              + jnp.dot(xh, wrl_ref[...], preferred_element_type=F32)
              + br_ref[...])
    lane = lax.broadcasted_iota(jnp.int32, logits.shape, 1).astype(F32)
    neg = jnp.float32(-jnp.inf)
    big = jnp.float32(1e9)
    gl = jnp.where(lane < MOE_GROUPS, logits, neg)
    gmax = jnp.max(gl, axis=-1, keepdims=True)
    gidx = jnp.min(jnp.where(gl == gmax, lane, big), axis=-1, keepdims=True)
    gp = 1.0 / jnp.sum(jnp.exp(gl - gmax), axis=-1, keepdims=True)
    lo = MOE_GROUPS + gidx * MOE_EXPERTS_PER_GROUP
    el = jnp.where((lane >= lo) & (lane < lo + MOE_EXPERTS_PER_GROUP), logits, neg)
    m1 = jnp.max(el, axis=-1, keepdims=True)
    i1 = jnp.min(jnp.where(el == m1, lane, big), axis=-1, keepdims=True)
    el2 = jnp.where(lane == i1, neg, el)
    m2 = jnp.max(el2, axis=-1, keepdims=True)
    i2 = jnp.min(jnp.where(el2 == m2, lane, big), axis=-1, keepdims=True)
    e2 = jnp.exp(m2 - m1)
    w1 = gp / (1.0 + e2)
    w2 = gp * e2 / (1.0 + e2)
    rw_ref[...] = jnp.where(lane == 0, w1, jnp.where(lane == 1, w2, 0.0))

    @pl.when(pl.program_id(0) == 0)
    def _():
        run_ref[...] = jnp.zeros_like(run_ref)

    tm = logits.shape[0]
    trow = lax.broadcasted_iota(jnp.int32, (tm, tm), 0)
    tcol = lax.broadcasted_iota(jnp.int32, (tm, tm), 1)
    before = jnp.where(tcol < trow, 1.0, 0.0).astype(BF16)
    hot1 = lane == i1
    hot2 = lane == i2
    oh1 = jnp.where(hot1, 1.0, 0.0)
    oh2 = jnp.where(hot2, 1.0, 0.0)
    prior1 = jnp.dot(before, oh1.astype(BF16), preferred_element_type=F32)
    prior2 = jnp.dot(before, oh2.astype(BF16), preferred_element_type=F32)
    cnt1 = jnp.sum(oh1, axis=0, keepdims=True)
    cnt2 = jnp.sum(oh2, axis=0, keepdims=True)
    run = run_ref[...]
    rank1 = jnp.sum(jnp.where(hot1, prior1 + run, 0.0), axis=-1, keepdims=True)
    rank2 = jnp.sum(jnp.where(hot2, prior2 + (run + cnt1), 0.0), axis=-1, keepdims=True)
    run = run + cnt1 + cnt2
    run_ref[...] = run
    cnt_ref[...] = run
    ri = jnp.where(lane == 0, i1 - MOE_GROUPS, jnp.where(lane == 1, i2 - MOE_GROUPS,
                   jnp.where(lane == 2, rank1, jnp.where(lane == 3, rank2, 0.0))))
    ri_ref[...] = ri.T[0:ROUTE_COLS, :].astype(jnp.int32)


def _out_router(mix, h, w_out, gain, w_router, b_router):
    t, d = h.shape
    tm = TOKEN_TILE
    kd = mix.shape[1]
    row = lambda i: (i, 0)
    fix = lambda i: (0, 0)
    wr_hi = w_router.astype(BF16)
    wr_lo = (w_router - wr_hi.astype(F32)).astype(BF16)
    return pl.pallas_call(
        _out_router_kernel,
        grid=(t // tm,),
        in_specs=[pl.BlockSpec((tm, kd), row), pl.BlockSpec((tm, d), row),
                  pl.BlockSpec(w_out.shape, fix), pl.BlockSpec((1, d), fix),
                  pl.BlockSpec(w_router.shape, fix), pl.BlockSpec(w_router.shape, fix),
                  pl.BlockSpec((1, LANES), fix)],
        out_specs=[pl.BlockSpec((tm, d), row), pl.BlockSpec((tm, d // 2), row),
                   pl.BlockSpec((tm, LANES), row), pl.BlockSpec((ROUTE_COLS, tm), lambda i: (0, i)),
                   pl.BlockSpec((1, LANES), fix)],
        out_shape=[jax.ShapeDtypeStruct((t, d), F32), jax.ShapeDtypeStruct((t, d // 2), jnp.uint32),
                   jax.ShapeDtypeStruct((t, LANES), F32), jax.ShapeDtypeStruct((ROUTE_COLS, t), jnp.int32),
                   jax.ShapeDtypeStruct((1, LANES), F32)],
        scratch_shapes=[pltpu.VMEM((1, LANES), F32)],
        compiler_params=_params("arbitrary"), name="out_proj_router",
    )(mix, h, w_out, gain, wr_hi, wr_lo, b_router)


def _expert_ffn_kernel(te_ref, nu_ref, xs_ref, wg_ref, wu_ref, wd_ref, ys_ref):
    i = pl.program_id(0)

    @pl.when(i < nu_ref[0])
    def _():
        xa, xb = _unpack_halves(xs_ref[...])
        xa = xa.astype(BF16)
        xb = xb.astype(BF16)
        half = xa.shape[1]
        wg = wg_ref[0].astype(BF16)
        wu = wu_ref[0].astype(BF16)
        hg = (jnp.dot(xa, wg[:half], preferred_element_type=F32)
              + jnp.dot(xb, wg[half:], preferred_element_type=F32))
        hu = (jnp.dot(xa, wu[:half], preferred_element_type=F32)
              + jnp.dot(xb, wu[half:], preferred_element_type=F32))
        hh = _silu(hg) * hu
        ys_ref[...] = _pack_halves(jnp.dot(hh.astype(BF16), wd_ref[0].astype(BF16),
                                           preferred_element_type=F32))

    @pl.when(i >= nu_ref[0])
    def _():
        ys_ref[...] = jnp.zeros_like(ys_ref)


def _expert_ffn(tile_expert, n_used, xs, w_gate, w_up, w_down):
    r, dw = xs.shape
    d = 2 * dw
    tm = FFN_TILE
    f = w_gate.shape[-1]
    grid_spec = pltpu.PrefetchScalarGridSpec(
        num_scalar_prefetch=2,
        grid=(r // tm,),
        in_specs=[
            pl.BlockSpec((tm, dw), lambda i, te, nu: (i, 0)),
            pl.BlockSpec((1, d, f), lambda i, te, nu: (te[i], 0, 0)),
            pl.BlockSpec((1, d, f), lambda i, te, nu: (te[i], 0, 0)),
            pl.BlockSpec((1, f, d), lambda i, te, nu: (te[i], 0, 0)),
        ],
        out_specs=pl.BlockSpec((tm, dw), lambda i, te, nu: (i, 0)),
    )
    return pl.pallas_call(
        _expert_ffn_kernel, grid_spec=grid_spec,
        out_shape=jax.ShapeDtypeStruct((r, dw), jnp.uint32),
        compiler_params=_params("arbitrary"), name="expert_ffn",
    )(tile_expert, n_used, xs, w_gate, w_up, w_down)


def _ple_kernel(final, h_ref, m0_ref, m1_ref, rw_ref, p_ref, gain_ref, wg_ref, wp_ref, fgain_ref, o_ref):
    rw = rw_ref[...]
    a0, b0 = _unpack_halves(m0_ref[...])
    a1, b1 = _unpack_halves(m1_ref[...])
    w0 = rw[:, 0:1]
    w1 = rw[:, 1:2]
    moe = jnp.concatenate([w0 * a0 + w1 * a1, w0 * b0 + w1 * b1], axis=1)
    h2 = h_ref[...] + moe
    xn = _rms(h2, gain_ref[...]).astype(BF16)
    gate = _sigmoid(jnp.dot(xn, wg_ref[...], preferred_element_type=F32))
    emb = jnp.dot(p_ref[...].astype(BF16), wp_ref[...], preferred_element_type=F32)
    h3 = h2 + gate * emb
    if final:
        h3 = _rms(h3, fgain_ref[...])
    o_ref[...] = h3


def _ple(h, m01, rw, p, layer, gain, w_gate, w_proj, final_gain, final):
    t, d = h.shape
    tm = TOKEN_TILE
    pd = p.shape[1]
    nt = t // tm
    row = lambda i: (i, 0)
    fix = lambda i: (0, 0)
    return pl.pallas_call(
        functools.partial(_ple_kernel, final),
        grid=(nt,),
        in_specs=[pl.BlockSpec((tm, d), row), pl.BlockSpec((tm, d // 2), row),
                  pl.BlockSpec((tm, d // 2), lambda i: (i + nt, 0)),
                  pl.BlockSpec((tm, LANES), row), pl.BlockSpec((tm, pd), lambda i: (i + layer * nt, 0)),
                  pl.BlockSpec((1, d), fix),
                  pl.BlockSpec(w_gate.shape, fix), pl.BlockSpec(w_proj.shape, fix),
                  pl.BlockSpec((1, d), fix)],
        out_specs=pl.BlockSpec((tm, d), row),
        out_shape=jax.ShapeDtypeStruct((t, d), F32),
        compiler_params=_params("parallel"), name="moe_residual_ple",
    )(h, m01, m01, rw, p, gain, w_gate, w_proj, final_gain)


def _routing_tables(ri, counts):
    t = ri.shape[1]
    tm = FFN_TILE
    n_rows = 2 * t + MOE_N_EXPERTS * tm
    padded = ((counts + tm - 1) // tm) * tm
    ends = jnp.cumsum(padded)
    starts = ends - padded
    ids = ri[0:2]
    offs = jnp.zeros_like(ids)
    for e in range(MOE_N_EXPERTS):
        offs = jnp.where(ids == e, starts[e], offs)
    dest = (ri[2:4] + offs).reshape(-1)
    src_tok = _sc_invert_slots(dest, n_rows, t)
    tile_start = jnp.arange(n_rows // tm, dtype=jnp.int32) * tm
    tile_expert = jnp.minimum(jnp.sum((ends[None, :] <= tile_start[:, None]).astype(jnp.int32), axis=1),
                              MOE_N_EXPERTS - 1)
    n_used = (ends[-1] // tm).astype(jnp.int32).reshape(1)
    return src_tok, tile_expert, n_used, dest


def _moe_ple(layer, mix, h, w_out, moe_gain, w_router, b_router, w_gate, w_up, w_down,
             p, ple_gain, ple_w_gate, ple_w_proj, final_gain, final):
    h1, xn, rw, ri, cnt = _out_router(mix, h, w_out, moe_gain, w_router, b_router)
    counts = cnt[0, MOE_GROUPS:MOE_GROUPS + MOE_N_EXPERTS].astype(jnp.int32)
    src_tok, tile_expert, n_used, dest = _routing_tables(ri, counts)
    xs = _sc_gather_rows(xn, src_tok)
    ys = _expert_ffn(tile_expert + layer * MOE_N_EXPERTS, n_used, xs, w_gate, w_up, w_down)
    m01 = _sc_gather_rows(ys, dest)
    return _ple(h1, m01, rw, p, layer, ple_gain, ple_w_gate, ple_w_proj, final_gain, final)


def _pad_lanes(v):
    return jnp.zeros((1, LANES), F32).at[0, :v.shape[0]].set(v.astype(F32))


def kernel(x, p, attn_norm, moe_norm, ple_norm, gdn_w_in, gdn_conv, gdn_a_log, gdn_dt_bias, gdn_o_norm, gdn_w_out, kv_norm, w_kv, sb_w_q, sb_w_out, moe_w_group, moe_b_group, moe_w_expert, moe_b_expert, moe_w_gate, moe_w_up, moe_w_down, ple_w_gate, ple_w_proj, final_norm):
    d = x.shape[-1]
    kd = GDN_HEADS * GDN_HEAD_DIM
    qkv_dim = 3 * kd

    def router_params(i):
        wr = jnp.zeros((d, LANES), F32)
        wr = wr.at[:, :MOE_GROUPS].set(moe_w_group[i])
        wr = wr.at[:, MOE_GROUPS:MOE_GROUPS + MOE_N_EXPERTS].set(moe_w_expert[i])
        br = jnp.zeros((1, LANES), F32)
        br = br.at[0, :MOE_GROUPS].set(moe_b_group[i])
        br = br.at[0, MOE_GROUPS:MOE_GROUPS + MOE_N_EXPERTS].set(moe_b_expert[i])
        return wr, br

    experts = (moe_w_gate.reshape(-1, d, MOE_D_EXPERT), moe_w_up.reshape(-1, d, MOE_D_EXPERT),
               moe_w_down.reshape(-1, MOE_D_EXPERT, d))

    w_in = gdn_w_in[0]
    w_ab = jnp.zeros((d, LANES), F32).at[:, :2 * GDN_HEADS].set(w_in[:, qkv_dim + kd:])
    w_in_parts = [w_in[:, :qkv_dim].astype(BF16), w_in[:, qkv_dim:qkv_dim + kd].astype(BF16), w_ab.astype(BF16)]
    dtb = jnp.zeros((1, LANES), F32).at[0, :GDN_HEADS].set(gdn_dt_bias[0])
    alog = _pad_lanes(gdn_a_log[0])
    routers = [router_params(0), router_params(1)]
    mix_w_out = [gdn_w_out[0].astype(BF16), sb_w_out[0].astype(BF16)]
    ple_wg = [ple_w_gate[0].astype(BF16), ple_w_gate[1].astype(BF16)]
    ple_wp = [ple_w_proj[0].astype(BF16), ple_w_proj[1].astype(BF16)]
    gains1 = jnp.stack([attn_norm[1], kv_norm], axis=0)
    w_q = sb_w_q[0].astype(BF16)
    w_kv_b = w_kv.astype(BF16)
    fgain = final_norm.reshape(1, d)

    b, s, _ = x.shape
    t = b * s
    h = x.reshape(t, d)
    p_all = p.reshape(-1, p.shape[-1])
    qkv, z, ab = _norm_matmul(h, attn_norm[0:1], w_in_parts,
                              [(0, 1.0, BF16), (0, 1.0, BF16), (0, 1.0, F32)])
    og = _gated_deltanet_core(
        qkv.reshape(b, s, qkv_dim), z.reshape(b, s, kd), ab.reshape(b, s, LANES),
        gdn_conv[0], alog, dtb, gdn_o_norm[0].reshape(1, GDN_HEAD_DIM))
    h = _moe_ple(0, og.reshape(t, kd), h, mix_w_out[0], moe_norm[0:1], *routers[0], *experts,
                 p_all, ple_norm[0:1], ple_wg[0], ple_wp[0], fgain, False)
    q, kv = _norm_matmul(h, gains1, [w_q, w_kv_b],
                         [(0, SB_HEAD_DIM ** -0.5, BF16), (1, 1.0, BF16)])
    nb = s // SB_BLOCK
    kvw = SB_KV_HEADS * SB_HEAD_DIM
    k5 = kv[:, :kvw].reshape(b, nb, SB_BLOCK, SB_KV_HEADS, SB_HEAD_DIM)
    v5 = kv[:, kvw:].reshape(b, nb, SB_BLOCK, SB_KV_HEADS, SB_HEAD_DIM)
    kt = k5.transpose(0, 3, 1, 4, 2)
    vv = v5.transpose(0, 3, 1, 2, 4)
    oa = _stick_breaking_attention(q.reshape(b, s, -1), kt, vv)
    out = _moe_ple(1, oa.reshape(t, -1), h, mix_w_out[1], moe_norm[1:2], *routers[1], *experts,
                   p_all, ple_norm[1:2], ple_wg[1], ple_wp[1], fgain, True)
    return out.reshape(b, s, d)
```

```python
import functools

import jax
import jax.numpy as jnp
from jax import lax
from jax.experimental import pallas as pl
from jax.experimental.pallas import tpu as pltpu
from jax.experimental.pallas import tpu_sc as plsc

NORM_EPS = 1e-6
LOG2E = 1.4426950408889634
LANES = 128
GDN_HEADS = 8
GDN_HEAD_DIM = 128
GDN_CONV = 4
GDN_CHUNK = 64
GDN_UNROLL = 2
GDN_HALO = 16
SB_Q_HEADS = 16
SB_KV_HEADS = 4
SB_GROUP = SB_Q_HEADS // SB_KV_HEADS
SB_HEAD_DIM = 64
SB_BLOCK = 128
MOE_GROUPS = 4
MOE_EXPERTS_PER_GROUP = 8
MOE_N_EXPERTS = MOE_GROUPS * MOE_EXPERTS_PER_GROUP
MOE_D_EXPERT = 256

VMEM_LIMIT = 56 * 1024 * 1024
TOKEN_TILE = 512
FFN_TILE = 256
GDN_TIME_BLOCK = 512
ROUTE_COLS = 8
SC_GATHER_ROWS = 64
SC_LANES = 16
SC_INDEX_CHUNK = 4096

F32 = jnp.float32
BF16 = jnp.bfloat16


def _params(*sem):
    return pltpu.CompilerParams(dimension_semantics=sem, vmem_limit_bytes=VMEM_LIMIT)


def _dot(a, b):
    return jnp.dot(a.astype(BF16), b.astype(BF16), preferred_element_type=F32)


def _dot_nt(a, b):
    return lax.dot_general(a.astype(BF16), b.astype(BF16), (((1,), (1,)), ((), ())),
                           preferred_element_type=F32)


def _dot_tn(a, b):
    return lax.dot_general(a.astype(BF16), b.astype(BF16), (((0,), (0,)), ((), ())),
                           preferred_element_type=F32)


def _dot_f32(a, b):
    return jnp.dot(a, b, precision=lax.Precision.HIGHEST, preferred_element_type=F32)


def _rms(x, gain):
    return x * lax.rsqrt(jnp.mean(x * x, axis=-1, keepdims=True) + NORM_EPS) * gain


def _silu(x):
    return x * (1.0 / (1.0 + jnp.exp(-x)))


def _sigmoid(x):
    return 1.0 / (1.0 + jnp.exp(-x))


def _softplus(x):
    return jnp.maximum(x, 0.0) + jnp.log(1.0 + jnp.exp(-jnp.abs(x)))


def _pack_halves(x):
    w = x.shape[1] // 2
    hi = pltpu.bitcast(x[:, :w].astype(BF16).astype(F32), jnp.uint32)
    lo = pltpu.bitcast(x[:, w:].astype(BF16).astype(F32), jnp.uint32)
    return (hi & jnp.uint32(0xFFFF0000)) | (lo >> 16)


def _unpack_halves(u):
    hi = pltpu.bitcast(u & jnp.uint32(0xFFFF0000), F32)
    lo = pltpu.bitcast(u << 16, F32)
    return hi, lo


def _sc_gather_rows(table, idx):
    _, w = table.shape
    m = idx.shape[0]
    info = plsc.get_sparse_core_info()
    nc, ns = info.num_cores, info.num_subcores
    rows = SC_GATHER_ROWS
    assert m % (nc * ns * rows * 2) == 0
    per_w = m // (nc * ns)
    n_pairs = per_w // (2 * rows)
    mesh = plsc.VectorSubcoreMesh(core_axis_name="c", subcore_axis_name="s")

    @functools.partial(
        pl.kernel, mesh=mesh,
        out_type=jax.ShapeDtypeStruct((m, w), table.dtype),
        scratch_types=[pltpu.VMEM((2, rows), jnp.int32),
                       pltpu.VMEM((2, rows, w), table.dtype),
                       pltpu.SemaphoreType.DMA((2,)),
                       pltpu.SemaphoreType.DMA((2,))],
    )
    def gather(table_hbm, idx_hbm, out_hbm, idx_v, rows_v, gsem, wsem):
        wid = lax.axis_index("s") * nc + lax.axis_index("c")
        base = wid * per_w

        def fetch(chunk, slot):
            off = pl.multiple_of(base + chunk * rows, rows)
            pltpu.sync_copy(idx_hbm.at[pl.ds(off, rows)], idx_v.at[slot])
            return pltpu.async_copy(table_hbm.at[idx_v.at[slot]], rows_v.at[slot], gsem.at[slot])

        def write(chunk, slot):
            off = pl.multiple_of(base + chunk * rows, rows)
            return pltpu.async_copy(rows_v.at[slot], out_hbm.at[pl.ds(off, rows)], wsem.at[slot])

        @pl.loop(0, n_pairs)
        def _(pi):
            g0 = fetch(2 * pi, 0)
            g1 = fetch(2 * pi + 1, 1)
            g0.wait()
            w0 = write(2 * pi, 0)
            g1.wait()
            w1 = write(2 * pi + 1, 1)
            w0.wait()
            w1.wait()

    return gather(table, idx)


def _sc_invert_slots(dest, n_rows, n_tokens):
    n_pairs = dest.shape[0]
    ch = SC_INDEX_CHUNK
    assert n_pairs % ch == 0 and n_rows % SC_LANES == 0
    nc = plsc.get_sparse_core_info().num_cores
    mesh = plsc.VectorSubcoreMesh(core_axis_name="c", subcore_axis_name="s")

    @functools.partial(
        pl.kernel, mesh=mesh,
        out_type=jax.ShapeDtypeStruct((n_rows,), jnp.int32),
        scratch_types=[pltpu.VMEM((n_rows,), jnp.int32), pltpu.VMEM((ch,), jnp.int32)],
        compiler_params=pltpu.CompilerParams(needs_layout_passes=False),
    )
    def invert(dest_hbm, out_hbm, tab_v, dest_v):
        wid = lax.axis_index("s") * nc + lax.axis_index("c")

        @pl.when(wid == 0)
        def _():
            lanes = lax.iota(jnp.int32, SC_LANES)

            @pl.loop(0, n_rows // SC_LANES)
            def _(i):
                tab_v[pl.ds(pl.multiple_of(i * SC_LANES, SC_LANES), SC_LANES)] = lax.rem(
                    i * SC_LANES + lanes, n_tokens)

            @pl.loop(0, n_pairs // ch)
            def _(c):
                pltpu.sync_copy(dest_hbm.at[pl.ds(pl.multiple_of(c * ch, ch), ch)], dest_v)

                @pl.loop(0, ch // SC_LANES)
                def _(j):
                    d = dest_v[pl.ds(pl.multiple_of(j * SC_LANES, SC_LANES), SC_LANES)]
                    pair = c * ch + j * SC_LANES + lanes
                    plsc.store_scatter(tab_v, [d], lax.rem(pair, n_tokens))

            pltpu.sync_copy(tab_v, out_hbm)

    return invert(dest)


def _norm_matmul_kernel(plan, n_chunk, x_ref, gains_ref, *refs):
    n_w = len(plan)
    w_refs, o_refs = refs[:n_w], refs[n_w:]
    x = x_ref[...]
    inv = lax.rsqrt(jnp.mean(x * x, axis=-1, keepdims=True) + NORM_EPS)
    xn = {}
    for (g, _, _) in plan:
        if g not in xn:
            xn[g] = (x * inv * gains_ref[g:g + 1, :]).astype(BF16)
    for (g, scale, _), w_ref, o_ref in zip(plan, w_refs, o_refs):
        n = w_ref.shape[1]
        for n0 in range(0, n, n_chunk):
            n1 = min(n, n0 + n_chunk)
            acc = jnp.dot(xn[g], w_ref[:, n0:n1], preferred_element_type=F32)
            if scale != 1.0:
                acc = acc * scale
            o_ref[:, n0:n1] = acc.astype(o_ref.dtype)


def _norm_matmul(x, gains, ws, plan):
    t, d = x.shape
    tm = TOKEN_TILE
    in_specs = [pl.BlockSpec((tm, d), lambda i: (i, 0)),
                pl.BlockSpec(gains.shape, lambda i: (0, 0))]
    in_specs += [pl.BlockSpec(w.shape, lambda i: (0, 0)) for w in ws]
    out_specs = [pl.BlockSpec((tm, w.shape[1]), lambda i: (i, 0)) for w in ws]
    out_shape = [jax.ShapeDtypeStruct((t, w.shape[1]), p[2]) for w, p in zip(ws, plan)]
    return pl.pallas_call(
        functools.partial(_norm_matmul_kernel, tuple(plan), 512),
        grid=(t // tm,), in_specs=in_specs, out_specs=out_specs, out_shape=out_shape,
        compiler_params=_params("parallel"), name="norm_matmul",
    )(x, gains, *ws)


def _gdn_kernel(n_heads, ts, qkv_ref, z_ref, ab_ref, conv_ref, alog_ref, dtb_ref, onorm_ref,
                o_ref, xpad_ref, state_ref, g_ref, beta_ref):
    c = GDN_CHUNK
    dk = GDN_HEAD_DIM
    tb = pl.program_id(1)
    n_ch = ts // c
    qkv_dim = qkv_ref.shape[-1]

    @pl.when(tb == 0)
    def _():
        state_ref[...] = jnp.zeros_like(state_ref)
        xpad_ref[0:GDN_HALO, :] = jnp.zeros((GDN_HALO, qkv_dim), xpad_ref.dtype)

    xpad_ref[GDN_HALO:GDN_HALO + ts, :] = qkv_ref[0]

    ab = ab_ref[0]
    g_all = -jnp.exp(alog_ref[...]) * _softplus(ab + dtb_ref[...])
    beta_all = _sigmoid(ab)
    rin = lax.broadcasted_iota(jnp.int32, (ts, LANES), 0) & (c - 1)
    for h in range(n_heads):
        gh = jnp.broadcast_to(g_all[:, h:h + 1], (ts, LANES))
        shift = 1
        while shift < c:
            gh = gh + jnp.where(rin >= shift, pltpu.roll(gh, shift, 0), 0.0)
            shift *= 2
        g_ref[h] = gh
        beta_ref[h] = jnp.broadcast_to(beta_all[:, n_heads + h:n_heads + h + 1], (ts, LANES))

    row = lax.broadcasted_iota(jnp.int32, (c, c), 0)
    col = lax.broadcasted_iota(jnp.int32, (c, c), 1)
    incl = row >= col
    strict = row > col

    srow = lax.broadcasted_iota(jnp.int32, (GDN_CONV * c, c + GDN_HALO), 0)
    scol = lax.broadcasted_iota(jnp.int32, (GDN_CONV * c, c + GDN_HALO), 1)
    shift_sel = jnp.where(scol == (srow & (c - 1)) + (srow >> (c.bit_length() - 1)) + (GDN_HALO - GDN_CONV + 1),
                          1.0, 0.0).astype(BF16)

    def conv_silu(r0, c0):
        win = xpad_ref[pl.ds(r0, c + GDN_HALO), c0:c0 + LANES]
        taps = jnp.dot(shift_sel, win, preferred_element_type=F32)
        acc = taps[0:c] * conv_ref[0:1, c0:c0 + LANES]
        for j in range(1, GDN_CONV):
            acc = acc + taps[j * c:(j + 1) * c] * conv_ref[j:j + 1, c0:c0 + LANES]
        return _silu(acc)

    def l2n(t):
        return t * lax.rsqrt(jnp.sum(t * t, axis=-1, keepdims=True) + NORM_EPS)

    heads = range(n_heads)
    eye = (row == col).astype(F32)
    same_blk = (row >> 4) == (col >> 4)

    items = range(GDN_UNROLL * n_heads)

    def chunk_body(n, carry):
        r_base = n * (GDN_UNROLL * c)
        r0 = [pl.multiple_of(r_base + (i // n_heads) * c, c) for i in items]
        hd = [i % n_heads for i in items]
        q = [l2n(conv_silu(r0[i], hd[i] * dk)) * (dk ** -0.5) for i in items]
        k = [l2n(conv_silu(r0[i], (n_heads + hd[i]) * dk)) for i in items]
        v = [conv_silu(r0[i], (2 * n_heads + hd[i]) * dk) for i in items]
        gc = [g_ref[hd[i], pl.ds(r0[i], c), :] for i in items]
        beta = [beta_ref[hd[i], pl.ds(r0[i], c), :] for i in items]
        kb = [k[i] * beta[i] for i in items]
        kk = [_dot_nt(kb[i], k[i]) for i in items]
        qk = [_dot_nt(q[i], k[i]) for i in items]
        decay = []
        for i in items:
            diff = gc[i][:, 0:c] - gc[i].T[0:c, :]
            decay.append(jnp.where(incl, jnp.exp(jnp.where(incl, diff, 0.0)), 0.0))
        lm = [jnp.where(strict, kk[i] * decay[i], 0.0) for i in items]
        attn = [jnp.where(incl, qk[i] * decay[i], 0.0) for i in items]
        ld = [jnp.where(same_blk, lm[i], 0.0) for i in items]
        lo = [lm[i] - ld[i] for i in items]
        p = [eye - ld[i] for i in items]
        sq = [_dot(ld[i], ld[i]) for i in items]
        for _ in range(2):
            pn = [_dot(p[i], sq[i]) for i in items]
            sq2 = [_dot(sq[i], sq[i]) for i in items]
            p = [p[i] + pn[i] for i in items]
            sq = sq2
        dinv = [p[i] + _dot(p[i], sq[i]) for i in items]
        m = [_dot(dinv[i], lo[i]) for i in items]
        m2 = [_dot(m[i], m[i]) for i in items]
        r = [(eye - m[i]) + _dot(eye - m[i], m2[i]) for i in items]
        tmat = [_dot(r[i], dinv[i]) for i in items]
        eg = [jnp.exp(gc[i]) for i in items]
        rhs = [jnp.concatenate([v[i] * beta[i], kb[i] * eg[i]], axis=1) for i in items]
        uw = [_dot(tmat[i], rhs[i]) for i in items]
        lhs = [jnp.concatenate([uw[i][:, dk:], q[i] * eg[i]], axis=0) for i in items]
        g_last = [gc[i][c - 1:c, :] for i in items]
        k_dec = [k[i] * jnp.exp(g_last[i] - gc[i]) for i in items]
        s = [state_ref[h] for h in heads]
        for ci in range(GDN_UNROLL):
            it = [ci * n_heads + h for h in heads]
            ws_qs = [_dot(lhs[it[h]], s[h]) for h in heads]
            v_new = [uw[it[h]][:, :dk] - ws_qs[h][:c] for h in heads]
            av = [_dot(attn[it[h]], v_new[h]) for h in heads]
            kv = [_dot_tn(k_dec[it[h]], v_new[h]) for h in heads]
            s = [s[h] * jnp.exp(g_last[it[h]]) + kv[h] for h in heads]
            for h in heads:
                o = ws_qs[h][c:] + av[h]
                o = o * lax.rsqrt(jnp.mean(o * o, axis=-1, keepdims=True) + NORM_EPS) * onorm_ref[...]
                zz = z_ref[0, pl.ds(r0[it[h]], c), h * dk:(h + 1) * dk].astype(F32)
                o_ref[0, pl.ds(r0[it[h]], c), h * dk:(h + 1) * dk] = (o * _silu(zz)).astype(o_ref.dtype)
        for h in heads:
            state_ref[h] = s[h]
        return carry

    lax.fori_loop(0, n_ch // GDN_UNROLL, chunk_body, 0)
    xpad_ref[0:GDN_HALO, :] = xpad_ref[ts:ts + GDN_HALO, :]


def _gated_deltanet_core(qkv, z, ab, conv_w, alog, dtb, onorm):
    b, s, qkv_dim = qkv.shape
    n_heads = GDN_HEADS
    ts = min(GDN_TIME_BLOCK, s)
    vd = z.shape[-1]
    kern = functools.partial(_gdn_kernel, n_heads, ts)
    return pl.pallas_call(
        kern,
        grid=(b, s // ts),
        in_specs=[
            pl.BlockSpec((1, ts, qkv_dim), lambda i, j: (i, j, 0)),
            pl.BlockSpec((1, ts, vd), lambda i, j: (i, j, 0)),
            pl.BlockSpec((1, ts, LANES), lambda i, j: (i, j, 0)),
            pl.BlockSpec(conv_w.shape, lambda i, j: (0, 0)),
            pl.BlockSpec((1, LANES), lambda i, j: (0, 0)),
            pl.BlockSpec((1, LANES), lambda i, j: (0, 0)),
            pl.BlockSpec((1, GDN_HEAD_DIM), lambda i, j: (0, 0)),
        ],
        out_specs=pl.BlockSpec((1, ts, vd), lambda i, j: (i, j, 0)),
        out_shape=jax.ShapeDtypeStruct((b, s, vd), BF16),
        scratch_shapes=[
            pltpu.VMEM((ts + GDN_HALO, qkv_dim), BF16),
            pltpu.VMEM((n_heads, GDN_HEAD_DIM, GDN_HEAD_DIM), F32),
            pltpu.VMEM((n_heads, ts, LANES), F32),
            pltpu.VMEM((n_heads, ts, LANES), F32),
        ],
        compiler_params=_params("parallel", "arbitrary"), name="gdn_delta_rule",
    )(qkv, z, ab, conv_w, alog, dtb, onorm)


def _sb_attn_kernel(q_ref, kt_ref, v_ref, o_ref, acc_ref, cs_ref, z_ref, p_ref):
    blk = SB_BLOCK
    dh = SB_HEAD_DIM
    grp = SB_GROUP
    i = pl.program_id(2)
    rows = grp * blk
    n_pairs = 1 + i // 2
    qb = q_ref[0]
    q4 = jnp.concatenate([qb[:, g * dh:(g + 1) * dh] for g in range(grp)], axis=0)
    krow = lax.broadcasted_iota(jnp.int32, (blk, blk), 0)
    kcol = lax.broadcasted_iota(jnp.int32, (blk, blk), 1)
    neg_suffix = jnp.where(krow >= kcol, -1.0, 0.0).astype(BF16)

    def pair_blocks(k):
        ja = i - 2 * k
        return jnp.maximum(ja, 0), jnp.maximum(ja - 1, 0), (ja >= 1).astype(F32)

    def logits(k):
        ja, jb, _ = pair_blocks(k)
        z_ref[0] = jnp.dot(q4, kt_ref[0, 0, ja], preferred_element_type=F32)
        z_ref[1] = jnp.dot(q4, kt_ref[0, 0, jb], preferred_element_type=F32)

    def weights(mask_a):
        masks = [mask_a, None]
        zs = [z_ref[0], z_ref[1]]
        sps = []
        for z, mask in zip(zs, masks):
            sp = jnp.maximum(z, 0.0) + jnp.log(1.0 + jnp.exp2(jnp.abs(z) * -LOG2E))
            if mask is not None:
                sp = jnp.where(mask, sp, 0.0)
            sps.append(sp)
        sufs = [jnp.dot(sp.astype(BF16), neg_suffix, preferred_element_type=F32) for sp in sps]
        rss = [jnp.sum(sp, axis=-1, keepdims=True) for sp in sps]
        cs = cs_ref[...]
        for n, (z, suf, rs, mask) in enumerate(zip(zs, sufs, rss, masks)):
            p = jnp.exp2(((z + suf) - cs) * LOG2E)
            if mask is not None:
                p = jnp.where(mask, p, 0.0)
            p_ref[n] = p.astype(BF16)
            cs = cs + rs
        cs_ref[...] = cs

    def weighted_values(k):
        ja, jb, valid_b = pair_blocks(k)
        vb = (v_ref[0, 0, jb].astype(F32) * valid_b).astype(BF16)
        acc_ref[...] += (jnp.dot(p_ref[0], v_ref[0, 0, ja], preferred_element_type=F32)
                         + jnp.dot(p_ref[1], vb, preferred_element_type=F32))

    acc_ref[...] = jnp.zeros_like(acc_ref)
    cs_ref[...] = jnp.zeros_like(cs_ref)
    t_in = lax.broadcasted_iota(jnp.int32, (rows, blk), 0) & (blk - 1)
    s_in = lax.broadcasted_iota(jnp.int32, (rows, blk), 1)
    logits(0)
    weights(s_in < t_in)
    logits(1)

    def body(k, c):
        weighted_values(k - 1)
        weights(None)
        logits(k + 1)
        return c

    lax.fori_loop(1, n_pairs, body, 0)
    weighted_values(n_pairs - 1)
    acc = acc_ref[...]
    for g in range(grp):
        o_ref[0, :, g * dh:(g + 1) * dh] = acc[g * blk:(g + 1) * blk, :].astype(o_ref.dtype)


def _stick_breaking_attention(q, kt, v):
    b, s, qd = q.shape
    nb = s // SB_BLOCK
    gw = SB_GROUP * SB_HEAD_DIM
    rows = SB_GROUP * SB_BLOCK
    return pl.pallas_call(
        _sb_attn_kernel,
        grid=(b, SB_KV_HEADS, nb),
        in_specs=[
            pl.BlockSpec((1, SB_BLOCK, gw), lambda bi, g, i: (bi, i, g)),
            pl.BlockSpec((1, 1, nb, SB_HEAD_DIM, SB_BLOCK), lambda bi, g, i: (bi, g, 0, 0, 0)),
            pl.BlockSpec((1, 1, nb, SB_BLOCK, SB_HEAD_DIM), lambda bi, g, i: (bi, g, 0, 0, 0)),
        ],
        out_specs=pl.BlockSpec((1, SB_BLOCK, gw), lambda bi, g, i: (bi, i, g)),
        out_shape=jax.ShapeDtypeStruct((b, s, qd), BF16),
        scratch_shapes=[pltpu.VMEM((rows, SB_HEAD_DIM), F32), pltpu.VMEM((rows, LANES), F32),
                        pltpu.VMEM((2, rows, SB_BLOCK), F32), pltpu.VMEM((2, rows, SB_BLOCK), BF16)],
        compiler_params=_params("parallel", "parallel", "arbitrary"), name="stick_breaking_attention",
    )(q, kt, v)


def _out_router_kernel(mix_ref, h_ref, wout_ref, gain_ref, wrh_ref, wrl_ref, br_ref,
                       h1_ref, xn_ref, rw_ref, ri_ref, cnt_ref, run_ref):
    h1 = h_ref[...] + jnp.dot(mix_ref[...], wout_ref[...], preferred_element_type=F32)
    h1_ref[...] = h1
    xn = _rms(h1, gain_ref[...])
    xh = xn.astype(BF16)
    xn_ref[...] = _pack_halves(xn)
    xl = (xn - xh.astype(F32)).astype(BF16)
    logits = (jnp.dot(xh, wrh_ref[...], preferred_element_type=F32)
              + jnp.dot(xl, wrh_ref[...], preferred_element_type=F32)
              + jnp.dot(xh, wrl_ref[...], preferred_element_type=F32)
              + br_ref[...])
    lane = lax.broadcasted_iota(jnp.int32, logits.shape, 1).astype(F32)
    neg = jnp.float32(-jnp.inf)
    big = jnp.float32(1e9)
    gl = jnp.where(lane < MOE_GROUPS, logits, neg)
    gmax = jnp.max(gl, axis=-1, keepdims=True)
    gidx = jnp.min(jnp.where(gl == gmax, lane, big), axis=-1, keepdims=True)
    gp = 1.0 / jnp.sum(jnp.exp(gl - gmax), axis=-1, keepdims=True)
    lo = MOE_GROUPS + gidx * MOE_EXPERTS_PER_GROUP
    el = jnp.where((lane >= lo) & (lane < lo + MOE_EXPERTS_PER_GROUP), logits, neg)
    m1 = jnp.max(el, axis=-1, keepdims=True)
    i1 = jnp.min(jnp.where(el == m1, lane, big), axis=-1, keepdims=True)
    el2 = jnp.where(lane == i1, neg, el)
    m2 = jnp.max(el2, axis=-1, keepdims=True)
    i2 = jnp.min(jnp.where(el2 == m2, lane, big), axis=-1, keepdims=True)
    e2 = jnp.exp(m2 - m1)
    w1 = gp / (1.0 + e2)
    w2 = gp * e2 / (1.0 + e2)
    rw_ref[...] = jnp.where(lane == 0, w1, jnp.where(lane == 1, w2, 0.0))

    @pl.when(pl.program_id(0) == 0)
    def _():
        run_ref[...] = jnp.zeros_like(run_ref)

    tm = logits.shape[0]
    trow = lax.broadcasted_iota(jnp.int32, (tm, tm), 0)
    tcol = lax.broadcasted_iota(jnp.int32, (tm, tm), 1)
    before = jnp.where(tcol < trow, 1.0, 0.0).astype(BF16)
    hot1 = lane == i1
    hot2 = lane == i2
    oh1 = jnp.where(hot1, 1.0, 0.0)
    oh2 = jnp.where(hot2, 1.0, 0.0)
    prior1 = jnp.dot(before, oh1.astype(BF16), preferred_element_type=F32)
    prior2 = jnp.dot(before, oh2.astype(BF16), preferred_element_type=F32)
    cnt1 = jnp.sum(oh1, axis=0, keepdims=True)
    cnt2 = jnp.sum(oh2, axis=0, keepdims=True)
    run = run_ref[...]
    rank1 = jnp.sum(jnp.where(hot1, prior1 + run, 0.0), axis=-1, keepdims=True)
    rank2 = jnp.sum(jnp.where(hot2, prior2 + (run + cnt1), 0.0), axis=-1, keepdims=True)
    run = run + cnt1 + cnt2
    run_ref[...] = run
    cnt_ref[...] = run
    ri = jnp.where(lane == 0, i1 - MOE_GROUPS, jnp.where(lane == 1, i2 - MOE_GROUPS,
                   jnp.where(lane == 2, rank1, jnp.where(lane == 3, rank2, 0.0))))
    ri_ref[...] = ri.T[0:ROUTE_COLS, :].astype(jnp.int32)


def _out_router(mix, h, w_out, gain, w_router, b_router):
    t, d = h.shape
    tm = TOKEN_TILE
    kd = mix.shape[1]
    row = lambda i: (i, 0)
    fix = lambda i: (0, 0)
    wr_hi = w_router.astype(BF16)
    wr_lo = (w_router - wr_hi.astype(F32)).astype(BF16)
    return pl.pallas_call(
        _out_router_kernel,
        grid=(t // tm,),
        in_specs=[pl.BlockSpec((tm, kd), row), pl.BlockSpec((tm, d), row),
                  pl.BlockSpec(w_out.shape, fix), pl.BlockSpec((1, d), fix),
                  pl.BlockSpec(w_router.shape, fix), pl.BlockSpec(w_router.shape, fix),
                  pl.BlockSpec((1, LANES), fix)],
        out_specs=[pl.BlockSpec((tm, d), row), pl.BlockSpec((tm, d // 2), row),
                   pl.BlockSpec((tm, LANES), row), pl.BlockSpec((ROUTE_COLS, tm), lambda i: (0, i)),
                   pl.BlockSpec((1, LANES), fix)],
        out_shape=[jax.ShapeDtypeStruct((t, d), F32), jax.ShapeDtypeStruct((t, d // 2), jnp.uint32),
                   jax.ShapeDtypeStruct((t, LANES), F32), jax.ShapeDtypeStruct((ROUTE_COLS, t), jnp.int32),
                   jax.ShapeDtypeStruct((1, LANES), F32)],
        scratch_shapes=[pltpu.VMEM((1, LANES), F32)],
        compiler_params=_params("arbitrary"), name="out_proj_router",
    )(mix, h, w_out, gain, wr_hi, wr_lo, b_router)


def _expert_ffn_kernel(te_ref, nu_ref, xs_ref, wg_ref, wu_ref, wd_ref, ys_ref):
    i = pl.program_id(0)

    @pl.when(i < nu_ref[0])
    def _():
        xa, xb = _unpack_halves(xs_ref[...])
        xa = xa.astype(BF16)
        xb = xb.astype(BF16)
        half = xa.shape[1]
        wg = wg_ref[0].astype(BF16)
        wu = wu_ref[0].astype(BF16)
        hg = (jnp.dot(xa, wg[:half], preferred_element_type=F32)
              + jnp.dot(xb, wg[half:], preferred_element_type=F32))
        hu = (jnp.dot(xa, wu[:half], preferred_element_type=F32)
              + jnp.dot(xb, wu[half:], preferred_element_type=F32))
        hh = _silu(hg) * hu
        ys_ref[...] = _pack_halves(jnp.dot(hh.astype(BF16), wd_ref[0].astype(BF16),
                                           preferred_element_type=F32))

    @pl.when(i >= nu_ref[0])
    def _():
        ys_ref[...] = jnp.zeros_like(ys_ref)


def _expert_ffn(tile_expert, n_used, xs, w_gate, w_up, w_down):
    r, dw = xs.shape
    d = 2 * dw
    tm = FFN_TILE
    f = w_gate.shape[-1]
    grid_spec = pltpu.PrefetchScalarGridSpec(
        num_scalar_prefetch=2,
        grid=(r // tm,),
        in_specs=[
            pl.BlockSpec((tm, dw), lambda i, te, nu: (i, 0)),
            pl.BlockSpec((1, d, f), lambda i, te, nu: (te[i], 0, 0)),
            pl.BlockSpec((1, d, f), lambda i, te, nu: (te[i], 0, 0)),
            pl.BlockSpec((1, f, d), lambda i, te, nu: (te[i], 0, 0)),
        ],
        out_specs=pl.BlockSpec((tm, dw), lambda i, te, nu: (i, 0)),
    )
    return pl.pallas_call(
        _expert_ffn_kernel, grid_spec=grid_spec,
        out_shape=jax.ShapeDtypeStruct((r, dw), jnp.uint32),
        compiler_params=_params("arbitrary"), name="expert_ffn",
    )(tile_expert, n_used, xs, w_gate, w_up, w_down)


def _ple_kernel(final, h_ref, m0_ref, m1_ref, rw_ref, p_ref, gain_ref, wg_ref, wp_ref, fgain_ref, o_ref):
    rw = rw_ref[...]
    a0, b0 = _unpack_halves(m0_ref[...])
    a1, b1 = _unpack_halves(m1_ref[...])
    w0 = rw[:, 0:1]
    w1 = rw[:, 1:2]
    moe = jnp.concatenate([w0 * a0 + w1 * a1, w0 * b0 + w1 * b1], axis=1)
    h2 = h_ref[...] + moe
    xn = _rms(h2, gain_ref[...]).astype(BF16)
    gate = _sigmoid(jnp.dot(xn, wg_ref[...], preferred_element_type=F32))
    emb = jnp.dot(p_ref[...].astype(BF16), wp_ref[...], preferred_element_type=F32)
    h3 = h2 + gate * emb
    if final:
        h3 = _rms(h3, fgain_ref[...])
    o_ref[...] = h3


def _ple(h, m01, rw, p, layer, gain, w_gate, w_proj, final_gain, final):
    t, d = h.shape
    tm = TOKEN_TILE
    pd = p.shape[1]
    nt = t // tm
    row = lambda i: (i, 0)
    fix = lambda i: (0, 0)
    return pl.pallas_call(
        functools.partial(_ple_kernel, final),
        grid=(nt,),
        in_specs=[pl.BlockSpec((tm, d), row), pl.BlockSpec((tm, d // 2), row),
                  pl.BlockSpec((tm, d // 2), lambda i: (i + nt, 0)),
                  pl.BlockSpec((tm, LANES), row), pl.BlockSpec((tm, pd), lambda i: (i + layer * nt, 0)),
                  pl.BlockSpec((1, d), fix),
                  pl.BlockSpec(w_gate.shape, fix), pl.BlockSpec(w_proj.shape, fix),
                  pl.BlockSpec((1, d), fix)],
        out_specs=pl.BlockSpec((tm, d), row),
        out_shape=jax.ShapeDtypeStruct((t, d), F32),
        compiler_params=_params("parallel"), name="moe_residual_ple",
    )(h, m01, m01, rw, p, gain, w_gate, w_proj, final_gain)


def _routing_tables(ri, counts):
    t = ri.shape[1]
    tm = FFN_TILE
    n_rows = 2 * t + MOE_N_EXPERTS * tm
    padded = ((counts + tm - 1) // tm) * tm
    ends = jnp.cumsum(padded)
    starts = ends - padded
    ids = ri[0:2]
    offs = jnp.zeros_like(ids)
    for e in range(MOE_N_EXPERTS):
        offs = jnp.where(ids == e, starts[e], offs)
    dest = (ri[2:4] + offs).reshape(-1)
    src_tok = _sc_invert_slots(dest, n_rows, t)
    tile_start = jnp.arange(n_rows // tm, dtype=jnp.int32) * tm
    tile_expert = jnp.minimum(jnp.sum((ends[None, :] <= tile_start[:, None]).astype(jnp.int32), axis=1),
                              MOE_N_EXPERTS - 1)
    n_used = (ends[-1] // tm).astype(jnp.int32).reshape(1)
    return src_tok, tile_expert, n_used, dest


def _moe_ple(layer, mix, h, w_out, moe_gain, w_router, b_router, w_gate, w_up, w_down,
             p, ple_gain, ple_w_gate, ple_w_proj, final_gain, final):
    h1, xn, rw, ri, cnt = _out_router(mix, h, w_out, moe_gain, w_router, b_router)
    counts = cnt[0, MOE_GROUPS:MOE_GROUPS + MOE_N_EXPERTS].astype(jnp.int32)
    src_tok, tile_expert, n_used, dest = _routing_tables(ri, counts)
    xs = _sc_gather_rows(xn, src_tok)
    ys = _expert_ffn(tile_expert + layer * MOE_N_EXPERTS, n_used, xs, w_gate, w_up, w_down)
    m01 = _sc_gather_rows(ys, dest)
    return _ple(h1, m01, rw, p, layer, ple_gain, ple_w_gate, ple_w_proj, final_gain, final)


def _pad_lanes(v):
    return jnp.zeros((1, LANES), F32).at[0, :v.shape[0]].set(v.astype(F32))


def kernel(x, p, attn_norm, moe_norm, ple_norm, gdn_w_in, gdn_conv, gdn_a_log, gdn_dt_bias, gdn_o_norm, gdn_w_out, kv_norm, w_kv, sb_w_q, sb_w_out, moe_w_group, moe_b_group, moe_w_expert, moe_b_expert, moe_w_gate, moe_w_up, moe_w_down, ple_w_gate, ple_w_proj, final_norm):
    d = x.shape[-1]
    kd = GDN_HEADS * GDN_HEAD_DIM
    qkv_dim = 3 * kd

    def router_params(i):
        wr = jnp.zeros((d, LANES), F32)
        wr = wr.at[:, :MOE_GROUPS].set(moe_w_group[i])
        wr = wr.at[:, MOE_GROUPS:MOE_GROUPS + MOE_N_EXPERTS].set(moe_w_expert[i])
        br = jnp.zeros((1, LANES), F32)
        br = br.at[0, :MOE_GROUPS].set(moe_b_group[i])
        br = br.at[0, MOE_GROUPS:MOE_GROUPS + MOE_N_EXPERTS].set(moe_b_expert[i])
        return wr, br

    experts = (moe_w_gate.reshape(-1, d, MOE_D_EXPERT), moe_w_up.reshape(-1, d, MOE_D_EXPERT),
               moe_w_down.reshape(-1, MOE_D_EXPERT, d))

    w_in = gdn_w_in[0]
    w_ab = jnp.zeros((d, LANES), F32).at[:, :2 * GDN_HEADS].set(w_in[:, qkv_dim + kd:])
    w_in_parts = [w_in[:, :qkv_dim].astype(BF16), w_in[:, qkv_dim:qkv_dim + kd].astype(BF16), w_ab.astype(BF16)]
    dtb = jnp.zeros((1, LANES), F32).at[0, :GDN_HEADS].set(gdn_dt_bias[0])
    alog = _pad_lanes(gdn_a_log[0])
    routers = [router_params(0), router_params(1)]
    mix_w_out = [gdn_w_out[0].astype(BF16), sb_w_out[0].astype(BF16)]
    ple_wg = [ple_w_gate[0].astype(BF16), ple_w_gate[1].astype(BF16)]
    ple_wp = [ple_w_proj[0].astype(BF16), ple_w_proj[1].astype(BF16)]
    gains1 = jnp.stack([attn_norm[1], kv_norm], axis=0)
    w_q = sb_w_q[0].astype(BF16)
    w_kv_b = w_kv.astype(BF16)
    fgain = final_norm.reshape(1, d)

    b, s, _ = x.shape
    t = b * s
    h = x.reshape(t, d)
    p_all = p.reshape(-1, p.shape[-1])
    qkv, z, ab = _norm_matmul(h, attn_norm[0:1], w_in_parts,
                              [(0, 1.0, BF16), (0, 1.0, BF16), (0, 1.0, F32)])
    og = _gated_deltanet_core(
        qkv.reshape(b, s, qkv_dim), z.reshape(b, s, kd), ab.reshape(b, s, LANES),
        gdn_conv[0], alog, dtb, gdn_o_norm[0].reshape(1, GDN_HEAD_DIM))
    h = _moe_ple(0, og.reshape(t, kd), h, mix_w_out[0], moe_norm[0:1], *routers[0], *experts,
                 p_all, ple_norm[0:1], ple_wg[0], ple_wp[0], fgain, False)
    q, kv = _norm_matmul(h, gains1, [w_q, w_kv_b],
                         [(0, SB_HEAD_DIM ** -0.5, BF16), (1, 1.0, BF16)])
    nb = s // SB_BLOCK
    kvw = SB_KV_HEADS * SB_HEAD_DIM
    k5 = kv[:, :kvw].reshape(b, nb, SB_BLOCK, SB_KV_HEADS, SB_HEAD_DIM)
    v5 = kv[:, kvw:].reshape(b, nb, SB_BLOCK, SB_KV_HEADS, SB_HEAD_DIM)
    kt = k5.transpose(0, 3, 1, 4, 2)
    vv = v5.transpose(0, 3, 1, 2, 4)
    oa = _stick_breaking_attention(q.reshape(b, s, -1), kt, vv)
    out = _moe_ple(1, oa.reshape(t, -1), h, mix_w_out[1], moe_norm[1:2], *routers[1], *experts,
                   p_all, ple_norm[1:2], ple_wg[1], ple_wp[1], fgain, True)
    return out.reshape(b, s, d)
```

```python
import functools

import jax
import jax.numpy as jnp
from jax import lax
from jax.experimental import pallas as pl
from jax.experimental.pallas import tpu as pltpu
from jax.experimental.pallas import tpu_sc as plsc

NORM_EPS = 1e-6
LOG2E = 1.4426950408889634
LANES = 128
GDN_HEADS = 8
GDN_HEAD_DIM = 128
GDN_CONV = 4
GDN_CHUNK = 64
GDN_UNROLL = 4
GDN_HALO = 16
SB_Q_HEADS = 16
SB_KV_HEADS = 4
SB_GROUP = SB_Q_HEADS // SB_KV_HEADS
SB_HEAD_DIM = 64
SB_BLOCK = 128
MOE_GROUPS = 4
MOE_EXPERTS_PER_GROUP = 8
MOE_N_EXPERTS = MOE_GROUPS * MOE_EXPERTS_PER_GROUP
MOE_D_EXPERT = 256

VMEM_LIMIT = 56 * 1024 * 1024
TOKEN_TILE = 512
FFN_TILE = 512
GDN_TIME_BLOCK = 512
ROUTE_COLS = 8
SC_GATHER_ROWS = 64
SC_LANES = 16
SC_INDEX_CHUNK = 4096

F32 = jnp.float32
BF16 = jnp.bfloat16


def _params(*sem):
    return pltpu.CompilerParams(dimension_semantics=sem, vmem_limit_bytes=VMEM_LIMIT)


def _dot(a, b):
    return jnp.dot(a.astype(BF16), b.astype(BF16), preferred_element_type=F32)


def _dot_nt(a, b):
    return lax.dot_general(a.astype(BF16), b.astype(BF16), (((1,), (1,)), ((), ())),
                           preferred_element_type=F32)


def _dot_tn(a, b):
    return lax.dot_general(a.astype(BF16), b.astype(BF16), (((0,), (0,)), ((), ())),
                           preferred_element_type=F32)


def _dot_f32(a, b):
    return jnp.dot(a, b, precision=lax.Precision.HIGHEST, preferred_element_type=F32)


def _rms(x, gain):
    return x * lax.rsqrt(jnp.mean(x * x, axis=-1, keepdims=True) + NORM_EPS) * gain


def _silu(x):
    return x * (1.0 / (1.0 + jnp.exp(-x)))


def _sigmoid(x):
    return 1.0 / (1.0 + jnp.exp(-x))


def _softplus(x):
    return jnp.maximum(x, 0.0) + jnp.log(1.0 + jnp.exp(-jnp.abs(x)))


def _pack_halves(x):
    w = x.shape[1] // 2
    hi = pltpu.bitcast(x[:, :w].astype(BF16).astype(F32), jnp.uint32)
    lo = pltpu.bitcast(x[:, w:].astype(BF16).astype(F32), jnp.uint32)
    return (hi & jnp.uint32(0xFFFF0000)) | (lo >> 16)


def _unpack_halves(u):
    hi = pltpu.bitcast(u & jnp.uint32(0xFFFF0000), F32)
    lo = pltpu.bitcast(u << 16, F32)
    return hi, lo


def _sc_gather_rows(table, idx):
    _, w = table.shape
    m = idx.shape[0]
    info = plsc.get_sparse_core_info()
    nc, ns = info.num_cores, info.num_subcores
    rows = SC_GATHER_ROWS
    assert m % (nc * ns * rows * 2) == 0
    per_w = m // (nc * ns)
    n_pairs = per_w // (2 * rows)
    mesh = plsc.VectorSubcoreMesh(core_axis_name="c", subcore_axis_name="s")

    @functools.partial(
        pl.kernel, mesh=mesh,
        out_type=jax.ShapeDtypeStruct((m, w), table.dtype),
        scratch_types=[pltpu.VMEM((2, rows), jnp.int32),
                       pltpu.VMEM((2, rows, w), table.dtype),
                       pltpu.SemaphoreType.DMA((2,)),
                       pltpu.SemaphoreType.DMA((2,))],
    )
    def gather(table_hbm, idx_hbm, out_hbm, idx_v, rows_v, gsem, wsem):
        wid = lax.axis_index("s") * nc + lax.axis_index("c")
        base = wid * per_w

        def fetch(chunk, slot):
            off = pl.multiple_of(base + chunk * rows, rows)
            pltpu.sync_copy(idx_hbm.at[pl.ds(off, rows)], idx_v.at[slot])
            return pltpu.async_copy(table_hbm.at[idx_v.at[slot]], rows_v.at[slot], gsem.at[slot])

        def write(chunk, slot):
            off = pl.multiple_of(base + chunk * rows, rows)
            return pltpu.async_copy(rows_v.at[slot], out_hbm.at[pl.ds(off, rows)], wsem.at[slot])

        @pl.loop(0, n_pairs)
        def _(pi):
            g0 = fetch(2 * pi, 0)
            g1 = fetch(2 * pi + 1, 1)
            g0.wait()
            w0 = write(2 * pi, 0)
            g1.wait()
            w1 = write(2 * pi + 1, 1)
            w0.wait()
            w1.wait()

    return gather(table, idx)


def _sc_invert_slots(dest, n_rows, n_tokens):
    n_pairs = dest.shape[0]
    ch = SC_INDEX_CHUNK
    assert n_pairs % ch == 0 and n_rows % SC_LANES == 0
    nc = plsc.get_sparse_core_info().num_cores
    mesh = plsc.VectorSubcoreMesh(core_axis_name="c", subcore_axis_name="s")

    @functools.partial(
        pl.kernel, mesh=mesh,
        out_type=jax.ShapeDtypeStruct((n_rows,), jnp.int32),
        scratch_types=[pltpu.VMEM((n_rows,), jnp.int32), pltpu.VMEM((ch,), jnp.int32)],
        compiler_params=pltpu.CompilerParams(needs_layout_passes=False),
    )
    def invert(dest_hbm, out_hbm, tab_v, dest_v):
        wid = lax.axis_index("s") * nc + lax.axis_index("c")

        @pl.when(wid == 0)
        def _():
            lanes = lax.iota(jnp.int32, SC_LANES)

            @pl.loop(0, n_rows // SC_LANES)
            def _(i):
                tab_v[pl.ds(pl.multiple_of(i * SC_LANES, SC_LANES), SC_LANES)] = lax.rem(
                    i * SC_LANES + lanes, n_tokens)

            @pl.loop(0, n_pairs // ch)
            def _(c):
                pltpu.sync_copy(dest_hbm.at[pl.ds(pl.multiple_of(c * ch, ch), ch)], dest_v)

                @pl.loop(0, ch // SC_LANES)
                def _(j):
                    d = dest_v[pl.ds(pl.multiple_of(j * SC_LANES, SC_LANES), SC_LANES)]
                    pair = c * ch + j * SC_LANES + lanes
                    plsc.store_scatter(tab_v, [d], lax.rem(pair, n_tokens))

            pltpu.sync_copy(tab_v, out_hbm)

    return invert(dest)


def _norm_matmul_kernel(plan, n_chunk, x_ref, gains_ref, *refs):
    n_w = len(plan)
    w_refs, o_refs = refs[:n_w], refs[n_w:]
    x = x_ref[...]
    inv = lax.rsqrt(jnp.mean(x * x, axis=-1, keepdims=True) + NORM_EPS)
    xn = {}
    for (g, _, _) in plan:
        if g not in xn:
            xn[g] = (x * inv * gains_ref[g:g + 1, :]).astype(BF16)
    for (g, scale, _), w_ref, o_ref in zip(plan, w_refs, o_refs):
        n = w_ref.shape[1]
        for n0 in range(0, n, n_chunk):
            n1 = min(n, n0 + n_chunk)
            acc = jnp.dot(xn[g], w_ref[:, n0:n1], preferred_element_type=F32)
            if scale != 1.0:
                acc = acc * scale
            o_ref[:, n0:n1] = acc.astype(o_ref.dtype)


def _norm_matmul(x, gains, ws, plan):
    t, d = x.shape
    tm = TOKEN_TILE
    in_specs = [pl.BlockSpec((tm, d), lambda i: (i, 0)),
                pl.BlockSpec(gains.shape, lambda i: (0, 0))]
    in_specs += [pl.BlockSpec(w.shape, lambda i: (0, 0)) for w in ws]
    out_specs = [pl.BlockSpec((tm, w.shape[1]), lambda i: (i, 0)) for w in ws]
    out_shape = [jax.ShapeDtypeStruct((t, w.shape[1]), p[2]) for w, p in zip(ws, plan)]
    return pl.pallas_call(
        functools.partial(_norm_matmul_kernel, tuple(plan), 512),
        grid=(t // tm,), in_specs=in_specs, out_specs=out_specs, out_shape=out_shape,
        compiler_params=_params("parallel"), name="norm_matmul",
    )(x, gains, *ws)


def _gdn_kernel(n_heads, ts, qkv_ref, z_ref, ab_ref, conv_ref, alog_ref, dtb_ref, onorm_ref,
                o_ref, xpad_ref, state_ref, g_ref, beta_ref):
    c = GDN_CHUNK
    dk = GDN_HEAD_DIM
    tb = pl.program_id(1)
    n_ch = ts // c
    qkv_dim = qkv_ref.shape[-1]

    @pl.when(tb == 0)
    def _():
        state_ref[...] = jnp.zeros_like(state_ref)
        xpad_ref[0:GDN_HALO, :] = jnp.zeros((GDN_HALO, qkv_dim), xpad_ref.dtype)

    xpad_ref[GDN_HALO:GDN_HALO + ts, :] = qkv_ref[0]

    ab = ab_ref[0]
    g_all = -jnp.exp(alog_ref[...]) * _softplus(ab + dtb_ref[...])
    beta_all = _sigmoid(ab)
    rin = lax.broadcasted_iota(jnp.int32, (ts, LANES), 0) & (c - 1)
    for h in range(n_heads):
        gh = jnp.broadcast_to(g_all[:, h:h + 1], (ts, LANES))
        shift = 1
        while shift < c:
            gh = gh + jnp.where(rin >= shift, pltpu.roll(gh, shift, 0), 0.0)
            shift *= 2
        g_ref[h] = gh
        beta_ref[h] = jnp.broadcast_to(beta_all[:, n_heads + h:n_heads + h + 1], (ts, LANES))

    row = lax.broadcasted_iota(jnp.int32, (c, c), 0)
    col = lax.broadcasted_iota(jnp.int32, (c, c), 1)
    incl = row >= col
    strict = row > col

    srow = lax.broadcasted_iota(jnp.int32, (GDN_CONV * c, c + GDN_HALO), 0)
    scol = lax.broadcasted_iota(jnp.int32, (GDN_CONV * c, c + GDN_HALO), 1)
    shift_sel = jnp.where(scol == (srow & (c - 1)) + (srow >> (c.bit_length() - 1)) + (GDN_HALO - GDN_CONV + 1),
                          1.0, 0.0).astype(BF16)

    def conv_silu(r0, c0):
        win = xpad_ref[pl.ds(r0, c + GDN_HALO), c0:c0 + LANES]
        taps = jnp.dot(shift_sel, win, preferred_element_type=F32)
        acc = taps[0:c] * conv_ref[0:1, c0:c0 + LANES]
        for j in range(1, GDN_CONV):
            acc = acc + taps[j * c:(j + 1) * c] * conv_ref[j:j + 1, c0:c0 + LANES]
        return _silu(acc)

    def l2n(t):
        return t * lax.rsqrt(jnp.sum(t * t, axis=-1, keepdims=True) + NORM_EPS)

    heads = range(n_heads)
    eye = (row == col).astype(F32)
    same_blk = (row >> 4) == (col >> 4)

    items = range(GDN_UNROLL * n_heads)

    def chunk_body(n, carry):
        r_base = n * (GDN_UNROLL * c)
        r0 = [pl.multiple_of(r_base + (i // n_heads) * c, c) for i in items]
        hd = [i % n_heads for i in items]
        q = [l2n(conv_silu(r0[i], hd[i] * dk)) * (dk ** -0.5) for i in items]
        k = [l2n(conv_silu(r0[i], (n_heads + hd[i]) * dk)) for i in items]
        v = [conv_silu(r0[i], (2 * n_heads + hd[i]) * dk) for i in items]
        gc = [g_ref[hd[i], pl.ds(r0[i], c), :] for i in items]
        beta = [beta_ref[hd[i], pl.ds(r0[i], c), :] for i in items]
        kb = [k[i] * beta[i] for i in items]
        kk = [_dot_nt(kb[i], k[i]) for i in items]
        qk = [_dot_nt(q[i], k[i]) for i in items]
        decay = []
        for i in items:
            diff = gc[i][:, 0:c] - gc[i].T[0:c, :]
            decay.append(jnp.where(incl, jnp.exp(jnp.where(incl, diff, 0.0)), 0.0))
        lm = [jnp.where(strict, kk[i] * decay[i], 0.0) for i in items]
        attn = [jnp.where(incl, qk[i] * decay[i], 0.0) for i in items]
        ld = [jnp.where(same_blk, lm[i], 0.0) for i in items]
        lo = [lm[i] - ld[i] for i in items]
        p = [eye - ld[i] for i in items]
        sq = [_dot(ld[i], ld[i]) for i in items]
        for _ in range(2):
            pn = [_dot(p[i], sq[i]) for i in items]
            sq2 = [_dot(sq[i], sq[i]) for i in items]
            p = [p[i] + pn[i] for i in items]
            sq = sq2
        dinv = [p[i] + _dot(p[i], sq[i]) for i in items]
        m = [_dot(dinv[i], lo[i]) for i in items]
        m2 = [_dot(m[i], m[i]) for i in items]
        r = [(eye - m[i]) + _dot(eye - m[i], m2[i]) for i in items]
        tmat = [_dot(r[i], dinv[i]) for i in items]
        eg = [jnp.exp(gc[i]) for i in items]
        rhs = [jnp.concatenate([v[i] * beta[i], kb[i] * eg[i]], axis=1) for i in items]
        uw = [_dot(tmat[i], rhs[i]) for i in items]
        lhs = [jnp.concatenate([uw[i][:, dk:], q[i] * eg[i]], axis=0) for i in items]
        g_last = [gc[i][c - 1:c, :] for i in items]
        k_dec = [k[i] * jnp.exp(g_last[i] - gc[i]) for i in items]
        s = [state_ref[h] for h in heads]
        for ci in range(GDN_UNROLL):
            it = [ci * n_heads + h for h in heads]
            ws_qs = [_dot(lhs[it[h]], s[h]) for h in heads]
            v_new = [uw[it[h]][:, :dk] - ws_qs[h][:c] for h in heads]
            av = [_dot(attn[it[h]], v_new[h]) for h in heads]
            kv = [_dot_tn(k_dec[it[h]], v_new[h]) for h in heads]
            s = [s[h] * jnp.exp(g_last[it[h]]) + kv[h] for h in heads]
            for h in heads:
                o = ws_qs[h][c:] + av[h]
                o = o * lax.rsqrt(jnp.mean(o * o, axis=-1, keepdims=True) + NORM_EPS) * onorm_ref[...]
                zz = z_ref[0, pl.ds(r0[it[h]], c), h * dk:(h + 1) * dk].astype(F32)
                o_ref[0, pl.ds(r0[it[h]], c), h * dk:(h + 1) * dk] = (o * _silu(zz)).astype(o_ref.dtype)
        for h in heads:
            state_ref[h] = s[h]
        return carry

    lax.fori_loop(0, n_ch // GDN_UNROLL, chunk_body, 0)
    xpad_ref[0:GDN_HALO, :] = xpad_ref[ts:ts + GDN_HALO, :]


def _gated_deltanet_core(qkv, z, ab, conv_w, alog, dtb, onorm):
    b, s, qkv_dim = qkv.shape
    n_heads = GDN_HEADS
    ts = min(GDN_TIME_BLOCK, s)
    vd = z.shape[-1]
    kern = functools.partial(_gdn_kernel, n_heads, ts)
    return pl.pallas_call(
        kern,
        grid=(b, s // ts),
        in_specs=[
            pl.BlockSpec((1, ts, qkv_dim), lambda i, j: (i, j, 0)),
            pl.BlockSpec((1, ts, vd), lambda i, j: (i, j, 0)),
            pl.BlockSpec((1, ts, LANES), lambda i, j: (i, j, 0)),
            pl.BlockSpec(conv_w.shape, lambda i, j: (0, 0)),
            pl.BlockSpec((1, LANES), lambda i, j: (0, 0)),
            pl.BlockSpec((1, LANES), lambda i, j: (0, 0)),
            pl.BlockSpec((1, GDN_HEAD_DIM), lambda i, j: (0, 0)),
        ],
        out_specs=pl.BlockSpec((1, ts, vd), lambda i, j: (i, j, 0)),
        out_shape=jax.ShapeDtypeStruct((b, s, vd), BF16),
        scratch_shapes=[
            pltpu.VMEM((ts + GDN_HALO, qkv_dim), BF16),
            pltpu.VMEM((n_heads, GDN_HEAD_DIM, GDN_HEAD_DIM), F32),
            pltpu.VMEM((n_heads, ts, LANES), F32),
            pltpu.VMEM((n_heads, ts, LANES), F32),
        ],
        compiler_params=_params("parallel", "arbitrary"), name="gdn_delta_rule",
    )(qkv, z, ab, conv_w, alog, dtb, onorm)


def _sb_attn_kernel(q_ref, kt_ref, v_ref, o_ref, acc_ref, cs_ref, z_ref, p_ref):
    blk = SB_BLOCK
    dh = SB_HEAD_DIM
    grp = SB_GROUP
    i = pl.program_id(2)
    rows = grp * blk
    n_pairs = 1 + i // 2
    qb = q_ref[0]
    q4 = jnp.concatenate([qb[:, g * dh:(g + 1) * dh] for g in range(grp)], axis=0)
    krow = lax.broadcasted_iota(jnp.int32, (blk, blk), 0)
    kcol = lax.broadcasted_iota(jnp.int32, (blk, blk), 1)
    neg_suffix = jnp.where(krow >= kcol, -1.0, 0.0).astype(BF16)

    def pair_blocks(k):
        ja = i - 2 * k
        return jnp.maximum(ja, 0), jnp.maximum(ja - 1, 0), (ja >= 1).astype(F32)

    def logits(k):
        ja, jb, _ = pair_blocks(k)
        z_ref[0] = jnp.dot(q4, kt_ref[0, 0, ja], preferred_element_type=F32)
        z_ref[1] = jnp.dot(q4, kt_ref[0, 0, jb], preferred_element_type=F32)

    def weights(mask_a):
        masks = [mask_a, None]
        zs = [z_ref[0], z_ref[1]]
        sps = []
        for z, mask in zip(zs, masks):
            sp = jnp.maximum(z, 0.0) + jnp.log(1.0 + jnp.exp2(jnp.abs(z) * -LOG2E))
            if mask is not None:
                sp = jnp.where(mask, sp, 0.0)
            sps.append(sp)
        sufs = [jnp.dot(sp.astype(BF16), neg_suffix, preferred_element_type=F32) for sp in sps]
        rss = [jnp.sum(sp, axis=-1, keepdims=True) for sp in sps]
        cs = cs_ref[...]
        for n, (z, suf, rs, mask) in enumerate(zip(zs, sufs, rss, masks)):
            p = jnp.exp2(((z + suf) - cs) * LOG2E)
            if mask is not None:
                p = jnp.where(mask, p, 0.0)
            p_ref[n] = p.astype(BF16)
            cs = cs + rs
        cs_ref[...] = cs

    def weighted_values(k):
        ja, jb, valid_b = pair_blocks(k)
        vb = (v_ref[0, 0, jb].astype(F32) * valid_b).astype(BF16)
        acc_ref[...] += (jnp.dot(p_ref[0], v_ref[0, 0, ja], preferred_element_type=F32)
                         + jnp.dot(p_ref[1], vb, preferred_element_type=F32))

    acc_ref[...] = jnp.zeros_like(acc_ref)
    cs_ref[...] = jnp.zeros_like(cs_ref)
    t_in = lax.broadcasted_iota(jnp.int32, (rows, blk), 0) & (blk - 1)
    s_in = lax.broadcasted_iota(jnp.int32, (rows, blk), 1)
    logits(0)
    weights(s_in < t_in)
    logits(1)

    def body(k, c):
        weighted_values(k - 1)
        weights(None)
        logits(k + 1)
        return c

    lax.fori_loop(1, n_pairs, body, 0)
    weighted_values(n_pairs - 1)
    acc = acc_ref[...]
    for g in range(grp):
        o_ref[0, :, g * dh:(g + 1) * dh] = acc[g * blk:(g + 1) * blk, :].astype(o_ref.dtype)


def _stick_breaking_attention(q, kt, v):
    b, s, qd = q.shape
    nb = s // SB_BLOCK
    gw = SB_GROUP * SB_HEAD_DIM
    rows = SB_GROUP * SB_BLOCK
    return pl.pallas_call(
        _sb_attn_kernel,
        grid=(b, SB_KV_HEADS, nb),
        in_specs=[
            pl.BlockSpec((1, SB_BLOCK, gw), lambda bi, g, i: (bi, i, g)),
            pl.BlockSpec((1, 1, nb, SB_HEAD_DIM, SB_BLOCK), lambda bi, g, i: (bi, g, 0, 0, 0)),
            pl.BlockSpec((1, 1, nb, SB_BLOCK, SB_HEAD_DIM), lambda bi, g, i: (bi, g, 0, 0, 0)),
        ],
        out_specs=pl.BlockSpec((1, SB_BLOCK, gw), lambda bi, g, i: (bi, i, g)),
        out_shape=jax.ShapeDtypeStruct((b, s, qd), BF16),
        scratch_shapes=[pltpu.VMEM((rows, SB_HEAD_DIM), F32), pltpu.VMEM((rows, LANES), F32),
                        pltpu.VMEM((2, rows, SB_BLOCK), F32), pltpu.VMEM((2, rows, SB_BLOCK), BF16)],
        compiler_params=_params("parallel", "parallel", "arbitrary"), name="stick_breaking_attention",
    )(q, kt, v)


def _out_router_kernel(mix_ref, h_ref, wout_ref, gain_ref, wrh_ref, wrl_ref, br_ref,
                       h1_ref, xn_ref, rw_ref, ri_ref, cnt_ref, run_ref):
    h1 = h_ref[...] + jnp.dot(mix_ref[...], wout_ref[...], preferred_element_type=F32)
    h1_ref[...] = h1
    xn = _rms(h1, gain_ref[...])
    xh = xn.astype(BF16)
    xn_ref[...] = _pack_halves(xn)
    xl = (xn - xh.astype(F32)).astype(BF16)
    logits = (jnp.dot(xh, wrh_ref[...], preferred_element_type=F32)
              + jnp.dot(xl, wrh_ref[...], preferred_element_type=F32)
              + jnp.dot(xh, wrl_ref[...], preferred_element_type=F32)
              + br_ref[...])
    lane = lax.broadcasted_iota(jnp.int32, logits.shape, 1).astype(F32)
    neg = jnp.float32(-jnp.inf)
    big = jnp.float32(1e9)
    gl = jnp.where(lane < MOE_GROUPS, logits, neg)
    gmax = jnp.max(gl, axis=-1, keepdims=True)
    gidx = jnp.min(jnp.where(gl == gmax, lane, big), axis=-1, keepdims=True)
    gp = 1.0 / jnp.sum(jnp.exp(gl - gmax), axis=-1, keepdims=True)
    lo = MOE_GROUPS + gidx * MOE_EXPERTS_PER_GROUP
    el = jnp.where((lane >= lo) & (lane < lo + MOE_EXPERTS_PER_GROUP), logits, neg)
    m1 = jnp.max(el, axis=-1, keepdims=True)
    i1 = jnp.min(jnp.where(el == m1, lane, big), axis=-1, keepdims=True)
    el2 = jnp.where(lane == i1, neg, el)
    m2 = jnp.max(el2, axis=-1, keepdims=True)
    i2 = jnp.min(jnp.where(el2 == m2, lane, big), axis=-1, keepdims=True)
    e2 = jnp.exp(m2 - m1)
    w1 = gp / (1.0 + e2)
    w2 = gp * e2 / (1.0 + e2)
    rw_ref[...] = jnp.where(lane == 0, w1, jnp.where(lane == 1, w2, 0.0))

    @pl.when(pl.program_id(0) == 0)
    def _():
        run_ref[...] = jnp.zeros_like(run_ref)

    tm = logits.shape[0]
    trow = lax.broadcasted_iota(jnp.int32, (tm, tm), 0)
    tcol = lax.broadcasted_iota(jnp.int32, (tm, tm), 1)
    before = jnp.where(tcol < trow, 1.0, 0.0).astype(BF16)
    hot1 = lane == i1
    hot2 = lane == i2
    oh1 = jnp.where(hot1, 1.0, 0.0)
    oh2 = jnp.where(hot2, 1.0, 0.0)
    prior1 = jnp.dot(before, oh1.astype(BF16), preferred_element_type=F32)
    prior2 = jnp.dot(before, oh2.astype(BF16), preferred_element_type=F32)
    cnt1 = jnp.sum(oh1, axis=0, keepdims=True)
    cnt2 = jnp.sum(oh2, axis=0, keepdims=True)
    run = run_ref[...]
    rank1 = jnp.sum(jnp.where(hot1, prior1 + run, 0.0), axis=-1, keepdims=True)
    rank2 = jnp.sum(jnp.where(hot2, prior2 + (run + cnt1), 0.0), axis=-1, keepdims=True)
    run = run + cnt1 + cnt2
    run_ref[...] = run
    cnt_ref[...] = run
    ri = jnp.where(lane == 0, i1 - MOE_GROUPS, jnp.where(lane == 1, i2 - MOE_GROUPS,
                   jnp.where(lane == 2, rank1, jnp.where(lane == 3, rank2, 0.0))))
    ri_ref[...] = ri.T[0:ROUTE_COLS, :].astype(jnp.int32)


def _out_router(mix, h, w_out, gain, w_router, b_router):
    t, d = h.shape
    tm = TOKEN_TILE
    kd = mix.shape[1]
    row = lambda i: (i, 0)
    fix = lambda i: (0, 0)
    wr_hi = w_router.astype(BF16)
    wr_lo = (w_router - wr_hi.astype(F32)).astype(BF16)
    return pl.pallas_call(
        _out_router_kernel,
        grid=(t // tm,),
        in_specs=[pl.BlockSpec((tm, kd), row), pl.BlockSpec((tm, d), row),
                  pl.BlockSpec(w_out.shape, fix), pl.BlockSpec((1, d), fix),
                  pl.BlockSpec(w_router.shape, fix), pl.BlockSpec(w_router.shape, fix),
                  pl.BlockSpec((1, LANES), fix)],
        out_specs=[pl.BlockSpec((tm, d), row), pl.BlockSpec((tm, d // 2), row),
                   pl.BlockSpec((tm, LANES), row), pl.BlockSpec((ROUTE_COLS, tm), lambda i: (0, i)),
                   pl.BlockSpec((1, LANES), fix)],
        out_shape=[jax.ShapeDtypeStruct((t, d), F32), jax.ShapeDtypeStruct((t, d // 2), jnp.uint32),
                   jax.ShapeDtypeStruct((t, LANES), F32), jax.ShapeDtypeStruct((ROUTE_COLS, t), jnp.int32),
                   jax.ShapeDtypeStruct((1, LANES), F32)],
        scratch_shapes=[pltpu.VMEM((1, LANES), F32)],
        compiler_params=_params("arbitrary"), name="out_proj_router",
    )(mix, h, w_out, gain, wr_hi, wr_lo, b_router)


def _expert_ffn_kernel(te_ref, nu_ref, xs_ref, wg_ref, wu_ref, wd_ref, ys_ref):
    i = pl.program_id(0)

    @pl.when(i < nu_ref[0])
    def _():
        xa, xb = _unpack_halves(xs_ref[...])
        xa = xa.astype(BF16)
        xb = xb.astype(BF16)
        half = xa.shape[1]
        wg = wg_ref[0].astype(BF16)
        wu = wu_ref[0].astype(BF16)
        hg = (jnp.dot(xa, wg[:half], preferred_element_type=F32)
              + jnp.dot(xb, wg[half:], preferred_element_type=F32))
        hu = (jnp.dot(xa, wu[:half], preferred_element_type=F32)
              + jnp.dot(xb, wu[half:], preferred_element_type=F32))
        hh = _silu(hg) * hu
        ys_ref[...] = _pack_halves(jnp.dot(hh.astype(BF16), wd_ref[0].astype(BF16),
                                           preferred_element_type=F32))

    @pl.when(i >= nu_ref[0])
    def _():
        ys_ref[...] = jnp.zeros_like(ys_ref)


def _expert_ffn(tile_expert, n_used, xs, w_gate, w_up, w_down):
    r, dw = xs.shape
    d = 2 * dw
    tm = FFN_TILE
    f = w_gate.shape[-1]
    grid_spec = pltpu.PrefetchScalarGridSpec(
        num_scalar_prefetch=2,
        grid=(r // tm,),
        in_specs=[
            pl.BlockSpec((tm, dw), lambda i, te, nu: (i, 0)),
            pl.BlockSpec((1, d, f), lambda i, te, nu: (te[i], 0, 0)),
            pl.BlockSpec((1, d, f), lambda i, te, nu: (te[i], 0, 0)),
            pl.BlockSpec((1, f, d), lambda i, te, nu: (te[i], 0, 0)),
        ],
        out_specs=pl.BlockSpec((tm, dw), lambda i, te, nu: (i, 0)),
    )
    return pl.pallas_call(
        _expert_ffn_kernel, grid_spec=grid_spec,
        out_shape=jax.ShapeDtypeStruct((r, dw), jnp.uint32),
        compiler_params=_params("arbitrary"), name="expert_ffn",
    )(tile_expert, n_used, xs, w_gate, w_up, w_down)


def _ple_kernel(final, h_ref, m0_ref, m1_ref, rw_ref, p_ref, gain_ref, wg_ref, wp_ref, fgain_ref, o_ref):
    rw = rw_ref[...]
    a0, b0 = _unpack_halves(m0_ref[...])
    a1, b1 = _unpack_halves(m1_ref[...])
    w0 = rw[:, 0:1]
    w1 = rw[:, 1:2]
    moe = jnp.concatenate([w0 * a0 + w1 * a1, w0 * b0 + w1 * b1], axis=1)
    h2 = h_ref[...] + moe
    xn = _rms(h2, gain_ref[...]).astype(BF16)
    gate = _sigmoid(jnp.dot(xn, wg_ref[...], preferred_element_type=F32))
    emb = jnp.dot(p_ref[...].astype(BF16), wp_ref[...], preferred_element_type=F32)
    h3 = h2 + gate * emb
    if final:
        h3 = _rms(h3, fgain_ref[...])
    o_ref[...] = h3


def _ple(h, m01, rw, p, layer, gain, w_gate, w_proj, final_gain, final):
    t, d = h.shape
    tm = TOKEN_TILE
    pd = p.shape[1]
    nt = t // tm
    row = lambda i: (i, 0)
    fix = lambda i: (0, 0)
    return pl.pallas_call(
        functools.partial(_ple_kernel, final),
        grid=(nt,),
        in_specs=[pl.BlockSpec((tm, d), row), pl.BlockSpec((tm, d // 2), row),
                  pl.BlockSpec((tm, d // 2), lambda i: (i + nt, 0)),
                  pl.BlockSpec((tm, LANES), row), pl.BlockSpec((tm, pd), lambda i: (i + layer * nt, 0)),
                  pl.BlockSpec((1, d), fix),
                  pl.BlockSpec(w_gate.shape, fix), pl.BlockSpec(w_proj.shape, fix),
                  pl.BlockSpec((1, d), fix)],
        out_specs=pl.BlockSpec((tm, d), row),
        out_shape=jax.ShapeDtypeStruct((t, d), F32),
        compiler_params=_params("parallel"), name="moe_residual_ple",
    )(h, m01, m01, rw, p, gain, w_gate, w_proj, final_gain)


def _routing_tables(ri, counts):
    t = ri.shape[1]
    tm = FFN_TILE
    n_rows = 2 * t + MOE_N_EXPERTS * tm
    padded = ((counts + tm - 1) // tm) * tm
    ends = jnp.cumsum(padded)
    starts = ends - padded
    ids = ri[0:2]
    offs = jnp.zeros_like(ids)
    for e in range(MOE_N_EXPERTS):
        offs = jnp.where(ids == e, starts[e], offs)
    dest = (ri[2:4] + offs).reshape(-1)
    src_tok = _sc_invert_slots(dest, n_rows, t)
    tile_start = jnp.arange(n_rows // tm, dtype=jnp.int32) * tm
    tile_expert = jnp.minimum(jnp.sum((ends[None, :] <= tile_start[:, None]).astype(jnp.int32), axis=1),
                              MOE_N_EXPERTS - 1)
    n_used = (ends[-1] // tm).astype(jnp.int32).reshape(1)
    return src_tok, tile_expert, n_used, dest


def _moe_ple(layer, mix, h, w_out, moe_gain, w_router, b_router, w_gate, w_up, w_down,
             p, ple_gain, ple_w_gate, ple_w_proj, final_gain, final):
    h1, xn, rw, ri, cnt = _out_router(mix, h, w_out, moe_gain, w_router, b_router)
    counts = cnt[0, MOE_GROUPS:MOE_GROUPS + MOE_N_EXPERTS].astype(jnp.int32)
    src_tok, tile_expert, n_used, dest = _routing_tables(ri, counts)
    xs = _sc_gather_rows(xn, src_tok)
    ys = _expert_ffn(tile_expert + layer * MOE_N_EXPERTS, n_used, xs, w_gate, w_up, w_down)
    m01 = _sc_gather_rows(ys, dest)
    return _ple(h1, m01, rw, p, layer, ple_gain, ple_w_gate, ple_w_proj, final_gain, final)


def _pad_lanes(v):
    return jnp.zeros((1, LANES), F32).at[0, :v.shape[0]].set(v.astype(F32))


def kernel(x, p, attn_norm, moe_norm, ple_norm, gdn_w_in, gdn_conv, gdn_a_log, gdn_dt_bias, gdn_o_norm, gdn_w_out, kv_norm, w_kv, sb_w_q, sb_w_out, moe_w_group, moe_b_group, moe_w_expert, moe_b_expert, moe_w_gate, moe_w_up, moe_w_down, ple_w_gate, ple_w_proj, final_norm):
    d = x.shape[-1]
    kd = GDN_HEADS * GDN_HEAD_DIM
    qkv_dim = 3 * kd

    def router_params(i):
        wr = jnp.zeros((d, LANES), F32)
        wr = wr.at[:, :MOE_GROUPS].set(moe_w_group[i])
        wr = wr.at[:, MOE_GROUPS:MOE_GROUPS + MOE_N_EXPERTS].set(moe_w_expert[i])
        br = jnp.zeros((1, LANES), F32)
        br = br.at[0, :MOE_GROUPS].set(moe_b_group[i])
        br = br.at[0, MOE_GROUPS:MOE_GROUPS + MOE_N_EXPERTS].set(moe_b_expert[i])
        return wr, br

    experts = (moe_w_gate.reshape(-1, d, MOE_D_EXPERT), moe_w_up.reshape(-1, d, MOE_D_EXPERT),
               moe_w_down.reshape(-1, MOE_D_EXPERT, d))

    w_in = gdn_w_in[0]
    w_ab = jnp.zeros((d, LANES), F32).at[:, :2 * GDN_HEADS].set(w_in[:, qkv_dim + kd:])
    w_in_parts = [w_in[:, :qkv_dim].astype(BF16), w_in[:, qkv_dim:qkv_dim + kd].astype(BF16), w_ab.astype(BF16)]
    dtb = jnp.zeros((1, LANES), F32).at[0, :GDN_HEADS].set(gdn_dt_bias[0])
    alog = _pad_lanes(gdn_a_log[0])
    routers = [router_params(0), router_params(1)]
    mix_w_out = [gdn_w_out[0].astype(BF16), sb_w_out[0].astype(BF16)]
    ple_wg = [ple_w_gate[0].astype(BF16), ple_w_gate[1].astype(BF16)]
    ple_wp = [ple_w_proj[0].astype(BF16), ple_w_proj[1].astype(BF16)]
    gains1 = jnp.stack([attn_norm[1], kv_norm], axis=0)
    w_q = sb_w_q[0].astype(BF16)
    w_kv_b = w_kv.astype(BF16)
    fgain = final_norm.reshape(1, d)

    b, s, _ = x.shape
    t = b * s
    h = x.reshape(t, d)
    p_all = p.reshape(-1, p.shape[-1])
    qkv, z, ab = _norm_matmul(h, attn_norm[0:1], w_in_parts,
                              [(0, 1.0, BF16), (0, 1.0, BF16), (0, 1.0, F32)])
    og = _gated_deltanet_core(
        qkv.reshape(b, s, qkv_dim), z.reshape(b, s, kd), ab.reshape(b, s, LANES),
        gdn_conv[0], alog, dtb, gdn_o_norm[0].reshape(1, GDN_HEAD_DIM))
    h = _moe_ple(0, og.reshape(t, kd), h, mix_w_out[0], moe_norm[0:1], *routers[0], *experts,
                 p_all, ple_norm[0:1], ple_wg[0], ple_wp[0], fgain, False)
    q, kv = _norm_matmul(h, gains1, [w_q, w_kv_b],
                         [(0, SB_HEAD_DIM ** -0.5, BF16), (1, 1.0, BF16)])
    nb = s // SB_BLOCK
    kvw = SB_KV_HEADS * SB_HEAD_DIM
    k5 = kv[:, :kvw].reshape(b, nb, SB_BLOCK, SB_KV_HEADS, SB_HEAD_DIM)
    v5 = kv[:, kvw:].reshape(b, nb, SB_BLOCK, SB_KV_HEADS, SB_HEAD_DIM)
    kt = k5.transpose(0, 3, 1, 4, 2)
    vv = v5.transpose(0, 3, 1, 2, 4)
    oa = _stick_breaking_attention(q.reshape(b, s, -1), kt, vv)
    out = _moe_ple(1, oa.reshape(t, -1), h, mix_w_out[1], moe_norm[1:2], *routers[1], *experts,
                   p_all, ple_norm[1:2], ple_wg[1], ple_wp[1], fgain, True)
    return out.reshape(b, s, d)
```

```python
import functools

import jax
import jax.numpy as jnp
from jax import lax
from jax.experimental import pallas as pl
from jax.experimental.pallas import tpu as pltpu
from jax.experimental.pallas import tpu_sc as plsc

NORM_EPS = 1e-6
LOG2E = 1.4426950408889634
LANES = 128
GDN_HEADS = 8
GDN_HEAD_DIM = 128
GDN_CONV = 4
GDN_CHUNK = 64
GDN_UNROLL = 4
GDN_HALO = 16
SB_Q_HEADS = 16
SB_KV_HEADS = 4
SB_GROUP = SB_Q_HEADS // SB_KV_HEADS
SB_HEAD_DIM = 64
SB_BLOCK = 128
MOE_GROUPS = 4
MOE_EXPERTS_PER_GROUP = 8
MOE_N_EXPERTS = MOE_GROUPS * MOE_EXPERTS_PER_GROUP
MOE_D_EXPERT = 256

VMEM_LIMIT = 56 * 1024 * 1024
TOKEN_TILE = 512
FFN_TILE = 512
GDN_TIME_BLOCK = 512
ROUTE_COLS = 8
SC_GATHER_ROWS = 64
SC_LANES = 16
SC_INDEX_CHUNK = 4096

F32 = jnp.float32
BF16 = jnp.bfloat16


def _params(*sem):
    return pltpu.CompilerParams(dimension_semantics=sem, vmem_limit_bytes=VMEM_LIMIT)


def _dot(a, b):
    return jnp.dot(a.astype(BF16), b.astype(BF16), preferred_element_type=F32)


def _dot_nt(a, b):
    return lax.dot_general(a.astype(BF16), b.astype(BF16), (((1,), (1,)), ((), ())),
                           preferred_element_type=F32)


def _dot_tn(a, b):
    return lax.dot_general(a.astype(BF16), b.astype(BF16), (((0,), (0,)), ((), ())),
                           preferred_element_type=F32)


def _dot_f32(a, b):
    return jnp.dot(a, b, precision=lax.Precision.HIGHEST, preferred_element_type=F32)


def _rms(x, gain):
    return x * lax.rsqrt(jnp.mean(x * x, axis=-1, keepdims=True) + NORM_EPS) * gain


def _silu(x):
    return x * (1.0 / (1.0 + jnp.exp(-x)))


def _sigmoid(x):
    return 1.0 / (1.0 + jnp.exp(-x))


def _softplus(x):
    return jnp.maximum(x, 0.0) + jnp.log(1.0 + jnp.exp(-jnp.abs(x)))


def _pack_halves(x):
    w = x.shape[1] // 2
    hi = pltpu.bitcast(x[:, :w].astype(BF16).astype(F32), jnp.uint32)
    lo = pltpu.bitcast(x[:, w:].astype(BF16).astype(F32), jnp.uint32)
    return (hi & jnp.uint32(0xFFFF0000)) | (lo >> 16)


def _unpack_halves(u):
    hi = pltpu.bitcast(u & jnp.uint32(0xFFFF0000), F32)
    lo = pltpu.bitcast(u << 16, F32)
    return hi, lo


def _sc_gather_rows(table, idx):
    _, w = table.shape
    m = idx.shape[0]
    info = plsc.get_sparse_core_info()
    nc, ns = info.num_cores, info.num_subcores
    rows = SC_GATHER_ROWS
    assert m % (nc * ns * rows * 2) == 0
    per_w = m // (nc * ns)
    n_pairs = per_w // (2 * rows)
    mesh = plsc.VectorSubcoreMesh(core_axis_name="c", subcore_axis_name="s")

    @functools.partial(
        pl.kernel, mesh=mesh,
        out_type=jax.ShapeDtypeStruct((m, w), table.dtype),
        scratch_types=[pltpu.VMEM((2, rows), jnp.int32),
                       pltpu.VMEM((2, rows, w), table.dtype),
                       pltpu.SemaphoreType.DMA((2,)),
                       pltpu.SemaphoreType.DMA((2,))],
    )
    def gather(table_hbm, idx_hbm, out_hbm, idx_v, rows_v, gsem, wsem):
        wid = lax.axis_index("s") * nc + lax.axis_index("c")
        base = wid * per_w

        def fetch(chunk, slot):
            off = pl.multiple_of(base + chunk * rows, rows)
            pltpu.sync_copy(idx_hbm.at[pl.ds(off, rows)], idx_v.at[slot])
            return pltpu.async_copy(table_hbm.at[idx_v.at[slot]], rows_v.at[slot], gsem.at[slot])

        def write(chunk, slot):
            off = pl.multiple_of(base + chunk * rows, rows)
            return pltpu.async_copy(rows_v.at[slot], out_hbm.at[pl.ds(off, rows)], wsem.at[slot])

        @pl.loop(0, n_pairs)
        def _(pi):
            g0 = fetch(2 * pi, 0)
            g1 = fetch(2 * pi + 1, 1)
            g0.wait()
            w0 = write(2 * pi, 0)
            g1.wait()
            w1 = write(2 * pi + 1, 1)
            w0.wait()
            w1.wait()

    return gather(table, idx)


def _sc_invert_slots(dest, n_rows, n_tokens):
    n_pairs = dest.shape[0]
    ch = SC_INDEX_CHUNK
    assert n_pairs % ch == 0 and n_rows % SC_LANES == 0
    nc = plsc.get_sparse_core_info().num_cores
    mesh = plsc.VectorSubcoreMesh(core_axis_name="c", subcore_axis_name="s")

    @functools.partial(
        pl.kernel, mesh=mesh,
        out_type=jax.ShapeDtypeStruct((n_rows,), jnp.int32),
        scratch_types=[pltpu.VMEM((n_rows,), jnp.int32), pltpu.VMEM((ch,), jnp.int32)],
        compiler_params=pltpu.CompilerParams(needs_layout_passes=False),
    )
    def invert(dest_hbm, out_hbm, tab_v, dest_v):
        wid = lax.axis_index("s") * nc + lax.axis_index("c")

        @pl.when(wid == 0)
        def _():
            lanes = lax.iota(jnp.int32, SC_LANES)

            @pl.loop(0, n_rows // SC_LANES)
            def _(i):
                tab_v[pl.ds(pl.multiple_of(i * SC_LANES, SC_LANES), SC_LANES)] = lax.rem(
                    i * SC_LANES + lanes, n_tokens)

            @pl.loop(0, n_pairs // ch)
            def _(c):
                pltpu.sync_copy(dest_hbm.at[pl.ds(pl.multiple_of(c * ch, ch), ch)], dest_v)

                @pl.loop(0, ch // SC_LANES)
                def _(j):
                    d = dest_v[pl.ds(pl.multiple_of(j * SC_LANES, SC_LANES), SC_LANES)]
                    pair = c * ch + j * SC_LANES + lanes
                    plsc.store_scatter(tab_v, [d], lax.rem(pair, n_tokens))

            pltpu.sync_copy(tab_v, out_hbm)

    return invert(dest)


def _norm_matmul_kernel(plan, n_chunk, x_ref, gains_ref, *refs):
    n_w = len(plan)
    w_refs, o_refs = refs[:n_w], refs[n_w:]
    x = x_ref[...]
    inv = lax.rsqrt(jnp.mean(x * x, axis=-1, keepdims=True) + NORM_EPS)
    xn = {}
    for (g, _, _) in plan:
        if g not in xn:
            xn[g] = (x * inv * gains_ref[g:g + 1, :]).astype(BF16)
    for (g, scale, _), w_ref, o_ref in zip(plan, w_refs, o_refs):
        n = w_ref.shape[1]
        for n0 in range(0, n, n_chunk):
            n1 = min(n, n0 + n_chunk)
            acc = jnp.dot(xn[g], w_ref[:, n0:n1], preferred_element_type=F32)
            if scale != 1.0:
                acc = acc * scale
            o_ref[:, n0:n1] = acc.astype(o_ref.dtype)


def _norm_matmul(x, gains, ws, plan):
    t, d = x.shape
    tm = TOKEN_TILE
    in_specs = [pl.BlockSpec((tm, d), lambda i: (i, 0)),
                pl.BlockSpec(gains.shape, lambda i: (0, 0))]
    in_specs += [pl.BlockSpec(w.shape, lambda i: (0, 0)) for w in ws]
    out_specs = [pl.BlockSpec((tm, w.shape[1]), lambda i: (i, 0)) for w in ws]
    out_shape = [jax.ShapeDtypeStruct((t, w.shape[1]), p[2]) for w, p in zip(ws, plan)]
    return pl.pallas_call(
        functools.partial(_norm_matmul_kernel, tuple(plan), 512),
        grid=(t // tm,), in_specs=in_specs, out_specs=out_specs, out_shape=out_shape,
        compiler_params=_params("parallel"), name="norm_matmul",
    )(x, gains, *ws)


def _gdn_kernel(n_heads, ts, qkv_ref, z_ref, ab_ref, conv_ref, alog_ref, dtb_ref, onorm_ref,
                o_ref, xpad_ref, state_ref, g_ref, beta_ref):
    c = GDN_CHUNK
    dk = GDN_HEAD_DIM
    tb = pl.program_id(1)
    n_ch = ts // c
    qkv_dim = qkv_ref.shape[-1]

    @pl.when(tb == 0)
    def _():
        state_ref[...] = jnp.zeros_like(state_ref)
        xpad_ref[0:GDN_HALO, :] = jnp.zeros((GDN_HALO, qkv_dim), xpad_ref.dtype)

    xpad_ref[GDN_HALO:GDN_HALO + ts, :] = qkv_ref[0]

    ab = ab_ref[0]
    g_all = -jnp.exp(alog_ref[...]) * _softplus(ab + dtb_ref[...])
    beta_all = _sigmoid(ab)
    rin = lax.broadcasted_iota(jnp.int32, (ts, LANES), 0) & (c - 1)
    for h in range(n_heads):
        gh = jnp.broadcast_to(g_all[:, h:h + 1], (ts, LANES))
        shift = 1
        while shift < c:
            gh = gh + jnp.where(rin >= shift, pltpu.roll(gh, shift, 0), 0.0)
            shift *= 2
        g_ref[h] = gh
        beta_ref[h] = jnp.broadcast_to(beta_all[:, n_heads + h:n_heads + h + 1], (ts, LANES))

    row = lax.broadcasted_iota(jnp.int32, (c, c), 0)
    col = lax.broadcasted_iota(jnp.int32, (c, c), 1)
    incl = row >= col
    strict = row > col

    srow = lax.broadcasted_iota(jnp.int32, (GDN_CONV * c, c + GDN_HALO), 0)
    scol = lax.broadcasted_iota(jnp.int32, (GDN_CONV * c, c + GDN_HALO), 1)
    shift_sel = jnp.where(scol == (srow & (c - 1)) + (srow >> (c.bit_length() - 1)) + (GDN_HALO - GDN_CONV + 1),
                          1.0, 0.0).astype(BF16)

    def conv_silu(r0, c0):
        win = xpad_ref[pl.ds(r0, c + GDN_HALO), c0:c0 + LANES]
        taps = jnp.dot(shift_sel, win, preferred_element_type=F32)
        acc = taps[0:c] * conv_ref[0:1, c0:c0 + LANES]
        for j in range(1, GDN_CONV):
            acc = acc + taps[j * c:(j + 1) * c] * conv_ref[j:j + 1, c0:c0 + LANES]
        return _silu(acc)

    def l2n(t):
        return t * lax.rsqrt(jnp.sum(t * t, axis=-1, keepdims=True) + NORM_EPS)

    heads = range(n_heads)
    eye = (row == col).astype(F32)
    same_blk = (row >> 4) == (col >> 4)

    items = range(GDN_UNROLL * n_heads)

    def chunk_body(n, carry):
        r_base = n * (GDN_UNROLL * c)
        r0 = [pl.multiple_of(r_base + (i // n_heads) * c, c) for i in items]
        hd = [i % n_heads for i in items]
        q = [l2n(conv_silu(r0[i], hd[i] * dk)) * (dk ** -0.5) for i in items]
        k = [l2n(conv_silu(r0[i], (n_heads + hd[i]) * dk)) for i in items]
        v = [conv_silu(r0[i], (2 * n_heads + hd[i]) * dk) for i in items]
        gc = [g_ref[hd[i], pl.ds(r0[i], c), :] for i in items]
        beta = [beta_ref[hd[i], pl.ds(r0[i], c), :] for i in items]
        kb = [k[i] * beta[i] for i in items]
        kk = [_dot_nt(kb[i], k[i]) for i in items]
        qk = [_dot_nt(q[i], k[i]) for i in items]
        decay = []
        for i in items:
            diff = gc[i][:, 0:c] - gc[i].T[0:c, :]
            decay.append(jnp.where(incl, jnp.exp(jnp.where(incl, diff, 0.0)), 0.0))
        lm = [jnp.where(strict, kk[i] * decay[i], 0.0) for i in items]
        attn = [jnp.where(incl, qk[i] * decay[i], 0.0) for i in items]
        ld = [jnp.where(same_blk, lm[i], 0.0) for i in items]
        lo = [lm[i] - ld[i] for i in items]
        p = [eye - ld[i] for i in items]
        sq = [_dot(ld[i], ld[i]) for i in items]
        for _ in range(2):
            pn = [_dot(p[i], sq[i]) for i in items]
            sq2 = [_dot(sq[i], sq[i]) for i in items]
            p = [p[i] + pn[i] for i in items]
            sq = sq2
        dinv = [p[i] + _dot(p[i], sq[i]) for i in items]
        m = [_dot(dinv[i], lo[i]) for i in items]
        m2 = [_dot(m[i], m[i]) for i in items]
        r = [(eye - m[i]) + _dot(eye - m[i], m2[i]) for i in items]
        tmat = [_dot(r[i], dinv[i]) for i in items]
        eg = [jnp.exp(gc[i]) for i in items]
        rhs = [jnp.concatenate([v[i] * beta[i], kb[i] * eg[i]], axis=1) for i in items]
        uw = [_dot(tmat[i], rhs[i]) for i in items]
        lhs = [jnp.concatenate([uw[i][:, dk:], q[i] * eg[i]], axis=0) for i in items]
        g_last = [gc[i][c - 1:c, :] for i in items]
        k_dec = [k[i] * jnp.exp(g_last[i] - gc[i]) for i in items]
        s = [state_ref[h] for h in heads]
        for ci in range(GDN_UNROLL):
            it = [ci * n_heads + h for h in heads]
            ws_qs = [_dot(lhs[it[h]], s[h]) for h in heads]
            v_new = [uw[it[h]][:, :dk] - ws_qs[h][:c] for h in heads]
            av = [_dot(attn[it[h]], v_new[h]) for h in heads]
            kv = [_dot_tn(k_dec[it[h]], v_new[h]) for h in heads]
            s = [s[h] * jnp.exp(g_last[it[h]]) + kv[h] for h in heads]
            for h in heads:
                o = ws_qs[h][c:] + av[h]
                o = o * lax.rsqrt(jnp.mean(o * o, axis=-1, keepdims=True) + NORM_EPS) * onorm_ref[...]
                zz = z_ref[0, pl.ds(r0[it[h]], c), h * dk:(h + 1) * dk].astype(F32)
                o_ref[0, pl.ds(r0[it[h]], c), h * dk:(h + 1) * dk] = (o * _silu(zz)).astype(o_ref.dtype)
        for h in heads:
            state_ref[h] = s[h]
        return carry

    lax.fori_loop(0, n_ch // GDN_UNROLL, chunk_body, 0)
    xpad_ref[0:GDN_HALO, :] = xpad_ref[ts:ts + GDN_HALO, :]


def _gated_deltanet_core(qkv, z, ab, conv_w, alog, dtb, onorm):
    b, s, qkv_dim = qkv.shape
    n_heads = GDN_HEADS
    ts = min(GDN_TIME_BLOCK, s)
    vd = z.shape[-1]
    kern = functools.partial(_gdn_kernel, n_heads, ts)
    return pl.pallas_call(
        kern,
        grid=(b, s // ts),
        in_specs=[
            pl.BlockSpec((1, ts, qkv_dim), lambda i, j: (i, j, 0)),
            pl.BlockSpec((1, ts, vd), lambda i, j: (i, j, 0)),
            pl.BlockSpec((1, ts, LANES), lambda i, j: (i, j, 0)),
            pl.BlockSpec(conv_w.shape, lambda i, j: (0, 0)),
            pl.BlockSpec((1, LANES), lambda i, j: (0, 0)),
            pl.BlockSpec((1, LANES), lambda i, j: (0, 0)),
            pl.BlockSpec((1, GDN_HEAD_DIM), lambda i, j: (0, 0)),
        ],
        out_specs=pl.BlockSpec((1, ts, vd), lambda i, j: (i, j, 0)),
        out_shape=jax.ShapeDtypeStruct((b, s, vd), BF16),
        scratch_shapes=[
            pltpu.VMEM((ts + GDN_HALO, qkv_dim), BF16),
            pltpu.VMEM((n_heads, GDN_HEAD_DIM, GDN_HEAD_DIM), F32),
            pltpu.VMEM((n_heads, ts, LANES), F32),
            pltpu.VMEM((n_heads, ts, LANES), F32),
        ],
        compiler_params=_params("parallel", "arbitrary"), name="gdn_delta_rule",
    )(qkv, z, ab, conv_w, alog, dtb, onorm)


def _sb_attn_kernel(q_ref, kt_ref, v_ref, o_ref, acc_ref, cs_ref, z_ref, p_ref):
    blk = SB_BLOCK
    dh = SB_HEAD_DIM
    grp = SB_GROUP
    i = pl.program_id(2)
    rows = grp * blk
    n_pairs = 1 + i // 2
    qb = q_ref[0]
    q4 = jnp.concatenate([qb[:, g * dh:(g + 1) * dh] for g in range(grp)], axis=0)
    krow = lax.broadcasted_iota(jnp.int32, (blk, blk), 0)
    kcol = lax.broadcasted_iota(jnp.int32, (blk, blk), 1)
    neg_suffix = jnp.where(krow >= kcol, -1.0, 0.0).astype(BF16)

    def pair_blocks(k):
        ja = i - 2 * k
        return jnp.maximum(ja, 0), jnp.maximum(ja - 1, 0), (ja >= 1).astype(F32)

    def logits(k):
        ja, jb, _ = pair_blocks(k)
        z_ref[0] = jnp.dot(q4, kt_ref[0, 0, ja], preferred_element_type=F32)
        z_ref[1] = jnp.dot(q4, kt_ref[0, 0, jb], preferred_element_type=F32)

    def weights(mask_a):
        masks = [mask_a, None]
        zs = [z_ref[0], z_ref[1]]
        sps = []
        for z, mask in zip(zs, masks):
            sp = jnp.maximum(z, 0.0) + jnp.log(1.0 + jnp.exp2(jnp.abs(z) * -LOG2E))
            if mask is not None:
                sp = jnp.where(mask, sp, 0.0)
            sps.append(sp)
        sufs = [jnp.dot(sp.astype(BF16), neg_suffix, preferred_element_type=F32) for sp in sps]
        rss = [jnp.sum(sp, axis=-1, keepdims=True) for sp in sps]
        cs = cs_ref[...]
        for n, (z, suf, rs, mask) in enumerate(zip(zs, sufs, rss, masks)):
            p = jnp.exp2(((z + suf) - cs) * LOG2E)
            if mask is not None:
                p = jnp.where(mask, p, 0.0)
            p_ref[n] = p.astype(BF16)
            cs = cs + rs
        cs_ref[...] = cs

    def weighted_values(k):
        ja, jb, valid_b = pair_blocks(k)
        vb = (v_ref[0, 0, jb].astype(F32) * valid_b).astype(BF16)
        acc_ref[...] += (jnp.dot(p_ref[0], v_ref[0, 0, ja], preferred_element_type=F32)
                         + jnp.dot(p_ref[1], vb, preferred_element_type=F32))

    acc_ref[...] = jnp.zeros_like(acc_ref)
    cs_ref[...] = jnp.zeros_like(cs_ref)
    t_in = lax.broadcasted_iota(jnp.int32, (rows, blk), 0) & (blk - 1)
    s_in = lax.broadcasted_iota(jnp.int32, (rows, blk), 1)
    logits(0)
    weights(s_in < t_in)
    logits(1)

    def step(k):
        weighted_values(k - 1)
        weights(None)
        logits(k + 1)

    def body2(m, c):
        step(2 * m + 1)
        step(2 * m + 2)
        return c

    def body1(k, c):
        step(k)
        return c

    n_double = (n_pairs - 1) // 2
    lax.fori_loop(0, n_double, body2, 0)
    lax.fori_loop(1 + 2 * n_double, n_pairs, body1, 0)
    weighted_values(n_pairs - 1)
    acc = acc_ref[...]
    for g in range(grp):
        o_ref[0, :, g * dh:(g + 1) * dh] = acc[g * blk:(g + 1) * blk, :].astype(o_ref.dtype)


def _stick_breaking_attention(q, kt, v):
    b, s, qd = q.shape
    nb = s // SB_BLOCK
    gw = SB_GROUP * SB_HEAD_DIM
    rows = SB_GROUP * SB_BLOCK
    return pl.pallas_call(
        _sb_attn_kernel,
        grid=(b, SB_KV_HEADS, nb),
        in_specs=[
            pl.BlockSpec((1, SB_BLOCK, gw), lambda bi, g, i: (bi, i, g)),
            pl.BlockSpec((1, 1, nb, SB_HEAD_DIM, SB_BLOCK), lambda bi, g, i: (bi, g, 0, 0, 0)),
            pl.BlockSpec((1, 1, nb, SB_BLOCK, SB_HEAD_DIM), lambda bi, g, i: (bi, g, 0, 0, 0)),
        ],
        out_specs=pl.BlockSpec((1, SB_BLOCK, gw), lambda bi, g, i: (bi, i, g)),
        out_shape=jax.ShapeDtypeStruct((b, s, qd), BF16),
        scratch_shapes=[pltpu.VMEM((rows, SB_HEAD_DIM), F32), pltpu.VMEM((rows, LANES), F32),
                        pltpu.VMEM((2, rows, SB_BLOCK), F32), pltpu.VMEM((2, rows, SB_BLOCK), BF16)],
        compiler_params=_params("parallel", "parallel", "arbitrary"), name="stick_breaking_attention",
    )(q, kt, v)


def _out_router_kernel(mix_ref, h_ref, wout_ref, gain_ref, wrh_ref, wrl_ref, br_ref,
                       h1_ref, xn_ref, rw_ref, ri_ref, cnt_ref, run_ref):
    h1 = h_ref[...] + jnp.dot(mix_ref[...], wout_ref[...], preferred_element_type=F32)
    h1_ref[...] = h1
    xn = _rms(h1, gain_ref[...])
    xh = xn.astype(BF16)
    xn_ref[...] = _pack_halves(xn)
    xl = (xn - xh.astype(F32)).astype(BF16)
    logits = (jnp.dot(xh, wrh_ref[...], preferred_element_type=F32)
              + jnp.dot(xl, wrh_ref[...], preferred_element_type=F32)
              + jnp.dot(xh, wrl_ref[...], preferred_element_type=F32)
              + br_ref[...])
    lane = lax.broadcasted_iota(jnp.int32, logits.shape, 1).astype(F32)
    neg = jnp.float32(-jnp.inf)
    big = jnp.float32(1e9)
    gl = jnp.where(lane < MOE_GROUPS, logits, neg)
    gmax = jnp.max(gl, axis=-1, keepdims=True)
    gidx = jnp.min(jnp.where(gl == gmax, lane, big), axis=-1, keepdims=True)
    gp = 1.0 / jnp.sum(jnp.exp(gl - gmax), axis=-1, keepdims=True)
    lo = MOE_GROUPS + gidx * MOE_EXPERTS_PER_GROUP
    el = jnp.where((lane >= lo) & (lane < lo + MOE_EXPERTS_PER_GROUP), logits, neg)
    m1 = jnp.max(el, axis=-1, keepdims=True)
    i1 = jnp.min(jnp.where(el == m1, lane, big), axis=-1, keepdims=True)
    el2 = jnp.where(lane == i1, neg, el)
    m2 = jnp.max(el2, axis=-1, keepdims=True)
    i2 = jnp.min(jnp.where(el2 == m2, lane, big), axis=-1, keepdims=True)
    e2 = jnp.exp(m2 - m1)
    w1 = gp / (1.0 + e2)
    w2 = gp * e2 / (1.0 + e2)
    rw_ref[...] = jnp.where(lane == 0, w1, jnp.where(lane == 1, w2, 0.0))

    @pl.when(pl.program_id(0) == 0)
    def _():
        run_ref[...] = jnp.zeros_like(run_ref)

    tm = logits.shape[0]
    trow = lax.broadcasted_iota(jnp.int32, (tm, tm), 0)
    tcol = lax.broadcasted_iota(jnp.int32, (tm, tm), 1)
    before = jnp.where(tcol < trow, 1.0, 0.0).astype(BF16)
    hot1 = lane == i1
    hot2 = lane == i2
    oh1 = jnp.where(hot1, 1.0, 0.0)
    oh2 = jnp.where(hot2, 1.0, 0.0)
    prior1 = jnp.dot(before, oh1.astype(BF16), preferred_element_type=F32)
    prior2 = jnp.dot(before, oh2.astype(BF16), preferred_element_type=F32)
    cnt1 = jnp.sum(oh1, axis=0, keepdims=True)
    cnt2 = jnp.sum(oh2, axis=0, keepdims=True)
    run = run_ref[...]
    rank1 = jnp.sum(jnp.where(hot1, prior1 + run, 0.0), axis=-1, keepdims=True)
    rank2 = jnp.sum(jnp.where(hot2, prior2 + (run + cnt1), 0.0), axis=-1, keepdims=True)
    run = run + cnt1 + cnt2
    run_ref[...] = run
    cnt_ref[...] = run
    ri = jnp.where(lane == 0, i1 - MOE_GROUPS, jnp.where(lane == 1, i2 - MOE_GROUPS,
                   jnp.where(lane == 2, rank1, jnp.where(lane == 3, rank2, 0.0))))
    ri_ref[...] = ri.T[0:ROUTE_COLS, :].astype(jnp.int32)


def _out_router(mix, h, w_out, gain, w_router, b_router):
    t, d = h.shape
    tm = TOKEN_TILE
    kd = mix.shape[1]
    row = lambda i: (i, 0)
    fix = lambda i: (0, 0)
    wr_hi = w_router.astype(BF16)
    wr_lo = (w_router - wr_hi.astype(F32)).astype(BF16)
    return pl.pallas_call(
        _out_router_kernel,
        grid=(t // tm,),
        in_specs=[pl.BlockSpec((tm, kd), row), pl.BlockSpec((tm, d), row),
                  pl.BlockSpec(w_out.shape, fix), pl.BlockSpec((1, d), fix),
                  pl.BlockSpec(w_router.shape, fix), pl.BlockSpec(w_router.shape, fix),
                  pl.BlockSpec((1, LANES), fix)],
        out_specs=[pl.BlockSpec((tm, d), row), pl.BlockSpec((tm, d // 2), row),
                   pl.BlockSpec((tm, LANES), row), pl.BlockSpec((ROUTE_COLS, tm), lambda i: (0, i)),
                   pl.BlockSpec((1, LANES), fix)],
        out_shape=[jax.ShapeDtypeStruct((t, d), F32), jax.ShapeDtypeStruct((t, d // 2), jnp.uint32),
                   jax.ShapeDtypeStruct((t, LANES), F32), jax.ShapeDtypeStruct((ROUTE_COLS, t), jnp.int32),
                   jax.ShapeDtypeStruct((1, LANES), F32)],
        scratch_shapes=[pltpu.VMEM((1, LANES), F32)],
        compiler_params=_params("arbitrary"), name="out_proj_router",
    )(mix, h, w_out, gain, wr_hi, wr_lo, b_router)


def _expert_ffn_kernel(te_ref, nu_ref, xs_ref, wg_ref, wu_ref, wd_ref, ys_ref):
    i = pl.program_id(0)

    @pl.when(i < nu_ref[0])
    def _():
        xa, xb = _unpack_halves(xs_ref[...])
        xa = xa.astype(BF16)
        xb = xb.astype(BF16)
        half = xa.shape[1]
        wg = wg_ref[0].astype(BF16)
        wu = wu_ref[0].astype(BF16)
        hg = (jnp.dot(xa, wg[:half], preferred_element_type=F32)
              + jnp.dot(xb, wg[half:], preferred_element_type=F32))
        hu = (jnp.dot(xa, wu[:half], preferred_element_type=F32)
              + jnp.dot(xb, wu[half:], preferred_element_type=F32))
        hh = _silu(hg) * hu
        ys_ref[...] = _pack_halves(jnp.dot(hh.astype(BF16), wd_ref[0].astype(BF16),
                                           preferred_element_type=F32))

    @pl.when(i >= nu_ref[0])
    def _():
        ys_ref[...] = jnp.zeros_like(ys_ref)


def _expert_ffn(tile_expert, n_used, xs, w_gate, w_up, w_down):
    r, dw = xs.shape
    d = 2 * dw
    tm = FFN_TILE
    f = w_gate.shape[-1]
    grid_spec = pltpu.PrefetchScalarGridSpec(
        num_scalar_prefetch=2,
        grid=(r // tm,),
        in_specs=[
            pl.BlockSpec((tm, dw), lambda i, te, nu: (i, 0)),
            pl.BlockSpec((1, d, f), lambda i, te, nu: (te[i], 0, 0)),
            pl.BlockSpec((1, d, f), lambda i, te, nu: (te[i], 0, 0)),
            pl.BlockSpec((1, f, d), lambda i, te, nu: (te[i], 0, 0)),
        ],
        out_specs=pl.BlockSpec((tm, dw), lambda i, te, nu: (i, 0)),
    )
    return pl.pallas_call(
        _expert_ffn_kernel, grid_spec=grid_spec,
        out_shape=jax.ShapeDtypeStruct((r, dw), jnp.uint32),
        compiler_params=_params("arbitrary"), name="expert_ffn",
    )(tile_expert, n_used, xs, w_gate, w_up, w_down)


def _ple_kernel(final, h_ref, m0_ref, m1_ref, rw_ref, p_ref, gain_ref, wg_ref, wp_ref, fgain_ref, o_ref):
    rw = rw_ref[...]
    a0, b0 = _unpack_halves(m0_ref[...])
    a1, b1 = _unpack_halves(m1_ref[...])
    w0 = rw[:, 0:1]
    w1 = rw[:, 1:2]
    moe = jnp.concatenate([w0 * a0 + w1 * a1, w0 * b0 + w1 * b1], axis=1)
    h2 = h_ref[...] + moe
    xn = _rms(h2, gain_ref[...]).astype(BF16)
    gate = _sigmoid(jnp.dot(xn, wg_ref[...], preferred_element_type=F32))
    emb = jnp.dot(p_ref[...].astype(BF16), wp_ref[...], preferred_element_type=F32)
    h3 = h2 + gate * emb
    if final:
        h3 = _rms(h3, fgain_ref[...])
    o_ref[...] = h3


def _ple(h, m01, rw, p, layer, gain, w_gate, w_proj, final_gain, final):
    t, d = h.shape
    tm = TOKEN_TILE
    pd = p.shape[1]
    nt = t // tm
    row = lambda i: (i, 0)
    fix = lambda i: (0, 0)
    return pl.pallas_call(
        functools.partial(_ple_kernel, final),
        grid=(nt,),
        in_specs=[pl.BlockSpec((tm, d), row), pl.BlockSpec((tm, d // 2), row),
                  pl.BlockSpec((tm, d // 2), lambda i: (i + nt, 0)),
                  pl.BlockSpec((tm, LANES), row), pl.BlockSpec((tm, pd), lambda i: (i + layer * nt, 0)),
                  pl.BlockSpec((1, d), fix),
                  pl.BlockSpec(w_gate.shape, fix), pl.BlockSpec(w_proj.shape, fix),
                  pl.BlockSpec((1, d), fix)],
        out_specs=pl.BlockSpec((tm, d), row),
        out_shape=jax.ShapeDtypeStruct((t, d), F32),
        compiler_params=_params("parallel"), name="moe_residual_ple",
    )(h, m01, m01, rw, p, gain, w_gate, w_proj, final_gain)


def _routing_tables(ri, counts):
    t = ri.shape[1]
    tm = FFN_TILE
    n_rows = 2 * t + MOE_N_EXPERTS * tm
    padded = ((counts + tm - 1) // tm) * tm
    ends = jnp.cumsum(padded)
    starts = ends - padded
    ids = ri[0:2]
    offs = jnp.zeros_like(ids)
    for e in range(MOE_N_EXPERTS):
        offs = jnp.where(ids == e, starts[e], offs)
    dest = (ri[2:4] + offs).reshape(-1)
    src_tok = _sc_invert_slots(dest, n_rows, t)
    tile_start = jnp.arange(n_rows // tm, dtype=jnp.int32) * tm
    tile_expert = jnp.minimum(jnp.sum((ends[None, :] <= tile_start[:, None]).astype(jnp.int32), axis=1),
                              MOE_N_EXPERTS - 1)
    n_used = (ends[-1] // tm).astype(jnp.int32).reshape(1)
    return src_tok, tile_expert, n_used, dest


def _moe_ple(layer, mix, h, w_out, moe_gain, w_router, b_router, w_gate, w_up, w_down,
             p, ple_gain, ple_w_gate, ple_w_proj, final_gain, final):
    h1, xn, rw, ri, cnt = _out_router(mix, h, w_out, moe_gain, w_router, b_router)
    counts = cnt[0, MOE_GROUPS:MOE_GROUPS + MOE_N_EXPERTS].astype(jnp.int32)
    src_tok, tile_expert, n_used, dest = _routing_tables(ri, counts)
    xs = _sc_gather_rows(xn, src_tok)
    ys = _expert_ffn(tile_expert + layer * MOE_N_EXPERTS, n_used, xs, w_gate, w_up, w_down)
    m01 = _sc_gather_rows(ys, dest)
    return _ple(h1, m01, rw, p, layer, ple_gain, ple_w_gate, ple_w_proj, final_gain, final)


def _pad_lanes(v):
    return jnp.zeros((1, LANES), F32).at[0, :v.shape[0]].set(v.astype(F32))


def kernel(x, p, attn_norm, moe_norm, ple_norm, gdn_w_in, gdn_conv, gdn_a_log, gdn_dt_bias, gdn_o_norm, gdn_w_out, kv_norm, w_kv, sb_w_q, sb_w_out, moe_w_group, moe_b_group, moe_w_expert, moe_b_expert, moe_w_gate, moe_w_up, moe_w_down, ple_w_gate, ple_w_proj, final_norm):
    d = x.shape[-1]
    kd = GDN_HEADS * GDN_HEAD_DIM
    qkv_dim = 3 * kd

    def router_params(i):
        wr = jnp.zeros((d, LANES), F32)
        wr = wr.at[:, :MOE_GROUPS].set(moe_w_group[i])
        wr = wr.at[:, MOE_GROUPS:MOE_GROUPS + MOE_N_EXPERTS].set(moe_w_expert[i])
        br = jnp.zeros((1, LANES), F32)
        br = br.at[0, :MOE_GROUPS].set(moe_b_group[i])
        br = br.at[0, MOE_GROUPS:MOE_GROUPS + MOE_N_EXPERTS].set(moe_b_expert[i])
        return wr, br

    experts = (moe_w_gate.reshape(-1, d, MOE_D_EXPERT), moe_w_up.reshape(-1, d, MOE_D_EXPERT),
               moe_w_down.reshape(-1, MOE_D_EXPERT, d))

    w_in = gdn_w_in[0]
    w_ab = jnp.zeros((d, LANES), F32).at[:, :2 * GDN_HEADS].set(w_in[:, qkv_dim + kd:])
    w_in_parts = [w_in[:, :qkv_dim].astype(BF16), w_in[:, qkv_dim:qkv_dim + kd].astype(BF16), w_ab.astype(BF16)]
    dtb = jnp.zeros((1, LANES), F32).at[0, :GDN_HEADS].set(gdn_dt_bias[0])
    alog = _pad_lanes(gdn_a_log[0])
    routers = [router_params(0), router_params(1)]
    mix_w_out = [gdn_w_out[0].astype(BF16), sb_w_out[0].astype(BF16)]
    ple_wg = [ple_w_gate[0].astype(BF16), ple_w_gate[1].astype(BF16)]
    ple_wp = [ple_w_proj[0].astype(BF16), ple_w_proj[1].astype(BF16)]
    gains1 = jnp.stack([attn_norm[1], kv_norm], axis=0)
    w_q = sb_w_q[0].astype(BF16)
    w_kv_b = w_kv.astype(BF16)
    fgain = final_norm.reshape(1, d)

    b, s, _ = x.shape
    t = b * s
    h = x.reshape(t, d)
    p_all = p.reshape(-1, p.shape[-1])
    qkv, z, ab = _norm_matmul(h, attn_norm[0:1], w_in_parts,
                              [(0, 1.0, BF16), (0, 1.0, BF16), (0, 1.0, F32)])
    og = _gated_deltanet_core(
        qkv.reshape(b, s, qkv_dim), z.reshape(b, s, kd), ab.reshape(b, s, LANES),
        gdn_conv[0], alog, dtb, gdn_o_norm[0].reshape(1, GDN_HEAD_DIM))
    h = _moe_ple(0, og.reshape(t, kd), h, mix_w_out[0], moe_norm[0:1], *routers[0], *experts,
                 p_all, ple_norm[0:1], ple_wg[0], ple_wp[0], fgain, False)
    q, kv = _norm_matmul(h, gains1, [w_q, w_kv_b],
                         [(0, SB_HEAD_DIM ** -0.5, BF16), (1, 1.0, BF16)])
    nb = s // SB_BLOCK
    kvw = SB_KV_HEADS * SB_HEAD_DIM
    k5 = kv[:, :kvw].reshape(b, nb, SB_BLOCK, SB_KV_HEADS, SB_HEAD_DIM)
    v5 = kv[:, kvw:].reshape(b, nb, SB_BLOCK, SB_KV_HEADS, SB_HEAD_DIM)
    kt = k5.transpose(0, 3, 1, 4, 2)
    vv = v5.transpose(0, 3, 1, 2, 4)
    oa = _stick_breaking_attention(q.reshape(b, s, -1), kt, vv)
    out = _moe_ple(1, oa.reshape(t, -1), h, mix_w_out[1], moe_norm[1:2], *routers[1], *experts,
                   p_all, ple_norm[1:2], ple_wg[1], ple_wp[1], fgain, True)
    return out.reshape(b, s, d)
```
